```python
import jax, jax.numpy as jnp
from jax import lax
import numpy as np

D_MODEL = 1024
BATCH = 16
SEQ = 2048
DEPTH = 1

HEAD_DIM = 64
MIX_TOTAL_HEADS = D_MODEL // HEAD_DIM
N_ATTN_HEADS = MIX_TOTAL_HEADS // 2
N_MIX_GROUPS = MIX_TOTAL_HEADS - N_ATTN_HEADS
GQA_GROUP = 4
N_KV_HEADS = N_ATTN_HEADS // GQA_GROUP
ATTN_WIDTH = N_ATTN_HEADS * HEAD_DIM
MIX_WIDTH = N_MIX_GROUPS * HEAD_DIM
KV_WIDTH = N_KV_HEADS * HEAD_DIM
N_GATES = 3
IN_COLS = ATTN_WIDTH + 6 * KV_WIDTH + N_GATES * N_ATTN_HEADS + 2 * MIX_WIDTH
CMP_BLOCK = 32
CMP_STRIDE = 16
CMP_HIDDEN = 256
SEL_BLOCK = 64
SEL_TOPN = 16
SEL_Q_BLOCK = 64
WINDOW = 512
Q_BLOCK = 128
CHUNK = 128
D_FF = 4 * D_MODEL
ROPE_THETA = 10000.0
EPS = 1e-6
FORCE_SCORE = 1e9

kernel_name = "hymba_nsa_gmlp_hybrid_block"


def rms_norm(x, g):
    xf = x.astype(jnp.float32)
    y = xf * lax.rsqrt(jnp.mean(xf * xf, axis=-1, keepdims=True) + EPS)
    return (y * g.astype(jnp.float32)).astype(x.dtype)


def rope(x, pos):
    half = x.shape[-1] // 2
    inv = ROPE_THETA ** (-jnp.arange(half, dtype=jnp.float32) / half)
    ang = pos.astype(jnp.float32)[:, None] * inv[None, :]
    cos = jnp.cos(ang)[:, None, :]
    sin = jnp.sin(ang)[:, None, :]
    xf = x.astype(jnp.float32)
    x1, x2 = xf[..., :half], xf[..., half:]
    return jnp.concatenate([x1 * cos - x2 * sin, x2 * cos + x1 * sin], axis=-1).astype(x.dtype)


def masked_softmax(s, mask):
    s = jnp.where(mask, s.astype(jnp.float32), -jnp.inf)
    m = jnp.max(s, axis=-1, keepdims=True)
    m = jnp.where(jnp.isfinite(m), m, 0.0)
    p = jnp.exp(s - m)
    return p / jnp.maximum(jnp.sum(p, axis=-1, keepdims=True), 1e-20)


def compress_blocks(t, pe, w1, w2):
    B, S, Hkv, D = t.shape
    n_cmp = (S - CMP_BLOCK) // CMP_STRIDE + 1
    idx = jnp.arange(n_cmp)[:, None] * CMP_STRIDE + jnp.arange(CMP_BLOCK)[None, :]
    blk = t[:, idx] + pe[None, None, :, None, :]
    blk = jnp.swapaxes(blk, 2, 3).reshape(B, n_cmp, Hkv, CMP_BLOCK * D)
    return jax.nn.gelu(blk @ w1) @ w2


def setup_inputs(seed: int = 0) -> dict:
    key = jax.random.key(seed)
    ks = jax.random.split(key, 16)
    L = DEPTH
    nrm = jax.random.normal
    x = nrm(ks[0], (BATCH, SEQ, D_MODEL), jnp.float32)
    g_mix_norm = 1.0 + 0.02 * nrm(ks[1], (L, D_MODEL), jnp.float32)
    w_in = nrm(ks[2], (L, D_MODEL, IN_COLS), jnp.float32) * D_MODEL ** -0.5
    g_q = 1.0 + 0.02 * nrm(ks[3], (L, HEAD_DIM), jnp.float32)
    g_k = 1.0 + 0.02 * nrm(ks[4], (L, 3, HEAD_DIM), jnp.float32)
    cmp_pe = 0.5 * nrm(ks[5], (L, 2, CMP_BLOCK, HEAD_DIM), jnp.float32)
    cmp_w1 = nrm(ks[6], (L, 2, CMP_BLOCK * HEAD_DIM, CMP_HIDDEN), jnp.float32) * (CMP_BLOCK * HEAD_DIM) ** -0.5
    cmp_w2 = nrm(ks[7], (L, 2, CMP_HIDDEN, HEAD_DIM), jnp.float32) * CMP_HIDDEN ** -0.5
    g_sgu = 1.0 + 0.02 * nrm(ks[8], (L, N_MIX_GROUPS, HEAD_DIM), jnp.float32)
    sp_w = nrm(ks[9], (L, N_MIX_GROUPS, CHUNK, CHUNK), jnp.float32) * CHUNK ** -0.5
    sp_b = 1.0 + 0.1 * nrm(ks[10], (L, N_MIX_GROUPS, CHUNK), jnp.float32)
    g_out = 1.0 + 0.02 * nrm(ks[11], (L, D_MODEL), jnp.float32)
    w_out = nrm(ks[12], (L, D_MODEL, D_MODEL), jnp.float32) * D_MODEL ** -0.5
    g_ffn_norm = 1.0 + 0.02 * nrm(ks[13], (L, D_MODEL), jnp.float32)
    w_ff1 = nrm(ks[14], (L, D_MODEL, D_FF), jnp.float32) * D_MODEL ** -0.5
    w_ff2 = nrm(ks[15], (L, D_FF, D_MODEL), jnp.float32) * D_FF ** -0.5
    return {"x": x, "g_mix_norm": g_mix_norm, "w_in": w_in, "g_q": g_q, "g_k": g_k,
            "cmp_pe": cmp_pe, "cmp_w1": cmp_w1, "cmp_w2": cmp_w2, "g_sgu": g_sgu,
            "sp_w": sp_w, "sp_b": sp_b, "g_out": g_out, "w_out": w_out,
            "g_ffn_norm": g_ffn_norm, "w_ff1": w_ff1, "w_ff2": w_ff2}


def reference(x, g_mix_norm, w_in, g_q, g_k, cmp_pe, cmp_w1, cmp_w2, g_sgu,
              sp_w, sp_b, g_out, w_out, g_ffn_norm, w_ff1, w_ff2):
    B, S, _ = x.shape
    Hkv, G, D = N_KV_HEADS, GQA_GROUP, HEAD_DIM
    f32 = jnp.float32
    pos = jnp.arange(S, dtype=jnp.int32)
    scale = HEAD_DIM ** -0.5
    sizes = [ATTN_WIDTH] + [KV_WIDTH] * 6 + [N_GATES * N_ATTN_HEADS, MIX_WIDTH]
    split_pts = [int(v) for v in np.cumsum(sizes)]

    for l in range(DEPTH):
        h = rms_norm(x, g_mix_norm[l])
        z = h @ w_in[l]
        q, kc, vc, ksl, vsl, kwn, vwn, gate_logits, zu, zv = jnp.split(z, split_pts, axis=-1)

        q = rms_norm(q.reshape(B, S, N_ATTN_HEADS, D), g_q[l])
        q_rot = rope(q, pos).reshape(B, S, Hkv, G, D)
        q = q.reshape(B, S, Hkv, G, D)
        kc = kc.reshape(B, S, Hkv, D)
        vc = vc.reshape(B, S, Hkv, D)
        ksl = rope(rms_norm(ksl.reshape(B, S, Hkv, D), g_k[l, 1]), pos)
        vsl = vsl.reshape(B, S, Hkv, D)
        kwn = rope(rms_norm(kwn.reshape(B, S, Hkv, D), g_k[l, 2]), pos)
        vwn = vwn.reshape(B, S, Hkv, D)

        kcmp = rms_norm(compress_blocks(kc, cmp_pe[l, 0], cmp_w1[l, 0], cmp_w2[l, 0]), g_k[l, 0])
        vcmp = compress_blocks(vc, cmp_pe[l, 1], cmp_w1[l, 1], cmp_w2[l, 1])
        n_cmp = kcmp.shape[1]
        cmp_start = jnp.arange(n_cmp, dtype=jnp.int32) * CMP_STRIDE
        cmp_end = cmp_start + CMP_BLOCK - 1
        s_c = jnp.einsum('bshgd,bnhd->bhgsn', q, kcmp).astype(f32) * scale
        p_c = masked_softmax(s_c, cmp_end[None, :] <= pos[:, None])
        o_cmp = jnp.einsum('bhgsn,bnhd->bshgd', p_c.astype(vcmp.dtype), vcmp)

        n_sel = S // SEL_BLOCK
        top_n = min(SEL_TOPN, n_sel)
        sel_start = jnp.arange(n_sel, dtype=jnp.int32) * SEL_BLOCK
        overlap = jnp.clip(jnp.minimum(cmp_start[:, None] + CMP_BLOCK, sel_start[None, :] + SEL_BLOCK)
                           - jnp.maximum(cmp_start[:, None], sel_start[None, :]), 0).astype(f32) / CMP_BLOCK
        imp = jnp.einsum('bhgsn,nj->bhsj', p_c, overlap)
        cur = pos // SEL_BLOCK
        jj = jnp.arange(n_sel, dtype=jnp.int32)
        forced = (jj[None, :] == 0) | (jj[None, :] == cur[:, None]) | (jj[None, :] == cur[:, None] - 1)
        eligible = sel_start[None, :] <= pos[:, None]
        imp = jnp.where(forced, FORCE_SCORE, jnp.where(eligible, imp, -jnp.inf))
        top_val, top_idx = lax.top_k(imp, top_n)
        top_valid = jnp.isfinite(top_val)

        kb = ksl.reshape(B, n_sel, SEL_BLOCK, Hkv, D).transpose(0, 3, 1, 2, 4)
        vb = vsl.reshape(B, n_sel, SEL_BLOCK, Hkv, D).transpose(0, 3, 1, 2, 4)
        nqs = S // SEL_Q_BLOCK
        q_ch = q_rot.reshape(B, nqs, SEL_Q_BLOCK, Hkv, G, D).transpose(1, 0, 3, 4, 2, 5)
        idx_ch = top_idx.reshape(B, Hkv, nqs, SEL_Q_BLOCK, top_n).transpose(2, 0, 1, 3, 4)
        val_ch = top_valid.reshape(B, Hkv, nqs, SEL_Q_BLOCK, top_n).transpose(2, 0, 1, 3, 4)
        pos_ch = pos.reshape(nqs, SEL_Q_BLOCK)
        gather = jax.vmap(jax.vmap(lambda blocks, ix: blocks[ix]))

        def sel_block(args):
            qc, ic, vmask, pc = args
            kg = gather(kb, ic)
            vg = gather(vb, ic)
            s = jnp.einsum('bhgqd,bhqnkd->bhgqnk', qc, kg).astype(f32) * scale
            kpos = ic[..., None] * SEL_BLOCK + jnp.arange(SEL_BLOCK, dtype=jnp.int32)
            m = vmask[..., None] & (kpos <= pc[:, None, None])
            p = masked_softmax(s.reshape(B, Hkv, G, SEL_Q_BLOCK, top_n * SEL_BLOCK),
                               m.reshape(B, Hkv, 1, SEL_Q_BLOCK, top_n * SEL_BLOCK))
            return jnp.einsum('bhgqm,bhqmd->bhgqd', p.astype(vg.dtype),
                              vg.reshape(B, Hkv, SEL_Q_BLOCK, top_n * SEL_BLOCK, D))

        o_sel = lax.map(sel_block, (q_ch, idx_ch, val_ch, pos_ch))
        o_sel = o_sel.transpose(1, 0, 4, 2, 3, 5).reshape(B, S, Hkv, G, D)

        nqw = S // Q_BLOCK
        span = WINDOW + Q_BLOCK
        kpad = jnp.pad(kwn, ((0, 0), (WINDOW, 0), (0, 0), (0, 0)))
        vpad = jnp.pad(vwn, ((0, 0), (WINDOW, 0), (0, 0), (0, 0)))
        widx = jnp.arange(nqw, dtype=jnp.int32)[:, None] * Q_BLOCK + jnp.arange(span, dtype=jnp.int32)[None, :]
        kwin = kpad[:, widx]
        vwin = vpad[:, widx]
        kpos_w = (widx - WINDOW)[:, None, :]
        qpos_w = pos.reshape(nqw, Q_BLOCK)[:, :, None]
        mw = (kpos_w >= 0) & (kpos_w <= qpos_w) & (qpos_w - kpos_w < WINDOW)
        qw = q_rot.reshape(B, nqw, Q_BLOCK, Hkv, G, D)
        s_w = jnp.einsum('bcqhgd,bckhd->bhgcqk', qw, kwin).astype(f32) * scale
        p_w = masked_softmax(s_w, mw)
        o_win = jnp.einsum('bhgcqk,bckhd->bcqhgd', p_w.astype(vwin.dtype), vwin).reshape(B, S, Hkv, G, D)

        gates = jax.nn.sigmoid(gate_logits.astype(f32)).reshape(B, S, Hkv, G, N_GATES).astype(x.dtype)
        o_attn = (gates[..., 0:1] * o_cmp + gates[..., 1:2] * o_sel
                  + gates[..., 2:3] * o_win).reshape(B, S, ATTN_WIDTH)

        zu = jax.nn.gelu(zu)
        zv = rms_norm(jax.nn.gelu(zv).reshape(B, S, N_MIX_GROUPS, HEAD_DIM), g_sgu[l])
        zv = zv.reshape(B, S // CHUNK, CHUNK, N_MIX_GROUPS, HEAD_DIM)
        w_s = sp_w[l] * jnp.tril(jnp.ones((CHUNK, CHUNK), sp_w.dtype))
        sv = jnp.einsum('gts,bcsgd->bctgd', w_s, zv) + sp_b[l].T[:, :, None]
        o_mix = zu * sv.reshape(B, S, MIX_WIDTH)

        o = jnp.concatenate([rms_norm(o_attn, g_out[l, :ATTN_WIDTH]),
                             rms_norm(o_mix, g_out[l, ATTN_WIDTH:])], axis=-1)
        x = x + o @ w_out[l]

        h = rms_norm(x, g_ffn_norm[l])
        x = x + jnp.square(jax.nn.relu(h @ w_ff1[l])) @ w_ff2[l]
    return x
```

```python
import functools

import numpy as np
import jax
import jax.numpy as jnp
from jax import lax
from jax.experimental import pallas as pl
from jax.experimental.pallas import tpu as pltpu

D_MODEL = 1024
HEAD_DIM = 64
N_ATTN_HEADS = 8
N_MIX_GROUPS = 8
GQA_GROUP = 4
N_KV_HEADS = 2
ATTN_WIDTH = 512
MIX_WIDTH = 512
KV_WIDTH = 128
N_GATES = 3
CMP_BLOCK = 32
CMP_STRIDE = 16
CMP_HIDDEN = 256
SEL_BLOCK = 64
SEL_TOPN = 16
WINDOW = 512
CHUNK = 128
D_FF = 4 * D_MODEL
ROPE_THETA = 10000.0
EPS = 1e-6
FORCE_SCORE = 1e9

LANES = 128
NEG_BIG = -1e30
PROJ_TILE = 256
ATTN_Q_TILE = 128
SEL_K_TILE = 256
FFN_TILE = 256
FF_CHUNK = 1024
VMEM_LIMIT = 56 * 1024 * 1024

_C_Q = 0
_C_KC = 512
_C_VC = 640
_C_KSL = 768
_C_VSL = 896
_C_KWN = 1024
_C_VWN = 1152
_C_ZU = 1280
_C_ZV = 1792
_C_GATE = 2304
_IN_COLS_PACKED = 2432

_bf16 = jnp.bfloat16
_f32 = jnp.float32


def _dot(a, b):
    return jnp.dot(a, b, preferred_element_type=_f32)


def _dot_nt(a, b):
    return lax.dot_general(a, b, (((1,), (1,)), ((), ())), preferred_element_type=_f32)


def _split_dot(v, m):
    hi = v.astype(_bf16)
    lo = (v - hi.astype(_f32)).astype(_bf16)
    return _dot(hi, m) + _dot(lo, m)


def _group_mean(sq, bd):
    parts = [_split_dot(sq[:, c * LANES:(c + 1) * LANES], bd) for c in range(sq.shape[1] // LANES)]
    return parts[0] if len(parts) == 1 else jnp.concatenate(parts, axis=1)


def _swap_halves(v):
    n = v.shape[1]
    lane = lax.broadcasted_iota(jnp.int32, v.shape, 1)
    return jnp.where((lane % HEAD_DIM) < HEAD_DIM // 2, pltpu.roll(v, n - HEAD_DIM // 2, 1),
                     pltpu.roll(v, HEAD_DIM // 2, 1))


def _proj_kernel(x_ref, gmix_ref, w_ref, gq_ref, gksl_ref, gkwn_ref, cos_ref, sin_ref, bd_ref,
                 gsgu_ref, spw_ref, spb_ref, gomix_ref,
                 qn_ref, qr_ref, kc_ref, vc_ref, ksl_ref, vsl_ref, kwn_ref, vwn_ref, gate_ref, omix_ref):
    tm = x_ref.shape[1]
    i = pl.program_id(1)
    x = x_ref[0]
    ms = jnp.mean(x * x, axis=-1, keepdims=True)
    h = (x * lax.rsqrt(ms + EPS) * gmix_ref[...]).astype(_bf16)
    bd = bd_ref[...]
    cos = cos_ref[...]
    sin = sin_ref[...]

    zq = _dot(h, w_ref[:, _C_Q:_C_Q + ATTN_WIDTH])
    qn = zq * lax.rsqrt(_group_mean(zq * zq, bd) + EPS) * gq_ref[...]
    cos4 = jnp.concatenate([cos] * 4, axis=1)
    sin4 = jnp.concatenate([sin] * 4, axis=1)
    qr = qn * cos4 + _swap_halves(qn) * sin4
    zero_half = jnp.zeros((tm, HEAD_DIM), _bf16)
    for hq in range(N_ATTN_HEADS):
        sl = slice(hq * HEAD_DIM, (hq + 1) * HEAD_DIM)
        qn_ref[0, hq] = jnp.concatenate([qn[:, sl].astype(_bf16), zero_half], axis=1)
        qr_ref[0, hq] = jnp.concatenate([qr[:, sl].astype(_bf16), zero_half], axis=1)

    kc_ref[0] = _dot(h, w_ref[:, _C_KC:_C_KC + KV_WIDTH]).astype(_bf16)
    vc_ref[0] = _dot(h, w_ref[:, _C_VC:_C_VC + KV_WIDTH]).astype(_bf16)
    vsl_ref[0] = _dot(h, w_ref[:, _C_VSL:_C_VSL + KV_WIDTH]).astype(_bf16)
    vwn_ref[0] = _dot(h, w_ref[:, _C_VWN:_C_VWN + KV_WIDTH]).astype(_bf16)

    lane = lax.broadcasted_iota(jnp.int32, (tm, LANES), 1)
    pos = i * tm + lax.broadcasted_iota(jnp.int32, (tm, LANES), 0)
    blk_onehot = jnp.where(lane - HEAD_DIM == pos // SEL_BLOCK, 1.0, 0.0)
    for col, g_ref, out_ref, extra in ((_C_KSL, gksl_ref, ksl_ref, blk_onehot),
                                       (_C_KWN, gkwn_ref, kwn_ref, jnp.zeros((tm, LANES), _f32))):
        zk = _dot(h, w_ref[:, col:col + KV_WIDTH])
        kn = zk * lax.rsqrt(_group_mean(zk * zk, bd) + EPS) * g_ref[...]
        kr = kn * cos + _swap_halves(kn) * sin
        out_ref[0, 0] = jnp.where(lane < HEAD_DIM, kr, extra).astype(_bf16)
        out_ref[0, 1] = jnp.where(lane < HEAD_DIM, pltpu.roll(kr, HEAD_DIM, 1), extra).astype(_bf16)

    gate_ref[0] = jax.nn.sigmoid(_dot(h, w_ref[:, _C_GATE:_C_GATE + LANES]))

    row = lax.broadcasted_iota(jnp.int32, (CHUNK, 2 * CHUNK), 0)
    colw = lax.broadcasted_iota(jnp.int32, (CHUNK, 2 * CHUNK), 1) % CHUNK
    causal_w = colw <= row
    lane_c = lax.broadcasted_iota(jnp.int32, (CHUNK, LANES), 1)
    for c in range(tm // CHUNK):
        rows = slice(c * CHUNK, (c + 1) * CHUNK)
        hc = h[rows]
        zu = jax.nn.gelu(_dot(hc, w_ref[:, _C_ZU:_C_ZU + MIX_WIDTH]))
        zv = jax.nn.gelu(_dot(hc, w_ref[:, _C_ZV:_C_ZV + MIX_WIDTH]))
        vn = zv * lax.rsqrt(_group_mean(zv * zv, bd) + EPS) * gsgu_ref[...]
        sv_parts = []
        for p in range(N_MIX_GROUPS // 2):
            vp = vn[:, p * LANES:(p + 1) * LANES]
            rhs = jnp.concatenate([jnp.where(lane_c < HEAD_DIM, vp, 0.0),
                                   jnp.where(lane_c < HEAD_DIM, 0.0, vp)], axis=0).astype(_bf16)
            w_pair = jnp.where(causal_w, spw_ref[p], jnp.zeros((), _bf16))
            sv_parts.append(_dot(w_pair, rhs))
        sv = jnp.concatenate(sv_parts, axis=1) + spb_ref[...]
        om = zu * sv
        oms = jnp.mean(om * om, axis=-1, keepdims=True)
        omix_ref[0, rows] = (om * lax.rsqrt(oms + EPS) * gomix_ref[...]).astype(_bf16)


def _proj_call(x, gmix, w_in_p, gq, gksl, gkwn, cos, sin, bd, gsgu, spw, spb, gomix):
    B, S, _ = x.shape
    tm = PROJ_TILE
    nt = S // tm
    const2 = lambda b, i: (0, 0)
    const3 = lambda b, i: (0, 0, 0)
    tok3 = lambda b, i: (b, i, 0)
    head4 = lambda b, i: (b, 0, i, 0)
    out_shape = (
        jax.ShapeDtypeStruct((B, N_ATTN_HEADS, S, LANES), _bf16),
        jax.ShapeDtypeStruct((B, N_ATTN_HEADS, S, LANES), _bf16),
        jax.ShapeDtypeStruct((B, S, KV_WIDTH), _bf16),
        jax.ShapeDtypeStruct((B, S, KV_WIDTH), _bf16),
        jax.ShapeDtypeStruct((B, N_KV_HEADS, S, LANES), _bf16),
        jax.ShapeDtypeStruct((B, S, KV_WIDTH), _bf16),
        jax.ShapeDtypeStruct((B, N_KV_HEADS, S, LANES), _bf16),
        jax.ShapeDtypeStruct((B, S, KV_WIDTH), _bf16),
        jax.ShapeDtypeStruct((B, S, LANES), _f32),
        jax.ShapeDtypeStruct((B, S, MIX_WIDTH), _bf16),
    )
    q_spec = pl.BlockSpec((1, N_ATTN_HEADS, tm, LANES), head4)
    k_spec = pl.BlockSpec((1, N_KV_HEADS, tm, LANES), head4)
    t_spec = pl.BlockSpec((1, tm, LANES), tok3)
    return pl.pallas_call(
        _proj_kernel,
        grid=(B, nt),
        in_specs=[
            pl.BlockSpec((1, tm, D_MODEL), tok3),
            pl.BlockSpec((1, D_MODEL), const2),
            pl.BlockSpec((D_MODEL, _IN_COLS_PACKED), const2),
            pl.BlockSpec((1, ATTN_WIDTH), const2),
            pl.BlockSpec((1, KV_WIDTH), const2),
            pl.BlockSpec((1, KV_WIDTH), const2),
            pl.BlockSpec((tm, LANES), lambda b, i: (i, 0)),
            pl.BlockSpec((tm, LANES), lambda b, i: (i, 0)),
            pl.BlockSpec((LANES, LANES), const2),
            pl.BlockSpec((1, MIX_WIDTH), const2),
            pl.BlockSpec((N_MIX_GROUPS // 2, CHUNK, 2 * CHUNK), const3),
            pl.BlockSpec((CHUNK, MIX_WIDTH), const2),
            pl.BlockSpec((1, MIX_WIDTH), const2),
        ],
        out_specs=(q_spec, q_spec, t_spec, t_spec, k_spec, t_spec, k_spec, t_spec, t_spec,
                   pl.BlockSpec((1, tm, MIX_WIDTH), tok3)),
        out_shape=out_shape,
        compiler_params=pltpu.CompilerParams(
            dimension_semantics=("parallel", "parallel"), vmem_limit_bytes=VMEM_LIMIT),
        name="proj",
    )(x, gmix, w_in_p, gq, gksl, gkwn, cos, sin, bd, gsgu, spw, spb, gomix)


def _compress_kernel(kc_ref, vc_ref, pe_ref, w1_ref, w2_ref, gk_ref, kcmp_ref, vcmp_ref):
    n_rows = kc_ref.shape[1]
    lane = lax.broadcasted_iota(jnp.int32, (n_rows, LANES), 1)
    for t, (src_ref, out_ref) in enumerate(((kc_ref, kcmp_ref), (vc_ref, vcmp_ref))):
        g = src_ref[0].astype(_f32)
        first = _dot((g + pe_ref[t, 0:1]).astype(_bf16), w1_ref[t, 0])
        second = _dot((g + pe_ref[t, 1:2]).astype(_bf16), w1_ref[t, 1])
        hid = jax.nn.gelu(first + pltpu.roll(second, n_rows - 1, 0)).astype(_bf16)
        heads = [_dot(hid[:, hh * CMP_HIDDEN:(hh + 1) * CMP_HIDDEN], w2_ref[t]) for hh in range(N_KV_HEADS)]
        if t == 0:
            heads = [c * lax.rsqrt(jnp.mean(c * c, axis=-1, keepdims=True) + EPS) * gk_ref[...] for c in heads]
            zero_half = jnp.zeros((n_rows, HEAD_DIM), _f32)
            for hh in range(N_KV_HEADS):
                out_ref[0, hh] = jnp.concatenate([heads[hh], zero_half], axis=1).astype(_bf16)
        else:
            out_ref[0] = jnp.concatenate(heads, axis=1).astype(_bf16)


def _compress_call(kc_g, vc_g, pe_rows, w1x, w2, gk0):
    B, n_rows, width = kc_g.shape
    return pl.pallas_call(
        _compress_kernel,
        grid=(B,),
        in_specs=[
            pl.BlockSpec((1, n_rows, width), lambda b: (b, 0, 0)),
            pl.BlockSpec((1, n_rows, width), lambda b: (b, 0, 0)),
            pl.BlockSpec((2, 2, width), lambda b: (0, 0, 0)),
            pl.BlockSpec((2, 2, width, N_KV_HEADS * CMP_HIDDEN), lambda b: (0, 0, 0, 0)),
            pl.BlockSpec((2, CMP_HIDDEN, HEAD_DIM), lambda b: (0, 0, 0)),
            pl.BlockSpec((1, HEAD_DIM), lambda b: (0, 0)),
        ],
        out_specs=(pl.BlockSpec((1, N_KV_HEADS, n_rows, LANES), lambda b: (b, 0, 0, 0)),
                   pl.BlockSpec((1, n_rows, LANES), lambda b: (b, 0, 0))),
        out_shape=(jax.ShapeDtypeStruct((B, N_KV_HEADS, n_rows, LANES), _bf16),
                   jax.ShapeDtypeStruct((B, n_rows, LANES), _bf16)),
        compiler_params=pltpu.CompilerParams(
            dimension_semantics=("parallel",), vmem_limit_bytes=VMEM_LIMIT),
        name="compress",
    )(kc_g, vc_g, pe_rows, w1x, w2, gk0)


def _attn_kernel(qn_ref, qr_ref, kcmp_ref, vcmp_ref, ksl_ref, vsl_ref, kwn_ref, vwn_ref, gate_ref,
                 ov_ref, place_ref, gout_ref, o_ref, m_scr, l_scr, acc_scr, o_scr):
    tq = o_ref.shape[1]
    rows = GQA_GROUP * tq
    S = vsl_ref.shape[1]
    n_cmp_pad = kcmp_ref.shape[2]
    n_sel = S // SEL_BLOCK
    i = pl.program_id(1)
    q0 = i * tq

    qpos = q0 + lax.broadcasted_iota(jnp.int32, (rows, 1), 0) % tq
    tpos = q0 + lax.broadcasted_iota(jnp.int32, (tq, 1), 0)
    lane_t = lax.broadcasted_iota(jnp.int32, (tq, LANES), 1)
    gates = gate_ref[0]

    for h in range(N_KV_HEADS):
        qn = qn_ref[0, GQA_GROUP * h:GQA_GROUP * (h + 1)].reshape(rows, LANES)
        qr = qr_ref[0, GQA_GROUP * h:GQA_GROUP * (h + 1)].reshape(rows, LANES)

        s = _dot_nt(qn, kcmp_ref[0, h])
        n_idx = lax.broadcasted_iota(jnp.int32, (1, n_cmp_pad), 1)
        valid_c = n_idx * CMP_STRIDE + (CMP_BLOCK - 1) <= qpos
        s = jnp.where(valid_c, s, NEG_BIG)
        m = jnp.max(s, axis=-1, keepdims=True)
        e = jnp.where(valid_c, jnp.exp(s - m), 0.0)
        p_c = e / jnp.maximum(jnp.sum(e, axis=-1, keepdims=True), 1e-20)
        o_cmp = _dot(p_c.astype(_bf16), vcmp_ref[0])

        p_sum = p_c[0:tq] + p_c[tq:2 * tq] + p_c[2 * tq:3 * tq] + p_c[3 * tq:4 * tq]
        imp = _split_dot(p_sum, ov_ref[...])
        cur = tpos // SEL_BLOCK
        forced = (lane_t == 0) | (lane_t == cur) | (lane_t == cur - 1)
        eligible = (lane_t * SEL_BLOCK <= tpos) & (lane_t < n_sel)
        score = jnp.where(forced, FORCE_SCORE, jnp.where(eligible, imp, -jnp.inf))
        rank = jnp.zeros((tq, LANES), _f32)
        for jp in range(n_sel):
            col = score[:, jp:jp + 1]
            ahead = (col > score) | ((col == score) & (lane_t > jp))
            rank = rank + jnp.where(ahead, 1.0, 0.0)
        chosen = (rank < float(SEL_TOPN)) & eligible
        sel_bias = jnp.where(chosen, 0.0, NEG_BIG).astype(_bf16)
        bias_row = _dot(sel_bias, place_ref[...]).astype(_bf16)
        q_sel = qr + jnp.concatenate([bias_row] * GQA_GROUP, axis=0)

        m_scr[...] = jnp.full(m_scr.shape, NEG_BIG, _f32)
        l_scr[...] = jnp.zeros(l_scr.shape, _f32)
        acc_scr[...] = jnp.zeros(acc_scr.shape, _f32)

        def sel_step(kt, carry):
            k0 = pl.multiple_of(kt * SEL_K_TILE, SEL_K_TILE)
            kk = ksl_ref[0, h, pl.ds(k0, SEL_K_TILE), :]
            vv = vsl_ref[0, pl.ds(k0, SEL_K_TILE), :]
            sc = _dot_nt(q_sel, kk)
            kpos = k0 + lax.broadcasted_iota(jnp.int32, (1, SEL_K_TILE), 1)
            sc = jnp.where(kpos <= qpos, sc, NEG_BIG)
            m_old = m_scr[...]
            m_new = jnp.maximum(m_old, jnp.max(sc, axis=-1, keepdims=True))
            alpha = jnp.exp(m_old - m_new)
            pp = jnp.exp(sc - m_new)
            l_scr[...] = alpha * l_scr[...] + jnp.sum(pp, axis=-1, keepdims=True)
            acc_scr[...] = alpha * acc_scr[...] + _dot(pp.astype(_bf16), vv)
            m_scr[...] = m_new
            return carry

        n_tiles = (q0 + tq + SEL_K_TILE - 1) // SEL_K_TILE
        lax.fori_loop(0, n_tiles, sel_step, 0)
        o_sel = acc_scr[...] / l_scr[...]

        span = WINDOW + tq
        w0 = pl.multiple_of(jnp.maximum(q0 - WINDOW, 0), tq)
        kw = kwn_ref[0, h, pl.ds(w0, span), :]
        vw = vwn_ref[0, pl.ds(w0, span), :]
        sw = _dot_nt(qr, kw)
        kpos_w = w0 + lax.broadcasted_iota(jnp.int32, (1, span), 1)
        valid_w = (kpos_w <= qpos) & (qpos - kpos_w < WINDOW)
        sw = jnp.where(valid_w, sw, NEG_BIG)
        mw = jnp.max(sw, axis=-1, keepdims=True)
        pw = jnp.where(valid_w, jnp.exp(sw - mw), 0.0)
        o_win = _dot(pw.astype(_bf16), vw) / jnp.sum(pw, axis=-1, keepdims=True)

        heads = []
        for g in range(GQA_GROUP):
            hq = GQA_GROUP * h + g
            r = slice(g * tq, (g + 1) * tq)
            heads.append(gates[:, hq:hq + 1] * o_cmp[r]
                         + gates[:, N_ATTN_HEADS + hq:N_ATTN_HEADS + hq + 1] * o_sel[r]
                         + gates[:, 2 * N_ATTN_HEADS + hq:2 * N_ATTN_HEADS + hq + 1] * o_win[r])
        for gp in range(GQA_GROUP // 2):
            a, b = heads[2 * gp], heads[2 * gp + 1]
            if h == 0:
                pair = jnp.where(lane_t < HEAD_DIM, a, pltpu.roll(b, HEAD_DIM, 1))
            else:
                pair = jnp.where(lane_t < HEAD_DIM, pltpu.roll(a, HEAD_DIM, 1), b)
            c = 2 * h + gp
            o_scr[:, c * LANES:(c + 1) * LANES] = pair

    o = o_scr[...]
    ms = jnp.mean(o * o, axis=-1, keepdims=True)
    o_ref[0] = (o * lax.rsqrt(ms + EPS) * gout_ref[...]).astype(_bf16)


def _attn_call(qn, qr, kcmp, vcmp, ksl, vsl, kwn, vwn, gates, ov, place, gout_attn):
    B, _, S, _ = qn.shape
    tq = ATTN_Q_TILE
    n_cmp_pad = kcmp.shape[2]
    rows = GQA_GROUP * tq
    per_b4 = lambda b, i: (b, 0, 0, 0)
    per_b3 = lambda b, i: (b, 0, 0)
    const2 = lambda b, i: (0, 0)
    return pl.pallas_call(
        _attn_kernel,
        grid=(B, S // tq),
        in_specs=[
            pl.BlockSpec((1, N_ATTN_HEADS, tq, LANES), lambda b, i: (b, 0, i, 0)),
            pl.BlockSpec((1, N_ATTN_HEADS, tq, LANES), lambda b, i: (b, 0, i, 0)),
            pl.BlockSpec((1, N_KV_HEADS, n_cmp_pad, LANES), per_b4),
            pl.BlockSpec((1, n_cmp_pad, LANES), per_b3),
            pl.BlockSpec((1, N_KV_HEADS, S, LANES), per_b4),
            pl.BlockSpec((1, S, LANES), per_b3),
            pl.BlockSpec((1, N_KV_HEADS, S, LANES), per_b4),
            pl.BlockSpec((1, S, LANES), per_b3),
            pl.BlockSpec((1, tq, LANES), lambda b, i: (b, i, 0)),
            pl.BlockSpec((n_cmp_pad, LANES), const2),
            pl.BlockSpec((LANES, LANES), const2),
            pl.BlockSpec((1, ATTN_WIDTH), const2),
        ],
        out_specs=pl.BlockSpec((1, tq, ATTN_WIDTH), lambda b, i: (b, i, 0)),
        out_shape=jax.ShapeDtypeStruct((B, S, ATTN_WIDTH), _bf16),
        scratch_shapes=[
            pltpu.VMEM((rows, 1), _f32),
            pltpu.VMEM((rows, 1), _f32),
            pltpu.VMEM((rows, LANES), _f32),
            pltpu.VMEM((tq, ATTN_WIDTH), _f32),
        ],
        compiler_params=pltpu.CompilerParams(
            dimension_semantics=("parallel", "arbitrary"), vmem_limit_bytes=VMEM_LIMIT),
        name="attn",
    )(qn, qr, kcmp, vcmp, ksl, vsl, kwn, vwn, gates, ov, place, gout_attn)


def _ffn_kernel(x_ref, oa_ref, om_ref, wo_ref, gffn_ref, w1_ref, w2_ref, out_ref, act_scr):
    x2 = (x_ref[...] + _dot(oa_ref[...], wo_ref[0:ATTN_WIDTH, :])
          + _dot(om_ref[...], wo_ref[ATTN_WIDTH:ATTN_WIDTH + MIX_WIDTH, :]))
    ms = jnp.mean(x2 * x2, axis=-1, keepdims=True)
    h = (x2 * lax.rsqrt(ms + EPS) * gffn_ref[...]).astype(_bf16)
    for c in range(D_FF // FF_CHUNK):
        cols = slice(c * FF_CHUNK, (c + 1) * FF_CHUNK)
        a = jnp.maximum(_dot(h, w1_ref[:, cols]), 0.0)
        act_scr[:, cols] = (a * a).astype(_bf16)
    out_ref[...] = x2 + _dot(act_scr[...], w2_ref[...])


def _ffn_call(x2d, oa, om, wo, gffn, w1, w2):
    T = x2d.shape[0]
    tm = FFN_TILE
    tok = lambda i: (i, 0)
    const = lambda i: (0, 0)
    return pl.pallas_call(
        _ffn_kernel,
        grid=(T // tm,),
        in_specs=[
            pl.BlockSpec((tm, D_MODEL), tok),
            pl.BlockSpec((tm, ATTN_WIDTH), tok),
            pl.BlockSpec((tm, MIX_WIDTH), tok),
            pl.BlockSpec((D_MODEL, D_MODEL), const),
            pl.BlockSpec((1, D_MODEL), const),
            pl.BlockSpec((D_MODEL, D_FF), const),
            pl.BlockSpec((D_FF, D_MODEL), const),
        ],
        out_specs=pl.BlockSpec((tm, D_MODEL), tok),
        out_shape=jax.ShapeDtypeStruct((T, D_MODEL), _f32),
        scratch_shapes=[pltpu.VMEM((tm, D_FF), _bf16)],
        compiler_params=pltpu.CompilerParams(
            dimension_semantics=("parallel",), vmem_limit_bytes=VMEM_LIMIT),
        name="ffn",
    )(x2d, oa, om, wo, gffn, w1, w2)


def _constants(S):
    half = HEAD_DIM // 2
    inv = ROPE_THETA ** (-jnp.arange(half, dtype=_f32) / half)
    ang = jnp.arange(S, dtype=_f32)[:, None] * inv[None, :]
    cos = jnp.cos(ang)
    sin = jnp.sin(ang)
    cos2 = jnp.concatenate([cos, cos, cos, cos], axis=1)
    sin2 = jnp.concatenate([-sin, sin, -sin, sin], axis=1)
    lane = np.arange(LANES)
    bd = (lane[:, None] // HEAD_DIM == lane[None, :] // HEAD_DIM).astype(np.float32) / HEAD_DIM
    n_cmp = (S - CMP_BLOCK) // CMP_STRIDE + 1
    n_sel = S // SEL_BLOCK
    n_cmp_pad = S // CMP_STRIDE
    cmp_start = np.arange(n_cmp)[:, None] * CMP_STRIDE
    sel_start = np.arange(n_sel)[None, :] * SEL_BLOCK
    overlap = np.clip(np.minimum(cmp_start + CMP_BLOCK, sel_start + SEL_BLOCK)
                      - np.maximum(cmp_start, sel_start), 0, None).astype(np.float32) / CMP_BLOCK
    ov = np.zeros((n_cmp_pad, LANES), np.float32)
    ov[:n_cmp, :n_sel] = overlap
    place = np.zeros((LANES, LANES), np.float32)
    place[np.arange(n_sel), HEAD_DIM + np.arange(n_sel)] = 1.0
    return (cos2, sin2, jnp.asarray(bd, _bf16), jnp.asarray(ov, _bf16), jnp.asarray(place, _bf16))


def _pack_w_in(w):
    gate0 = ATTN_WIDTH + 6 * KV_WIDTH
    n_g = N_GATES * N_ATTN_HEADS
    gates = w[:, gate0:gate0 + n_g].reshape(D_MODEL, N_ATTN_HEADS, N_GATES)
    gates = jnp.transpose(gates, (0, 2, 1)).reshape(D_MODEL, n_g)
    pad = jnp.zeros((D_MODEL, _IN_COLS_PACKED - _C_GATE - n_g), w.dtype)
    return jnp.concatenate([w[:, :gate0], w[:, gate0 + n_g:], gates, pad], axis=1).astype(_bf16)


def _expand_cmp_w1(w1):
    w = w1.reshape(2, CMP_STRIDE, HEAD_DIM, CMP_HIDDEN)
    z = jnp.zeros_like(w)
    h0 = jnp.concatenate([w, z], axis=-1)
    h1 = jnp.concatenate([z, w], axis=-1)
    both = jnp.stack([h0, h1], axis=2)
    return both.reshape(2, CMP_STRIDE * KV_WIDTH, N_KV_HEADS * CMP_HIDDEN).astype(_bf16)


def _expand_pe(pe):
    p = pe.reshape(2, CMP_STRIDE, 1, HEAD_DIM)
    return jnp.broadcast_to(p, (2, CMP_STRIDE, N_KV_HEADS, HEAD_DIM)).reshape(2, CMP_STRIDE * KV_WIDTH)


def _forward(x, g_mix_norm, w_in, g_q, g_k, cmp_pe, cmp_w1, cmp_w2, g_sgu, sp_w, sp_b, g_out, w_out,
             g_ffn_norm, w_ff1, w_ff2):
    B, S, _ = x.shape
    l = 0
    scale = HEAD_DIM ** -0.5
    cos2, sin2, bd, ov, place = _constants(S)

    w_in_p = _pack_w_in(w_in[l])
    gq = (jnp.tile(g_q[l], N_ATTN_HEADS) * scale)[None, :]
    gksl = jnp.tile(g_k[l, 1], N_KV_HEADS)[None, :]
    gkwn = jnp.tile(g_k[l, 2], N_KV_HEADS)[None, :]
    gsgu = g_sgu[l].reshape(1, MIX_WIDTH)
    spw = sp_w[l].reshape(N_MIX_GROUPS // 2, 2, CHUNK, CHUNK)
    spw = jnp.transpose(spw, (0, 2, 1, 3)).reshape(N_MIX_GROUPS // 2, CHUNK, 2 * CHUNK).astype(_bf16)
    spb = jnp.repeat(sp_b[l].T, HEAD_DIM, axis=1)
    gout = g_out[l]

    qn, qr, kc, vc, ksl, vsl, kwn, vwn, gates, omix = _proj_call(
        x, g_mix_norm[l][None, :], w_in_p, gq, gksl, gkwn, cos2, sin2, bd, gsgu, spw, spb,
        gout[None, ATTN_WIDTH:])

    n_rows = S // CMP_STRIDE
    pe_rows = jnp.stack([_expand_pe(cmp_pe[l, 0]), _expand_pe(cmp_pe[l, 1])])
    w1x = jnp.stack([_expand_cmp_w1(cmp_w1[l, 0]), _expand_cmp_w1(cmp_w1[l, 1])])
    kcmp, vcmp = _compress_call(
        kc.reshape(B, n_rows, CMP_STRIDE * KV_WIDTH), vc.reshape(B, n_rows, CMP_STRIDE * KV_WIDTH),
        pe_rows, w1x, cmp_w2[l].astype(_bf16), g_k[l, 0][None, :])

    oattn = _attn_call(qn, qr, kcmp, vcmp, ksl, vsl, kwn, vwn, gates, ov, place, gout[None, :ATTN_WIDTH])

    out = _ffn_call(x.reshape(B * S, D_MODEL), oattn.reshape(B * S, ATTN_WIDTH),
                    omix.reshape(B * S, MIX_WIDTH), w_out[l].astype(_bf16), g_ffn_norm[l][None, :],
                    w_ff1[l].astype(_bf16), w_ff2[l].astype(_bf16))
    return out.reshape(B, S, D_MODEL), (qn, qr, kc, vc, ksl, vsl, kwn, vwn, gates, omix, kcmp, vcmp, oattn)


def kernel(x, g_mix_norm, w_in, g_q, g_k, cmp_pe, cmp_w1, cmp_w2, g_sgu, sp_w, sp_b, g_out, w_out,
           g_ffn_norm, w_ff1, w_ff2):
    return _forward(x, g_mix_norm, w_in, g_q, g_k, cmp_pe, cmp_w1, cmp_w2, g_sgu, sp_w, sp_b, g_out, w_out,
                    g_ffn_norm, w_ff1, w_ff2)[0]
```

```python
import functools

import numpy as np
import jax
import jax.numpy as jnp
from jax import lax
from jax.experimental import pallas as pl
from jax.experimental.pallas import tpu as pltpu

D_MODEL = 1024
HEAD_DIM = 64
N_ATTN_HEADS = 8
N_MIX_GROUPS = 8
GQA_GROUP = 4
N_KV_HEADS = 2
ATTN_WIDTH = 512
MIX_WIDTH = 512
KV_WIDTH = 128
N_GATES = 3
CMP_BLOCK = 32
CMP_STRIDE = 16
CMP_HIDDEN = 256
SEL_BLOCK = 64
SEL_TOPN = 16
WINDOW = 512
CHUNK = 128
D_FF = 4 * D_MODEL
ROPE_THETA = 10000.0
EPS = 1e-6
FORCE_SCORE = 1e9

LANES = 128
NEG_BIG = -1e30
PROJ_TILE = 256
ATTN_Q_TILE = 128
SEL_CHUNK = 512
MASK_FULL, MASK_DIAG, MASK_LOW, MASK_NONE = 0, 1, 2, 3
FFN_TILE = 256
FF_CHUNK = 1024
VMEM_LIMIT = 56 * 1024 * 1024

_C_Q = 0
_C_KC = 512
_C_VC = 640
_C_KSL = 768
_C_VSL = 896
_C_KWN = 1024
_C_VWN = 1152
_C_ZU = 1280
_C_ZV = 1792
_C_GATE = 2304
_IN_COLS_PACKED = 2432

_bf16 = jnp.bfloat16
_f32 = jnp.float32


def _dot(a, b):
    return jnp.dot(a, b, preferred_element_type=_f32)


def _dot_nt(a, b):
    return lax.dot_general(a, b, (((1,), (1,)), ((), ())), preferred_element_type=_f32)


def _split_dot(v, m):
    hi = v.astype(_bf16)
    lo = (v - hi.astype(_f32)).astype(_bf16)
    return _dot(hi, m) + _dot(lo, m)


def _group_mean(sq, bd):
    parts = [_split_dot(sq[:, c * LANES:(c + 1) * LANES], bd) for c in range(sq.shape[1] // LANES)]
    return parts[0] if len(parts) == 1 else jnp.concatenate(parts, axis=1)


def _swap_halves(v):
    n = v.shape[1]
    lane = lax.broadcasted_iota(jnp.int32, v.shape, 1)
    return jnp.where((lane % HEAD_DIM) < HEAD_DIM // 2, pltpu.roll(v, n - HEAD_DIM // 2, 1),
                     pltpu.roll(v, HEAD_DIM // 2, 1))


def _proj_kernel(x_ref, gmix_ref, w_ref, gq_ref, gksl_ref, gkwn_ref, cos_ref, sin_ref, bd_ref,
                 gsgu_ref, spw_ref, spb_ref, gomix_ref,
                 qn_ref, qr_ref, kc_ref, vc_ref, ksl_ref, vsl_ref, kwn_ref, vwn_ref, gate_ref, omix_ref):
    tm = x_ref.shape[1]
    i = pl.program_id(1)
    x = x_ref[0]
    ms = jnp.mean(x * x, axis=-1, keepdims=True)
    h = (x * lax.rsqrt(ms + EPS) * gmix_ref[...]).astype(_bf16)
    bd = bd_ref[...]
    cos = cos_ref[...]
    sin = sin_ref[...]

    zq = _dot(h, w_ref[:, _C_Q:_C_Q + ATTN_WIDTH])
    qn = zq * lax.rsqrt(_group_mean(zq * zq, bd) + EPS) * gq_ref[...]
    cos4 = jnp.concatenate([cos] * 4, axis=1)
    sin4 = jnp.concatenate([sin] * 4, axis=1)
    qr = qn * cos4 + _swap_halves(qn) * sin4
    zero_half = jnp.zeros((tm, HEAD_DIM), _bf16)
    for hq in range(N_ATTN_HEADS):
        sl = slice(hq * HEAD_DIM, (hq + 1) * HEAD_DIM)
        qn_ref[0, hq] = jnp.concatenate([qn[:, sl].astype(_bf16), zero_half], axis=1)
        qr_ref[0, hq] = jnp.concatenate([qr[:, sl].astype(_bf16), zero_half], axis=1)

    kc_ref[0] = _dot(h, w_ref[:, _C_KC:_C_KC + KV_WIDTH]).astype(_bf16)
    vc_ref[0] = _dot(h, w_ref[:, _C_VC:_C_VC + KV_WIDTH]).astype(_bf16)

    lane = lax.broadcasted_iota(jnp.int32, (tm, LANES), 1)
    ones_lane = jnp.where(lane == HEAD_DIM, 1.0, 0.0)
    for col, out_ref in ((_C_VSL, vsl_ref), (_C_VWN, vwn_ref)):
        zv2 = _dot(h, w_ref[:, col:col + KV_WIDTH])
        out_ref[0, 0] = jnp.where(lane < HEAD_DIM, zv2, ones_lane).astype(_bf16)
        out_ref[0, 1] = jnp.where(lane < HEAD_DIM, pltpu.roll(zv2, HEAD_DIM, 1), ones_lane).astype(_bf16)

    pos = i * tm + lax.broadcasted_iota(jnp.int32, (tm, LANES), 0)
    blk_onehot = jnp.where(lane - HEAD_DIM == pos // SEL_BLOCK, 1.0, 0.0)
    for col, g_ref, out_ref, extra in ((_C_KSL, gksl_ref, ksl_ref, blk_onehot),
                                       (_C_KWN, gkwn_ref, kwn_ref, jnp.zeros((tm, LANES), _f32))):
        zk = _dot(h, w_ref[:, col:col + KV_WIDTH])
        kn = zk * lax.rsqrt(_group_mean(zk * zk, bd) + EPS) * g_ref[...]
        kr = kn * cos + _swap_halves(kn) * sin
        out_ref[0, 0] = jnp.where(lane < HEAD_DIM, kr, extra).astype(_bf16)
        out_ref[0, 1] = jnp.where(lane < HEAD_DIM, pltpu.roll(kr, HEAD_DIM, 1), extra).astype(_bf16)

    gate_ref[0] = jax.nn.sigmoid(_dot(h, w_ref[:, _C_GATE:_C_GATE + LANES]))

    row = lax.broadcasted_iota(jnp.int32, (CHUNK, 2 * CHUNK), 0)
    colw = lax.broadcasted_iota(jnp.int32, (CHUNK, 2 * CHUNK), 1) % CHUNK
    causal_w = colw <= row
    lane_c = lax.broadcasted_iota(jnp.int32, (CHUNK, LANES), 1)
    for c in range(tm // CHUNK):
        rows = slice(c * CHUNK, (c + 1) * CHUNK)
        hc = h[rows]
        zu = jax.nn.gelu(_dot(hc, w_ref[:, _C_ZU:_C_ZU + MIX_WIDTH]))
        zv = jax.nn.gelu(_dot(hc, w_ref[:, _C_ZV:_C_ZV + MIX_WIDTH]))
        vn = zv * lax.rsqrt(_group_mean(zv * zv, bd) + EPS) * gsgu_ref[...]
        sv_parts = []
        for p in range(N_MIX_GROUPS // 2):
            vp = vn[:, p * LANES:(p + 1) * LANES]
            rhs = jnp.concatenate([jnp.where(lane_c < HEAD_DIM, vp, 0.0),
                                   jnp.where(lane_c < HEAD_DIM, 0.0, vp)], axis=0).astype(_bf16)
            w_pair = jnp.where(causal_w, spw_ref[p], jnp.zeros((), _bf16))
            sv_parts.append(_dot(w_pair, rhs))
        sv = jnp.concatenate(sv_parts, axis=1) + spb_ref[...]
        om = zu * sv
        oms = jnp.mean(om * om, axis=-1, keepdims=True)
        omix_ref[0, rows] = (om * lax.rsqrt(oms + EPS) * gomix_ref[...]).astype(_bf16)


def _proj_call(x, gmix, w_in_p, gq, gksl, gkwn, cos, sin, bd, gsgu, spw, spb, gomix):
    B, S, _ = x.shape
    tm = PROJ_TILE
    nt = S // tm
    const2 = lambda b, i: (0, 0)
    const3 = lambda b, i: (0, 0, 0)
    tok3 = lambda b, i: (b, i, 0)
    head4 = lambda b, i: (b, 0, i, 0)
    out_shape = (
        jax.ShapeDtypeStruct((B, N_ATTN_HEADS, S, LANES), _bf16),
        jax.ShapeDtypeStruct((B, N_ATTN_HEADS, S, LANES), _bf16),
        jax.ShapeDtypeStruct((B, S, KV_WIDTH), _bf16),
        jax.ShapeDtypeStruct((B, S, KV_WIDTH), _bf16),
        jax.ShapeDtypeStruct((B, N_KV_HEADS, S, LANES), _bf16),
        jax.ShapeDtypeStruct((B, N_KV_HEADS, S, LANES), _bf16),
        jax.ShapeDtypeStruct((B, N_KV_HEADS, S, LANES), _bf16),
        jax.ShapeDtypeStruct((B, N_KV_HEADS, S, LANES), _bf16),
        jax.ShapeDtypeStruct((B, S, LANES), _f32),
        jax.ShapeDtypeStruct((B, S, MIX_WIDTH), _bf16),
    )
    q_spec = pl.BlockSpec((1, N_ATTN_HEADS, tm, LANES), head4)
    k_spec = pl.BlockSpec((1, N_KV_HEADS, tm, LANES), head4)
    t_spec = pl.BlockSpec((1, tm, LANES), tok3)
    return pl.pallas_call(
        _proj_kernel,
        grid=(B, nt),
        in_specs=[
            pl.BlockSpec((1, tm, D_MODEL), tok3),
            pl.BlockSpec((1, D_MODEL), const2),
            pl.BlockSpec((D_MODEL, _IN_COLS_PACKED), const2),
            pl.BlockSpec((1, ATTN_WIDTH), const2),
            pl.BlockSpec((1, KV_WIDTH), const2),
            pl.BlockSpec((1, KV_WIDTH), const2),
            pl.BlockSpec((tm, LANES), lambda b, i: (i, 0)),
            pl.BlockSpec((tm, LANES), lambda b, i: (i, 0)),
            pl.BlockSpec((LANES, LANES), const2),
            pl.BlockSpec((1, MIX_WIDTH), const2),
            pl.BlockSpec((N_MIX_GROUPS // 2, CHUNK, 2 * CHUNK), const3),
            pl.BlockSpec((CHUNK, MIX_WIDTH), const2),
            pl.BlockSpec((1, MIX_WIDTH), const2),
        ],
        out_specs=(q_spec, q_spec, t_spec, t_spec, k_spec, k_spec, k_spec, k_spec, t_spec,
                   pl.BlockSpec((1, tm, MIX_WIDTH), tok3)),
        out_shape=out_shape,
        compiler_params=pltpu.CompilerParams(
            dimension_semantics=("parallel", "parallel"), vmem_limit_bytes=VMEM_LIMIT),
        name="proj",
    )(x, gmix, w_in_p, gq, gksl, gkwn, cos, sin, bd, gsgu, spw, spb, gomix)


def _compress_kernel(kc_ref, vc_ref, pe_ref, w1_ref, w2_ref, gk_ref, kcmp_ref, vcmp_ref):
    n_rows = kc_ref.shape[1]
    zero_half = jnp.zeros((n_rows, HEAD_DIM), _f32)
    for t, (src_ref, out_ref) in enumerate(((kc_ref, kcmp_ref), (vc_ref, vcmp_ref))):
        g = src_ref[0].astype(_f32)
        first = _dot((g + pe_ref[t, 0:1]).astype(_bf16), w1_ref[t, 0])
        second = _dot((g + pe_ref[t, 1:2]).astype(_bf16), w1_ref[t, 1])
        hid = jax.nn.gelu(first + pltpu.roll(second, n_rows - 1, 0)).astype(_bf16)
        heads = [_dot(hid[:, hh * CMP_HIDDEN:(hh + 1) * CMP_HIDDEN], w2_ref[t]) for hh in range(N_KV_HEADS)]
        if t == 0:
            heads = [c * lax.rsqrt(jnp.mean(c * c, axis=-1, keepdims=True) + EPS) * gk_ref[...] for c in heads]
        for hh in range(N_KV_HEADS):
            out_ref[0, hh] = jnp.concatenate([heads[hh], zero_half], axis=1).astype(_bf16)


def _compress_call(kc_g, vc_g, pe_rows, w1x, w2, gk0):
    B, n_rows, width = kc_g.shape
    return pl.pallas_call(
        _compress_kernel,
        grid=(B,),
        in_specs=[
            pl.BlockSpec((1, n_rows, width), lambda b: (b, 0, 0)),
            pl.BlockSpec((1, n_rows, width), lambda b: (b, 0, 0)),
            pl.BlockSpec((2, 2, width), lambda b: (0, 0, 0)),
            pl.BlockSpec((2, 2, width, N_KV_HEADS * CMP_HIDDEN), lambda b: (0, 0, 0, 0)),
            pl.BlockSpec((2, CMP_HIDDEN, HEAD_DIM), lambda b: (0, 0, 0)),
            pl.BlockSpec((1, HEAD_DIM), lambda b: (0, 0)),
        ],
        out_specs=(pl.BlockSpec((1, N_KV_HEADS, n_rows, LANES), lambda b: (b, 0, 0, 0)),
                   pl.BlockSpec((1, N_KV_HEADS, n_rows, LANES), lambda b: (b, 0, 0, 0))),
        out_shape=(jax.ShapeDtypeStruct((B, N_KV_HEADS, n_rows, LANES), _bf16),
                   jax.ShapeDtypeStruct((B, N_KV_HEADS, n_rows, LANES), _bf16)),
        compiler_params=pltpu.CompilerParams(
            dimension_semantics=("parallel",), vmem_limit_bytes=VMEM_LIMIT),
        name="compress",
    )(kc_g, vc_g, pe_rows, w1x, w2, gk0)


def _max_over_lane_tiles(s):
    m = s[:, 0:LANES]
    for t in range(1, s.shape[1] // LANES):
        m = jnp.maximum(m, s[:, t * LANES:(t + 1) * LANES])
    return m


def _attn_kernel(qn_ref, qr_ref, kcmp_ref, vcmp_ref, ksl_ref, vsl_ref, kwn_ref, vwn_ref, gate_ref,
                 ovt_ref, place_ref, mask_ref, gout_ref, o_ref, s_scr, p_scr, osel_scr, o_scr):
    tq = o_ref.shape[1]
    rows = GQA_GROUP * tq
    S = ksl_ref.shape[2]
    n_cmp_pad = kcmp_ref.shape[2]
    n_sel = S // SEL_BLOCK
    tiles_per_chunk = SEL_CHUNK // tq
    win_tiles = WINDOW // tq
    i = pl.program_id(1)
    q0 = i * tq

    qpos = q0 + lax.broadcasted_iota(jnp.int32, (rows, 1), 0) % tq
    blk = lax.broadcasted_iota(jnp.int32, (n_sel, tq), 0)
    tpos = q0 + lax.broadcasted_iota(jnp.int32, (n_sel, tq), 1)
    lane_t = lax.broadcasted_iota(jnp.int32, (tq, LANES), 1)
    gates = gate_ref[0]

    def tile_mask(key_tile, low_edge=False):
        kind = jnp.where(key_tile < i, MASK_FULL, jnp.where(key_tile == i, MASK_DIAG, MASK_NONE))
        if low_edge:
            kind = jnp.where(i >= win_tiles, MASK_LOW, kind)
        return mask_ref[kind]

    for h in range(N_KV_HEADS):
        qn = qn_ref[0, GQA_GROUP * h:GQA_GROUP * (h + 1)].reshape(rows, LANES)
        qr = qr_ref[0, GQA_GROUP * h:GQA_GROUP * (h + 1)].reshape(rows, LANES)

        s = _dot_nt(qn, kcmp_ref[0, h])
        n_idx = lax.broadcasted_iota(jnp.int32, (1, n_cmp_pad), 1)
        valid_c = n_idx * CMP_STRIDE + (CMP_BLOCK - 1) <= qpos
        s = jnp.where(valid_c, s, NEG_BIG)
        m = jnp.max(s, axis=-1, keepdims=True)
        e = jnp.where(valid_c, jnp.exp(s - m), 0.0)
        p_c = e / jnp.maximum(jnp.sum(e, axis=-1, keepdims=True), 1e-20)
        o_cmp = _dot(p_c.astype(_bf16), vcmp_ref[0, h])

        p_sum = p_c[0:tq] + p_c[tq:2 * tq] + p_c[2 * tq:3 * tq] + p_c[3 * tq:4 * tq]
        p_hi = p_sum.astype(_bf16)
        p_lo = (p_sum - p_hi.astype(_f32)).astype(_bf16)
        imp = _dot_nt(ovt_ref[...], p_hi) + _dot_nt(ovt_ref[...], p_lo)
        cur = tpos // SEL_BLOCK
        forced = (blk == 0) | (blk == cur) | (blk == cur - 1)
        eligible = blk * SEL_BLOCK <= tpos
        score = jnp.where(forced, FORCE_SCORE, jnp.where(eligible, imp, -jnp.inf))
        rank = jnp.zeros((n_sel, tq), _f32)
        for jp in range(n_sel):
            other = score[jp:jp + 1, :]
            tie = jnp.where(blk > jp, 1.0, 0.0)
            rank = rank + jnp.where(other > score, 1.0, jnp.where(other == score, tie, 0.0))
        chosen = (rank < float(SEL_TOPN)) & eligible
        sel_bias = jnp.where(chosen, 0.0, NEG_BIG).astype(_bf16)
        bias_row = lax.dot_general(sel_bias, place_ref[...], (((0,), (0,)), ((), ())),
                                   preferred_element_type=_f32).astype(_bf16)
        q_sel = qr + jnp.concatenate([bias_row] * GQA_GROUP, axis=0)

        def sel_branch(c, h=h, q_sel=q_sel):
            span = SEL_CHUNK * (c + 1)
            mx = None
            for j in range(c + 1):
                cols = slice(j * SEL_CHUNK, (j + 1) * SEL_CHUNK)
                sc = _dot_nt(q_sel, ksl_ref[0, h, cols, :])
                if j == c:
                    sc = jnp.concatenate(
                        [sc[:, t * tq:(t + 1) * tq] + tile_mask(tiles_per_chunk * c + t)
                         for t in range(tiles_per_chunk)], axis=1)
                s_scr[:, cols] = sc
                cm = _max_over_lane_tiles(sc)
                mx = cm if mx is None else jnp.maximum(mx, cm)
            m_sel = jnp.max(mx, axis=-1, keepdims=True)
            for j in range(c + 1):
                cols = slice(j * SEL_CHUNK, (j + 1) * SEL_CHUNK)
                p_scr[:, cols] = jnp.exp(s_scr[:, cols] - m_sel).astype(_bf16)
            acc = _dot(p_scr[:, 0:span], vsl_ref[0, h, 0:span, :])
            osel_scr[...] = acc / acc[:, HEAD_DIM:HEAD_DIM + 1]

        cls = q0 // SEL_CHUNK
        for c in range(S // SEL_CHUNK):
            pl.when(cls == c)(functools.partial(sel_branch, c))
        o_sel = osel_scr[...]

        span = WINDOW + tq
        first_tile = jnp.maximum(i - win_tiles, 0)
        w0 = pl.multiple_of(first_tile * tq, tq)
        sw = _dot_nt(qr, kwn_ref[0, h, pl.ds(w0, span), :])
        sw = jnp.concatenate(
            [sw[:, t * tq:(t + 1) * tq] + tile_mask(first_tile + t, low_edge=(t == 0))
             for t in range(span // tq)], axis=1)
        mw = jnp.max(sw, axis=-1, keepdims=True)
        pw = jnp.exp(sw - mw).astype(_bf16)
        acc_w = _dot(pw, vwn_ref[0, h, pl.ds(w0, span), :])
        o_win = acc_w / acc_w[:, HEAD_DIM:HEAD_DIM + 1]

        heads = []
        for g in range(GQA_GROUP):
            hq = GQA_GROUP * h + g
            r = slice(g * tq, (g + 1) * tq)
            heads.append(gates[:, hq:hq + 1] * o_cmp[r]
                         + gates[:, N_ATTN_HEADS + hq:N_ATTN_HEADS + hq + 1] * o_sel[r]
                         + gates[:, 2 * N_ATTN_HEADS + hq:2 * N_ATTN_HEADS + hq + 1] * o_win[r])
        for gp in range(GQA_GROUP // 2):
            pair = jnp.where(lane_t < HEAD_DIM, heads[2 * gp], pltpu.roll(heads[2 * gp + 1], HEAD_DIM, 1))
            c = (GQA_GROUP // 2) * h + gp
            o_scr[:, c * LANES:(c + 1) * LANES] = pair

    o = o_scr[...]
    ms = jnp.mean(o * o, axis=-1, keepdims=True)
    o_ref[0] = (o * lax.rsqrt(ms + EPS) * gout_ref[...]).astype(_bf16)


def _attn_call(qn, qr, kcmp, vcmp, ksl, vsl, kwn, vwn, gates, ovt, place, masks, gout_attn):
    B, _, S, _ = qn.shape
    tq = ATTN_Q_TILE
    assert tq == LANES and S % SEL_CHUNK == 0 and SEL_CHUNK % tq == 0 and WINDOW % tq == 0
    n_cmp_pad = kcmp.shape[2]
    n_sel = S // SEL_BLOCK
    rows = GQA_GROUP * tq
    per_b4 = lambda b, i: (b, 0, 0, 0)
    const2 = lambda b, i: (0, 0)
    return pl.pallas_call(
        _attn_kernel,
        grid=(B, S // tq),
        in_specs=[
            pl.BlockSpec((1, N_ATTN_HEADS, tq, LANES), lambda b, i: (b, 0, i, 0)),
            pl.BlockSpec((1, N_ATTN_HEADS, tq, LANES), lambda b, i: (b, 0, i, 0)),
            pl.BlockSpec((1, N_KV_HEADS, n_cmp_pad, LANES), per_b4),
            pl.BlockSpec((1, N_KV_HEADS, n_cmp_pad, LANES), per_b4),
            pl.BlockSpec((1, N_KV_HEADS, S, LANES), per_b4),
            pl.BlockSpec((1, N_KV_HEADS, S, LANES), per_b4),
            pl.BlockSpec((1, N_KV_HEADS, S, LANES), per_b4),
            pl.BlockSpec((1, N_KV_HEADS, S, LANES), per_b4),
            pl.BlockSpec((1, tq, LANES), lambda b, i: (b, i, 0)),
            pl.BlockSpec((n_sel, n_cmp_pad), const2),
            pl.BlockSpec((n_sel, LANES), const2),
            pl.BlockSpec((4, rows, LANES), lambda b, i: (0, 0, 0)),
            pl.BlockSpec((1, ATTN_WIDTH), const2),
        ],
        out_specs=pl.BlockSpec((1, tq, ATTN_WIDTH), lambda b, i: (b, i, 0)),
        out_shape=jax.ShapeDtypeStruct((B, S, ATTN_WIDTH), _bf16),
        scratch_shapes=[
            pltpu.VMEM((rows, S), _f32),
            pltpu.VMEM((rows, S), _bf16),
            pltpu.VMEM((rows, LANES), _f32),
            pltpu.VMEM((tq, ATTN_WIDTH), _f32),
        ],
        compiler_params=pltpu.CompilerParams(
            dimension_semantics=("parallel", "arbitrary"), vmem_limit_bytes=VMEM_LIMIT),
        name="attn",
    )(qn, qr, kcmp, vcmp, ksl, vsl, kwn, vwn, gates, ovt, place, masks, gout_attn)


def _ffn_kernel(x_ref, oa_ref, om_ref, wo_ref, gffn_ref, w1_ref, w2_ref, out_ref, act_scr):
    x2 = (x_ref[...] + _dot(oa_ref[...], wo_ref[0:ATTN_WIDTH, :])
          + _dot(om_ref[...], wo_ref[ATTN_WIDTH:ATTN_WIDTH + MIX_WIDTH, :]))
    ms = jnp.mean(x2 * x2, axis=-1, keepdims=True)
    h = (x2 * lax.rsqrt(ms + EPS) * gffn_ref[...]).astype(_bf16)
    for c in range(D_FF // FF_CHUNK):
        cols = slice(c * FF_CHUNK, (c + 1) * FF_CHUNK)
        a = jnp.maximum(_dot(h, w1_ref[:, cols]), 0.0)
        act_scr[:, cols] = (a * a).astype(_bf16)
    out_ref[...] = x2 + _dot(act_scr[...], w2_ref[...])


def _ffn_call(x2d, oa, om, wo, gffn, w1, w2):
    T = x2d.shape[0]
    tm = FFN_TILE
    tok = lambda i: (i, 0)
    const = lambda i: (0, 0)
    return pl.pallas_call(
        _ffn_kernel,
        grid=(T // tm,),
        in_specs=[
            pl.BlockSpec((tm, D_MODEL), tok),
            pl.BlockSpec((tm, ATTN_WIDTH), tok),
            pl.BlockSpec((tm, MIX_WIDTH), tok),
            pl.BlockSpec((D_MODEL, D_MODEL), const),
            pl.BlockSpec((1, D_MODEL), const),
            pl.BlockSpec((D_MODEL, D_FF), const),
            pl.BlockSpec((D_FF, D_MODEL), const),
        ],
        out_specs=pl.BlockSpec((tm, D_MODEL), tok),
        out_shape=jax.ShapeDtypeStruct((T, D_MODEL), _f32),
        scratch_shapes=[pltpu.VMEM((tm, D_FF), _bf16)],
        compiler_params=pltpu.CompilerParams(
            dimension_semantics=("parallel",), vmem_limit_bytes=VMEM_LIMIT),
        name="ffn",
    )(x2d, oa, om, wo, gffn, w1, w2)


def _constants(S):
    half = HEAD_DIM // 2
    inv = ROPE_THETA ** (-jnp.arange(half, dtype=_f32) / half)
    ang = jnp.arange(S, dtype=_f32)[:, None] * inv[None, :]
    cos = jnp.cos(ang)
    sin = jnp.sin(ang)
    cos2 = jnp.concatenate([cos, cos, cos, cos], axis=1)
    sin2 = jnp.concatenate([-sin, sin, -sin, sin], axis=1)
    lane = np.arange(LANES)
    bd = (lane[:, None] // HEAD_DIM == lane[None, :] // HEAD_DIM).astype(np.float32) / HEAD_DIM
    n_cmp = (S - CMP_BLOCK) // CMP_STRIDE + 1
    n_sel = S // SEL_BLOCK
    n_cmp_pad = S // CMP_STRIDE
    cmp_start = np.arange(n_cmp)[:, None] * CMP_STRIDE
    sel_start = np.arange(n_sel)[None, :] * SEL_BLOCK
    overlap = np.clip(np.minimum(cmp_start + CMP_BLOCK, sel_start + SEL_BLOCK)
                      - np.maximum(cmp_start, sel_start), 0, None).astype(np.float32) / CMP_BLOCK
    ovt = np.zeros((n_sel, n_cmp_pad), np.float32)
    ovt[:, :n_cmp] = overlap.T
    place = np.zeros((n_sel, LANES), np.float32)
    place[np.arange(n_sel), HEAD_DIM + np.arange(n_sel)] = 1.0
    qq = (np.arange(GQA_GROUP * ATTN_Q_TILE) % ATTN_Q_TILE)[:, None]
    kk = lane[None, :]
    masks = np.zeros((4, GQA_GROUP * ATTN_Q_TILE, LANES), np.float32)
    masks[MASK_DIAG] = np.where(kk <= qq, 0.0, NEG_BIG)
    masks[MASK_LOW] = np.where(kk > qq, 0.0, NEG_BIG)
    masks[MASK_NONE] = NEG_BIG
    return (cos2, sin2, jnp.asarray(bd, _bf16), jnp.asarray(ovt, _bf16), jnp.asarray(place, _bf16),
            jnp.asarray(masks, _f32))


def _pack_w_in(w):
    gate0 = ATTN_WIDTH + 6 * KV_WIDTH
    n_g = N_GATES * N_ATTN_HEADS
    gates = w[:, gate0:gate0 + n_g].reshape(D_MODEL, N_ATTN_HEADS, N_GATES)
    gates = jnp.transpose(gates, (0, 2, 1)).reshape(D_MODEL, n_g)
    pad = jnp.zeros((D_MODEL, _IN_COLS_PACKED - _C_GATE - n_g), w.dtype)
    return jnp.concatenate([w[:, :gate0], w[:, gate0 + n_g:], gates, pad], axis=1).astype(_bf16)


def _expand_cmp_w1(w1):
    w = w1.reshape(2, CMP_STRIDE, HEAD_DIM, CMP_HIDDEN)
    z = jnp.zeros_like(w)
    h0 = jnp.concatenate([w, z], axis=-1)
    h1 = jnp.concatenate([z, w], axis=-1)
    both = jnp.stack([h0, h1], axis=2)
    return both.reshape(2, CMP_STRIDE * KV_WIDTH, N_KV_HEADS * CMP_HIDDEN).astype(_bf16)


def _expand_pe(pe):
    p = pe.reshape(2, CMP_STRIDE, 1, HEAD_DIM)
    return jnp.broadcast_to(p, (2, CMP_STRIDE, N_KV_HEADS, HEAD_DIM)).reshape(2, CMP_STRIDE * KV_WIDTH)


def _forward(x, g_mix_norm, w_in, g_q, g_k, cmp_pe, cmp_w1, cmp_w2, g_sgu, sp_w, sp_b, g_out, w_out,
             g_ffn_norm, w_ff1, w_ff2):
    B, S, _ = x.shape
    l = 0
    scale = HEAD_DIM ** -0.5
    cos2, sin2, bd, ovt, place, masks = _constants(S)

    w_in_p = _pack_w_in(w_in[l])
    gq = (jnp.tile(g_q[l], N_ATTN_HEADS) * scale)[None, :]
    gksl = jnp.tile(g_k[l, 1], N_KV_HEADS)[None, :]
    gkwn = jnp.tile(g_k[l, 2], N_KV_HEADS)[None, :]
    gsgu = g_sgu[l].reshape(1, MIX_WIDTH)
    spw = sp_w[l].reshape(N_MIX_GROUPS // 2, 2, CHUNK, CHUNK)
    spw = jnp.transpose(spw, (0, 2, 1, 3)).reshape(N_MIX_GROUPS // 2, CHUNK, 2 * CHUNK).astype(_bf16)
    spb = jnp.repeat(sp_b[l].T, HEAD_DIM, axis=1)
    gout = g_out[l]

    qn, qr, kc, vc, ksl, vsl, kwn, vwn, gates, omix = _proj_call(
        x, g_mix_norm[l][None, :], w_in_p, gq, gksl, gkwn, cos2, sin2, bd, gsgu, spw, spb,
        gout[None, ATTN_WIDTH:])

    n_rows = S // CMP_STRIDE
    pe_rows = jnp.stack([_expand_pe(cmp_pe[l, 0]), _expand_pe(cmp_pe[l, 1])])
    w1x = jnp.stack([_expand_cmp_w1(cmp_w1[l, 0]), _expand_cmp_w1(cmp_w1[l, 1])])
    kcmp, vcmp = _compress_call(
        kc.reshape(B, n_rows, CMP_STRIDE * KV_WIDTH), vc.reshape(B, n_rows, CMP_STRIDE * KV_WIDTH),
        pe_rows, w1x, cmp_w2[l].astype(_bf16), g_k[l, 0][None, :])

    oattn = _attn_call(qn, qr, kcmp, vcmp, ksl, vsl, kwn, vwn, gates, ovt, place, masks,
                       gout[None, :ATTN_WIDTH])

    out = _ffn_call(x.reshape(B * S, D_MODEL), oattn.reshape(B * S, ATTN_WIDTH),
                    omix.reshape(B * S, MIX_WIDTH), w_out[l].astype(_bf16), g_ffn_norm[l][None, :],
                    w_ff1[l].astype(_bf16), w_ff2[l].astype(_bf16))
    return out.reshape(B, S, D_MODEL), (qn, qr, kc, vc, ksl, vsl, kwn, vwn, gates, omix, kcmp, vcmp, oattn)


def kernel(x, g_mix_norm, w_in, g_q, g_k, cmp_pe, cmp_w1, cmp_w2, g_sgu, sp_w, sp_b, g_out, w_out,
           g_ffn_norm, w_ff1, w_ff2):
    return _forward(x, g_mix_norm, w_in, g_q, g_k, cmp_pe, cmp_w1, cmp_w2, g_sgu, sp_w, sp_b, g_out, w_out,
                    g_ffn_norm, w_ff1, w_ff2)[0]
```

```python
import functools

import numpy as np
import jax
import jax.numpy as jnp
from jax import lax
from jax.experimental import pallas as pl
from jax.experimental.pallas import tpu as pltpu

D_MODEL = 1024
HEAD_DIM = 64
N_ATTN_HEADS = 8
N_MIX_GROUPS = 8
GQA_GROUP = 4
N_KV_HEADS = 2
ATTN_WIDTH = 512
MIX_WIDTH = 512
KV_WIDTH = 128
N_GATES = 3
CMP_BLOCK = 32
CMP_STRIDE = 16
CMP_HIDDEN = 256
SEL_BLOCK = 64
SEL_TOPN = 16
WINDOW = 512
CHUNK = 128
D_FF = 4 * D_MODEL
ROPE_THETA = 10000.0
EPS = 1e-6
FORCE_SCORE = 1e9

LANES = 128
NEG_BIG = -1e30
PROJ_TILE = 256
ATTN_Q_TILE = 128
SEL_CHUNK = 512
MASK_FULL, MASK_DIAG, MASK_LOW, MASK_NONE = 0, 1, 2, 3
FFN_TILE = 256
FF_CHUNK = 1024
VMEM_LIMIT = 56 * 1024 * 1024

_C_Q = 0
_C_KC = 512
_C_VC = 640
_C_KSL = 768
_C_VSL = 896
_C_KWN = 1024
_C_VWN = 1152
_C_ZU = 1280
_C_ZV = 1792
_C_GATE = 2304
_IN_COLS_PACKED = 2432

_bf16 = jnp.bfloat16
_f32 = jnp.float32


def _dot(a, b):
    return jnp.dot(a, b, preferred_element_type=_f32)


def _dot_nt(a, b):
    return lax.dot_general(a, b, (((1,), (1,)), ((), ())), preferred_element_type=_f32)


def _split_dot(v, m):
    hi = v.astype(_bf16)
    lo = (v - hi.astype(_f32)).astype(_bf16)
    return _dot(hi, m) + _dot(lo, m)


def _group_mean(sq, bd):
    parts = [_split_dot(sq[:, c * LANES:(c + 1) * LANES], bd) for c in range(sq.shape[1] // LANES)]
    return parts[0] if len(parts) == 1 else jnp.concatenate(parts, axis=1)


def _swap_halves(v):
    n = v.shape[1]
    lane = lax.broadcasted_iota(jnp.int32, v.shape, 1)
    return jnp.where((lane % HEAD_DIM) < HEAD_DIM // 2, pltpu.roll(v, n - HEAD_DIM // 2, 1),
                     pltpu.roll(v, HEAD_DIM // 2, 1))


def _proj_kernel(x_ref, gmix_ref, w_ref, gq_ref, gksl_ref, gkwn_ref, cos_ref, sin_ref, bd_ref,
                 gsgu_ref, spw_ref, spb_ref, gomix_ref,
                 qn_ref, qr_ref, kc_ref, vc_ref, ksl_ref, vsl_ref, kwn_ref, vwn_ref, gate_ref, omix_ref):
    tm = x_ref.shape[1]
    i = pl.program_id(1)
    x = x_ref[0]
    ms = jnp.mean(x * x, axis=-1, keepdims=True)
    h = (x * lax.rsqrt(ms + EPS) * gmix_ref[...]).astype(_bf16)
    bd = bd_ref[...]
    cos = cos_ref[...]
    sin = sin_ref[...]

    zq = _dot(h, w_ref[:, _C_Q:_C_Q + ATTN_WIDTH])
    qn = zq * lax.rsqrt(_group_mean(zq * zq, bd) + EPS) * gq_ref[...]
    cos4 = jnp.concatenate([cos] * 4, axis=1)
    sin4 = jnp.concatenate([sin] * 4, axis=1)
    qr = qn * cos4 + _swap_halves(qn) * sin4
    zero_half = jnp.zeros((tm, HEAD_DIM), _bf16)
    for hq in range(N_ATTN_HEADS):
        sl = slice(hq * HEAD_DIM, (hq + 1) * HEAD_DIM)
        qn_ref[0, hq] = jnp.concatenate([qn[:, sl].astype(_bf16), zero_half], axis=1)
        qr_ref[0, hq] = jnp.concatenate([qr[:, sl].astype(_bf16), zero_half], axis=1)

    kc_ref[0] = _dot(h, w_ref[:, _C_KC:_C_KC + KV_WIDTH]).astype(_bf16)
    vc_ref[0] = _dot(h, w_ref[:, _C_VC:_C_VC + KV_WIDTH]).astype(_bf16)

    lane = lax.broadcasted_iota(jnp.int32, (tm, LANES), 1)
    ones_lane = jnp.where(lane == HEAD_DIM, 1.0, 0.0)
    for col, out_ref in ((_C_VSL, vsl_ref), (_C_VWN, vwn_ref)):
        zv2 = _dot(h, w_ref[:, col:col + KV_WIDTH])
        out_ref[0, 0] = jnp.where(lane < HEAD_DIM, zv2, ones_lane).astype(_bf16)
        out_ref[0, 1] = jnp.where(lane < HEAD_DIM, pltpu.roll(zv2, HEAD_DIM, 1), ones_lane).astype(_bf16)

    pos = i * tm + lax.broadcasted_iota(jnp.int32, (tm, LANES), 0)
    blk_onehot = jnp.where(lane - HEAD_DIM == pos // SEL_BLOCK, 1.0, 0.0)
    for col, g_ref, out_ref, extra in ((_C_KSL, gksl_ref, ksl_ref, blk_onehot),
                                       (_C_KWN, gkwn_ref, kwn_ref, jnp.zeros((tm, LANES), _f32))):
        zk = _dot(h, w_ref[:, col:col + KV_WIDTH])
        kn = zk * lax.rsqrt(_group_mean(zk * zk, bd) + EPS) * g_ref[...]
        kr = kn * cos + _swap_halves(kn) * sin
        out_ref[0, 0] = jnp.where(lane < HEAD_DIM, kr, extra).astype(_bf16)
        out_ref[0, 1] = jnp.where(lane < HEAD_DIM, pltpu.roll(kr, HEAD_DIM, 1), extra).astype(_bf16)

    gate_ref[0] = jax.nn.sigmoid(_dot(h, w_ref[:, _C_GATE:_C_GATE + LANES]))

    row = lax.broadcasted_iota(jnp.int32, (CHUNK, 2 * CHUNK), 0)
    colw = lax.broadcasted_iota(jnp.int32, (CHUNK, 2 * CHUNK), 1) % CHUNK
    causal_w = colw <= row
    lane_c = lax.broadcasted_iota(jnp.int32, (CHUNK, LANES), 1)
    for c in range(tm // CHUNK):
        rows = slice(c * CHUNK, (c + 1) * CHUNK)
        hc = h[rows]
        zu = jax.nn.gelu(_dot(hc, w_ref[:, _C_ZU:_C_ZU + MIX_WIDTH]))
        zv = jax.nn.gelu(_dot(hc, w_ref[:, _C_ZV:_C_ZV + MIX_WIDTH]))
        vn = zv * lax.rsqrt(_group_mean(zv * zv, bd) + EPS) * gsgu_ref[...]
        sv_parts = []
        for p in range(N_MIX_GROUPS // 2):
            vp = vn[:, p * LANES:(p + 1) * LANES]
            rhs = jnp.concatenate([jnp.where(lane_c < HEAD_DIM, vp, 0.0),
                                   jnp.where(lane_c < HEAD_DIM, 0.0, vp)], axis=0).astype(_bf16)
            w_pair = jnp.where(causal_w, spw_ref[p], jnp.zeros((), _bf16))
            sv_parts.append(_dot(w_pair, rhs))
        sv = jnp.concatenate(sv_parts, axis=1) + spb_ref[...]
        om = zu * sv
        oms = jnp.mean(om * om, axis=-1, keepdims=True)
        omix_ref[0, rows] = (om * lax.rsqrt(oms + EPS) * gomix_ref[...]).astype(_bf16)


def _proj_call(x, gmix, w_in_p, gq, gksl, gkwn, cos, sin, bd, gsgu, spw, spb, gomix):
    B, S, _ = x.shape
    tm = PROJ_TILE
    nt = S // tm
    const2 = lambda b, i: (0, 0)
    const3 = lambda b, i: (0, 0, 0)
    tok3 = lambda b, i: (b, i, 0)
    head4 = lambda b, i: (b, 0, i, 0)
    out_shape = (
        jax.ShapeDtypeStruct((B, N_ATTN_HEADS, S, LANES), _bf16),
        jax.ShapeDtypeStruct((B, N_ATTN_HEADS, S, LANES), _bf16),
        jax.ShapeDtypeStruct((B, S, KV_WIDTH), _bf16),
        jax.ShapeDtypeStruct((B, S, KV_WIDTH), _bf16),
        jax.ShapeDtypeStruct((B, N_KV_HEADS, S, LANES), _bf16),
        jax.ShapeDtypeStruct((B, N_KV_HEADS, S, LANES), _bf16),
        jax.ShapeDtypeStruct((B, N_KV_HEADS, S, LANES), _bf16),
        jax.ShapeDtypeStruct((B, N_KV_HEADS, S, LANES), _bf16),
        jax.ShapeDtypeStruct((B, S, LANES), _f32),
        jax.ShapeDtypeStruct((B, S, MIX_WIDTH), _bf16),
    )
    q_spec = pl.BlockSpec((1, N_ATTN_HEADS, tm, LANES), head4)
    k_spec = pl.BlockSpec((1, N_KV_HEADS, tm, LANES), head4)
    t_spec = pl.BlockSpec((1, tm, LANES), tok3)
    return pl.pallas_call(
        _proj_kernel,
        grid=(B, nt),
        in_specs=[
            pl.BlockSpec((1, tm, D_MODEL), tok3),
            pl.BlockSpec((1, D_MODEL), const2),
            pl.BlockSpec((D_MODEL, _IN_COLS_PACKED), const2),
            pl.BlockSpec((1, ATTN_WIDTH), const2),
            pl.BlockSpec((1, KV_WIDTH), const2),
            pl.BlockSpec((1, KV_WIDTH), const2),
            pl.BlockSpec((tm, LANES), lambda b, i: (i, 0)),
            pl.BlockSpec((tm, LANES), lambda b, i: (i, 0)),
            pl.BlockSpec((LANES, LANES), const2),
            pl.BlockSpec((1, MIX_WIDTH), const2),
            pl.BlockSpec((N_MIX_GROUPS // 2, CHUNK, 2 * CHUNK), const3),
            pl.BlockSpec((CHUNK, MIX_WIDTH), const2),
            pl.BlockSpec((1, MIX_WIDTH), const2),
        ],
        out_specs=(q_spec, q_spec, t_spec, t_spec, k_spec, k_spec, k_spec, k_spec, t_spec,
                   pl.BlockSpec((1, tm, MIX_WIDTH), tok3)),
        out_shape=out_shape,
        compiler_params=pltpu.CompilerParams(
            dimension_semantics=("parallel", "parallel"), vmem_limit_bytes=VMEM_LIMIT),
        name="proj",
    )(x, gmix, w_in_p, gq, gksl, gkwn, cos, sin, bd, gsgu, spw, spb, gomix)


def _compress_kernel(kc_ref, vc_ref, pe_ref, w1_ref, w2_ref, gk_ref, kcmp_ref, vcmp_ref):
    n_rows = kc_ref.shape[1]
    zero_half = jnp.zeros((n_rows, HEAD_DIM), _f32)
    for t, (src_ref, out_ref) in enumerate(((kc_ref, kcmp_ref), (vc_ref, vcmp_ref))):
        g = src_ref[0].astype(_f32)
        first = _dot((g + pe_ref[t, 0:1]).astype(_bf16), w1_ref[t, 0])
        second = _dot((g + pe_ref[t, 1:2]).astype(_bf16), w1_ref[t, 1])
        hid = jax.nn.gelu(first + pltpu.roll(second, n_rows - 1, 0)).astype(_bf16)
        heads = [_dot(hid[:, hh * CMP_HIDDEN:(hh + 1) * CMP_HIDDEN], w2_ref[t]) for hh in range(N_KV_HEADS)]
        if t == 0:
            heads = [c * lax.rsqrt(jnp.mean(c * c, axis=-1, keepdims=True) + EPS) * gk_ref[...] for c in heads]
        for hh in range(N_KV_HEADS):
            out_ref[0, hh] = jnp.concatenate([heads[hh], zero_half], axis=1).astype(_bf16)


def _compress_call(kc_g, vc_g, pe_rows, w1x, w2, gk0):
    B, n_rows, width = kc_g.shape
    return pl.pallas_call(
        _compress_kernel,
        grid=(B,),
        in_specs=[
            pl.BlockSpec((1, n_rows, width), lambda b: (b, 0, 0)),
            pl.BlockSpec((1, n_rows, width), lambda b: (b, 0, 0)),
            pl.BlockSpec((2, 2, width), lambda b: (0, 0, 0)),
            pl.BlockSpec((2, 2, width, N_KV_HEADS * CMP_HIDDEN), lambda b: (0, 0, 0, 0)),
            pl.BlockSpec((2, CMP_HIDDEN, HEAD_DIM), lambda b: (0, 0, 0)),
            pl.BlockSpec((1, HEAD_DIM), lambda b: (0, 0)),
        ],
        out_specs=(pl.BlockSpec((1, N_KV_HEADS, n_rows, LANES), lambda b: (b, 0, 0, 0)),
                   pl.BlockSpec((1, N_KV_HEADS, n_rows, LANES), lambda b: (b, 0, 0, 0))),
        out_shape=(jax.ShapeDtypeStruct((B, N_KV_HEADS, n_rows, LANES), _bf16),
                   jax.ShapeDtypeStruct((B, N_KV_HEADS, n_rows, LANES), _bf16)),
        compiler_params=pltpu.CompilerParams(
            dimension_semantics=("parallel",), vmem_limit_bytes=VMEM_LIMIT),
        name="compress",
    )(kc_g, vc_g, pe_rows, w1x, w2, gk0)


def _max_over_lane_tiles(s):
    m = s[:, 0:LANES]
    for t in range(1, s.shape[1] // LANES):
        m = jnp.maximum(m, s[:, t * LANES:(t + 1) * LANES])
    return m


def _attn_kernel(qn_ref, qr_ref, kcmp_ref, vcmp_ref, ksl_ref, vsl_ref, kwn_ref, vwn_ref, gate_ref,
                 ovt_ref, place_ref, mask_ref, gout_ref, o_ref, s_scr, p_scr):
    tq = o_ref.shape[1]
    rows = GQA_GROUP * tq
    S = ksl_ref.shape[2]
    n_cmp_pad = kcmp_ref.shape[2]
    n_sel = S // SEL_BLOCK
    tiles_per_chunk = SEL_CHUNK // tq
    win_tiles = WINDOW // tq
    i = pl.program_id(1)
    q0 = i * tq

    qpos = q0 + lax.broadcasted_iota(jnp.int32, (rows, 1), 0) % tq
    blk = lax.broadcasted_iota(jnp.int32, (n_sel, tq), 0)
    tpos = q0 + lax.broadcasted_iota(jnp.int32, (n_sel, tq), 1)
    lane_t = lax.broadcasted_iota(jnp.int32, (tq, LANES), 1)
    gates = gate_ref[0]

    def tile_mask(key_tile, low_edge=False):
        kind = jnp.where(key_tile < i, MASK_FULL, jnp.where(key_tile == i, MASK_DIAG, MASK_NONE))
        if low_edge:
            kind = jnp.where(i >= win_tiles, MASK_LOW, kind)
        return mask_ref[kind]

    def window_scores(h):
        qr = qr_ref[0, GQA_GROUP * h:GQA_GROUP * (h + 1)].reshape(rows, LANES)
        first_tile = jnp.maximum(i - win_tiles, 0)
        w0 = pl.multiple_of(first_tile * tq, tq)
        sw = _dot_nt(qr, kwn_ref[0, h, pl.ds(w0, WINDOW + tq), :])
        return jnp.concatenate(
            [sw[:, t * tq:(t + 1) * tq] + tile_mask(first_tile + t, low_edge=(t == 0))
             for t in range(win_tiles + 1)], axis=1)

    def window_out(h, sw):
        w0 = pl.multiple_of(jnp.maximum(i - win_tiles, 0) * tq, tq)
        mw = jnp.max(sw, axis=-1, keepdims=True)
        pw = jnp.exp(sw - mw).astype(_bf16)
        acc_w = _dot(pw, vwn_ref[0, h, pl.ds(w0, WINDOW + tq), :])
        return acc_w / acc_w[:, HEAD_DIM:HEAD_DIM + 1]

    def compressed_and_select(h):
        qn = qn_ref[0, GQA_GROUP * h:GQA_GROUP * (h + 1)].reshape(rows, LANES)
        qr = qr_ref[0, GQA_GROUP * h:GQA_GROUP * (h + 1)].reshape(rows, LANES)

        s = _dot_nt(qn, kcmp_ref[0, h])
        n_idx = lax.broadcasted_iota(jnp.int32, (1, n_cmp_pad), 1)
        valid_c = n_idx * CMP_STRIDE + (CMP_BLOCK - 1) <= qpos
        s = jnp.where(valid_c, s, NEG_BIG)
        m = jnp.max(s, axis=-1, keepdims=True)
        e = jnp.where(valid_c, jnp.exp(s - m), 0.0)
        p_c = e / jnp.maximum(jnp.sum(e, axis=-1, keepdims=True), 1e-20)
        o_cmp = _dot(p_c.astype(_bf16), vcmp_ref[0, h])

        p_sum = p_c[0:tq] + p_c[tq:2 * tq] + p_c[2 * tq:3 * tq] + p_c[3 * tq:4 * tq]
        p_hi = p_sum.astype(_bf16)
        p_lo = (p_sum - p_hi.astype(_f32)).astype(_bf16)
        imp = _dot_nt(ovt_ref[...], p_hi) + _dot_nt(ovt_ref[...], p_lo)
        cur = tpos // SEL_BLOCK
        forced = (blk == 0) | (blk == cur) | (blk == cur - 1)
        eligible = blk * SEL_BLOCK <= tpos
        score = jnp.where(forced, FORCE_SCORE, jnp.where(eligible, imp, -jnp.inf))
        rank = jnp.zeros((n_sel, tq), _f32)
        for jp in range(n_sel):
            other = score[jp:jp + 1, :]
            tie = jnp.where(blk > jp, 1.0, 0.0)
            rank = rank + jnp.where(other > score, 1.0, jnp.where(other == score, tie, 0.0))
        chosen = (rank < float(SEL_TOPN)) & eligible
        sel_bias = jnp.where(chosen, 0.0, NEG_BIG).astype(_bf16)
        bias_row = lax.dot_general(sel_bias, place_ref[...], (((0,), (0,)), ((), ())),
                                   preferred_element_type=_f32).astype(_bf16)
        return o_cmp, qr + jnp.concatenate([bias_row] * GQA_GROUP, axis=0)

    def step_body(c):
        span = SEL_CHUNK * (c + 1)
        sws = [window_scores(h) for h in range(N_KV_HEADS)]
        o_cmps, q_sels = zip(*[compressed_and_select(h) for h in range(N_KV_HEADS)])
        o_wins = [window_out(h, sws[h]) for h in range(N_KV_HEADS)]

        m_sels = []
        for h in range(N_KV_HEADS):
            q_sel = q_sels[h]
            mx = None
            for j in range(c + 1):
                cols = slice(j * SEL_CHUNK, (j + 1) * SEL_CHUNK)
                sc = _dot_nt(q_sel, ksl_ref[0, h, cols, :])
                if j == c:
                    sc = jnp.concatenate(
                        [sc[:, t * tq:(t + 1) * tq] + tile_mask(tiles_per_chunk * c + t)
                         for t in range(tiles_per_chunk)], axis=1)
                s_scr[h, :, cols] = sc
                cm = _max_over_lane_tiles(sc)
                mx = cm if mx is None else jnp.maximum(mx, cm)
            m_sels.append(jnp.max(mx, axis=-1, keepdims=True))
        o_sels = []
        for h in range(N_KV_HEADS):
            m_sel = m_sels[h]
            for j in range(c + 1):
                cols = slice(j * SEL_CHUNK, (j + 1) * SEL_CHUNK)
                p_scr[h, :, cols] = jnp.exp(s_scr[h, :, cols] - m_sel).astype(_bf16)
            acc = _dot(p_scr[h, :, 0:span], vsl_ref[0, h, 0:span, :])
            o_sels.append(acc / acc[:, HEAD_DIM:HEAD_DIM + 1])

        pairs = []
        for h in range(N_KV_HEADS):
            heads = []
            for g in range(GQA_GROUP):
                hq = GQA_GROUP * h + g
                r = slice(g * tq, (g + 1) * tq)
                heads.append(gates[:, hq:hq + 1] * o_cmps[h][r]
                             + gates[:, N_ATTN_HEADS + hq:N_ATTN_HEADS + hq + 1] * o_sels[h][r]
                             + gates[:, 2 * N_ATTN_HEADS + hq:2 * N_ATTN_HEADS + hq + 1] * o_wins[h][r])
            for gp in range(GQA_GROUP // 2):
                pairs.append(jnp.where(lane_t < HEAD_DIM, heads[2 * gp],
                                       pltpu.roll(heads[2 * gp + 1], HEAD_DIM, 1)))
        o = jnp.concatenate(pairs, axis=1)
        ms = jnp.mean(o * o, axis=-1, keepdims=True)
        o_ref[0] = (o * lax.rsqrt(ms + EPS) * gout_ref[...]).astype(_bf16)

    cls = q0 // SEL_CHUNK
    for c in range(S // SEL_CHUNK):
        pl.when(cls == c)(functools.partial(step_body, c))


def _attn_call(qn, qr, kcmp, vcmp, ksl, vsl, kwn, vwn, gates, ovt, place, masks, gout_attn):
    B, _, S, _ = qn.shape
    tq = ATTN_Q_TILE
    assert tq == LANES and S % SEL_CHUNK == 0 and SEL_CHUNK % tq == 0 and WINDOW % tq == 0
    n_cmp_pad = kcmp.shape[2]
    n_sel = S // SEL_BLOCK
    rows = GQA_GROUP * tq
    per_b4 = lambda b, i: (b, 0, 0, 0)
    const2 = lambda b, i: (0, 0)
    return pl.pallas_call(
        _attn_kernel,
        grid=(B, S // tq),
        in_specs=[
            pl.BlockSpec((1, N_ATTN_HEADS, tq, LANES), lambda b, i: (b, 0, i, 0)),
            pl.BlockSpec((1, N_ATTN_HEADS, tq, LANES), lambda b, i: (b, 0, i, 0)),
            pl.BlockSpec((1, N_KV_HEADS, n_cmp_pad, LANES), per_b4),
            pl.BlockSpec((1, N_KV_HEADS, n_cmp_pad, LANES), per_b4),
            pl.BlockSpec((1, N_KV_HEADS, S, LANES), per_b4),
            pl.BlockSpec((1, N_KV_HEADS, S, LANES), per_b4),
            pl.BlockSpec((1, N_KV_HEADS, S, LANES), per_b4),
            pl.BlockSpec((1, N_KV_HEADS, S, LANES), per_b4),
            pl.BlockSpec((1, tq, LANES), lambda b, i: (b, i, 0)),
            pl.BlockSpec((n_sel, n_cmp_pad), const2),
            pl.BlockSpec((n_sel, LANES), const2),
            pl.BlockSpec((4, rows, LANES), lambda b, i: (0, 0, 0)),
            pl.BlockSpec((1, ATTN_WIDTH), const2),
        ],
        out_specs=pl.BlockSpec((1, tq, ATTN_WIDTH), lambda b, i: (b, i, 0)),
        out_shape=jax.ShapeDtypeStruct((B, S, ATTN_WIDTH), _bf16),
        scratch_shapes=[
            pltpu.VMEM((N_KV_HEADS, rows, S), _f32),
            pltpu.VMEM((N_KV_HEADS, rows, S), _bf16),
        ],
        compiler_params=pltpu.CompilerParams(
            dimension_semantics=("parallel", "arbitrary"), vmem_limit_bytes=VMEM_LIMIT),
        name="attn",
    )(qn, qr, kcmp, vcmp, ksl, vsl, kwn, vwn, gates, ovt, place, masks, gout_attn)


def _ffn_kernel(x_ref, oa_ref, om_ref, wo_ref, gffn_ref, w1_ref, w2_ref, out_ref, act_scr):
    x2 = (x_ref[...] + _dot(oa_ref[...], wo_ref[0:ATTN_WIDTH, :])
          + _dot(om_ref[...], wo_ref[ATTN_WIDTH:ATTN_WIDTH + MIX_WIDTH, :]))
    ms = jnp.mean(x2 * x2, axis=-1, keepdims=True)
    h = (x2 * lax.rsqrt(ms + EPS) * gffn_ref[...]).astype(_bf16)
    for c in range(D_FF // FF_CHUNK):
        cols = slice(c * FF_CHUNK, (c + 1) * FF_CHUNK)
        a = jnp.maximum(_dot(h, w1_ref[:, cols]), 0.0)
        act_scr[:, cols] = (a * a).astype(_bf16)
    out_ref[...] = x2 + _dot(act_scr[...], w2_ref[...])


def _ffn_call(x2d, oa, om, wo, gffn, w1, w2):
    T = x2d.shape[0]
    tm = FFN_TILE
    tok = lambda i: (i, 0)
    const = lambda i: (0, 0)
    return pl.pallas_call(
        _ffn_kernel,
        grid=(T // tm,),
        in_specs=[
            pl.BlockSpec((tm, D_MODEL), tok),
            pl.BlockSpec((tm, ATTN_WIDTH), tok),
            pl.BlockSpec((tm, MIX_WIDTH), tok),
            pl.BlockSpec((D_MODEL, D_MODEL), const),
            pl.BlockSpec((1, D_MODEL), const),
            pl.BlockSpec((D_MODEL, D_FF), const),
            pl.BlockSpec((D_FF, D_MODEL), const),
        ],
        out_specs=pl.BlockSpec((tm, D_MODEL), tok),
        out_shape=jax.ShapeDtypeStruct((T, D_MODEL), _f32),
        scratch_shapes=[pltpu.VMEM((tm, D_FF), _bf16)],
        compiler_params=pltpu.CompilerParams(
            dimension_semantics=("parallel",), vmem_limit_bytes=VMEM_LIMIT),
        name="ffn",
    )(x2d, oa, om, wo, gffn, w1, w2)


def _constants(S):
    half = HEAD_DIM // 2
    inv = ROPE_THETA ** (-jnp.arange(half, dtype=_f32) / half)
    ang = jnp.arange(S, dtype=_f32)[:, None] * inv[None, :]
    cos = jnp.cos(ang)
    sin = jnp.sin(ang)
    cos2 = jnp.concatenate([cos, cos, cos, cos], axis=1)
    sin2 = jnp.concatenate([-sin, sin, -sin, sin], axis=1)
    lane = np.arange(LANES)
    bd = (lane[:, None] // HEAD_DIM == lane[None, :] // HEAD_DIM).astype(np.float32) / HEAD_DIM
    n_cmp = (S - CMP_BLOCK) // CMP_STRIDE + 1
    n_sel = S // SEL_BLOCK
    n_cmp_pad = S // CMP_STRIDE
    cmp_start = np.arange(n_cmp)[:, None] * CMP_STRIDE
    sel_start = np.arange(n_sel)[None, :] * SEL_BLOCK
    overlap = np.clip(np.minimum(cmp_start + CMP_BLOCK, sel_start + SEL_BLOCK)
                      - np.maximum(cmp_start, sel_start), 0, None).astype(np.float32) / CMP_BLOCK
    ovt = np.zeros((n_sel, n_cmp_pad), np.float32)
    ovt[:, :n_cmp] = overlap.T
    place = np.zeros((n_sel, LANES), np.float32)
    place[np.arange(n_sel), HEAD_DIM + np.arange(n_sel)] = 1.0
    qq = (np.arange(GQA_GROUP * ATTN_Q_TILE) % ATTN_Q_TILE)[:, None]
    kk = lane[None, :]
    masks = np.zeros((4, GQA_GROUP * ATTN_Q_TILE, LANES), np.float32)
    masks[MASK_DIAG] = np.where(kk <= qq, 0.0, NEG_BIG)
    masks[MASK_LOW] = np.where(kk > qq, 0.0, NEG_BIG)
    masks[MASK_NONE] = NEG_BIG
    return (cos2, sin2, jnp.asarray(bd, _bf16), jnp.asarray(ovt, _bf16), jnp.asarray(place, _bf16),
            jnp.asarray(masks, _f32))


def _pack_w_in(w):
    gate0 = ATTN_WIDTH + 6 * KV_WIDTH
    n_g = N_GATES * N_ATTN_HEADS
    gates = w[:, gate0:gate0 + n_g].reshape(D_MODEL, N_ATTN_HEADS, N_GATES)
    gates = jnp.transpose(gates, (0, 2, 1)).reshape(D_MODEL, n_g)
    pad = jnp.zeros((D_MODEL, _IN_COLS_PACKED - _C_GATE - n_g), w.dtype)
    return jnp.concatenate([w[:, :gate0], w[:, gate0 + n_g:], gates, pad], axis=1).astype(_bf16)


def _expand_cmp_w1(w1):
    w = w1.reshape(2, CMP_STRIDE, HEAD_DIM, CMP_HIDDEN)
    z = jnp.zeros_like(w)
    h0 = jnp.concatenate([w, z], axis=-1)
    h1 = jnp.concatenate([z, w], axis=-1)
    both = jnp.stack([h0, h1], axis=2)
    return both.reshape(2, CMP_STRIDE * KV_WIDTH, N_KV_HEADS * CMP_HIDDEN).astype(_bf16)


def _expand_pe(pe):
    p = pe.reshape(2, CMP_STRIDE, 1, HEAD_DIM)
    return jnp.broadcast_to(p, (2, CMP_STRIDE, N_KV_HEADS, HEAD_DIM)).reshape(2, CMP_STRIDE * KV_WIDTH)


def _forward(x, g_mix_norm, w_in, g_q, g_k, cmp_pe, cmp_w1, cmp_w2, g_sgu, sp_w, sp_b, g_out, w_out,
             g_ffn_norm, w_ff1, w_ff2):
    B, S, _ = x.shape
    l = 0
    scale = HEAD_DIM ** -0.5
    cos2, sin2, bd, ovt, place, masks = _constants(S)

    w_in_p = _pack_w_in(w_in[l])
    gq = (jnp.tile(g_q[l], N_ATTN_HEADS) * scale)[None, :]
    gksl = jnp.tile(g_k[l, 1], N_KV_HEADS)[None, :]
    gkwn = jnp.tile(g_k[l, 2], N_KV_HEADS)[None, :]
    gsgu = g_sgu[l].reshape(1, MIX_WIDTH)
    spw = sp_w[l].reshape(N_MIX_GROUPS // 2, 2, CHUNK, CHUNK)
    spw = jnp.transpose(spw, (0, 2, 1, 3)).reshape(N_MIX_GROUPS // 2, CHUNK, 2 * CHUNK).astype(_bf16)
    spb = jnp.repeat(sp_b[l].T, HEAD_DIM, axis=1)
    gout = g_out[l]

    qn, qr, kc, vc, ksl, vsl, kwn, vwn, gates, omix = _proj_call(
        x, g_mix_norm[l][None, :], w_in_p, gq, gksl, gkwn, cos2, sin2, bd, gsgu, spw, spb,
        gout[None, ATTN_WIDTH:])

    n_rows = S // CMP_STRIDE
    pe_rows = jnp.stack([_expand_pe(cmp_pe[l, 0]), _expand_pe(cmp_pe[l, 1])])
    w1x = jnp.stack([_expand_cmp_w1(cmp_w1[l, 0]), _expand_cmp_w1(cmp_w1[l, 1])])
    kcmp, vcmp = _compress_call(
        kc.reshape(B, n_rows, CMP_STRIDE * KV_WIDTH), vc.reshape(B, n_rows, CMP_STRIDE * KV_WIDTH),
        pe_rows, w1x, cmp_w2[l].astype(_bf16), g_k[l, 0][None, :])

    oattn = _attn_call(qn, qr, kcmp, vcmp, ksl, vsl, kwn, vwn, gates, ovt, place, masks,
                       gout[None, :ATTN_WIDTH])

    out = _ffn_call(x.reshape(B * S, D_MODEL), oattn.reshape(B * S, ATTN_WIDTH),
                    omix.reshape(B * S, MIX_WIDTH), w_out[l].astype(_bf16), g_ffn_norm[l][None, :],
                    w_ff1[l].astype(_bf16), w_ff2[l].astype(_bf16))
    return out.reshape(B, S, D_MODEL), (qn, qr, kc, vc, ksl, vsl, kwn, vwn, gates, omix, kcmp, vcmp, oattn)


def kernel(x, g_mix_norm, w_in, g_q, g_k, cmp_pe, cmp_w1, cmp_w2, g_sgu, sp_w, sp_b, g_out, w_out,
           g_ffn_norm, w_ff1, w_ff2):
    return _forward(x, g_mix_norm, w_in, g_q, g_k, cmp_pe, cmp_w1, cmp_w2, g_sgu, sp_w, sp_b, g_out, w_out,
                    g_ffn_norm, w_ff1, w_ff2)[0]
```

```python
import functools

import numpy as np
import jax
import jax.numpy as jnp
from jax import lax
from jax.experimental import pallas as pl
from jax.experimental.pallas import tpu as pltpu

D_MODEL = 1024
HEAD_DIM = 64
N_ATTN_HEADS = 8
N_MIX_GROUPS = 8
GQA_GROUP = 4
N_KV_HEADS = 2
ATTN_WIDTH = 512
MIX_WIDTH = 512
KV_WIDTH = 128
N_GATES = 3
CMP_BLOCK = 32
CMP_STRIDE = 16
CMP_HIDDEN = 256
SEL_BLOCK = 64
SEL_TOPN = 16
WINDOW = 512
CHUNK = 128
D_FF = 4 * D_MODEL
ROPE_THETA = 10000.0
EPS = 1e-6
FORCE_SCORE = 1e9

LANES = 128
NEG_BIG = -1e30
PROJ_TILE = 256
ATTN_Q_TILE = 128
SEL_CHUNK = 512
MASK_FULL, MASK_DIAG, MASK_LOW, MASK_NONE = 0, 1, 2, 3
FFN_TILE = 256
FF_CHUNK = 1024
VMEM_LIMIT = 56 * 1024 * 1024

_C_Q = 0
_C_KC = 512
_C_VC = 640
_C_KSL = 768
_C_VSL = 896
_C_KWN = 1024
_C_VWN = 1152
_C_ZU = 1280
_C_ZV = 1792
_C_GATE = 2304
_IN_COLS_PACKED = 2432

_bf16 = jnp.bfloat16
_f32 = jnp.float32


def _dot(a, b):
    return jnp.dot(a, b, preferred_element_type=_f32)


def _dot_nt(a, b):
    return lax.dot_general(a, b, (((1,), (1,)), ((), ())), preferred_element_type=_f32)


def _split_dot(v, m):
    hi = v.astype(_bf16)
    lo = (v - hi.astype(_f32)).astype(_bf16)
    return _dot(hi, m) + _dot(lo, m)


def _group_mean(sq, bd):
    parts = [_split_dot(sq[:, c * LANES:(c + 1) * LANES], bd) for c in range(sq.shape[1] // LANES)]
    return parts[0] if len(parts) == 1 else jnp.concatenate(parts, axis=1)


def _swap_halves(v):
    n = v.shape[1]
    lane = lax.broadcasted_iota(jnp.int32, v.shape, 1)
    return jnp.where((lane % HEAD_DIM) < HEAD_DIM // 2, pltpu.roll(v, n - HEAD_DIM // 2, 1),
                     pltpu.roll(v, HEAD_DIM // 2, 1))


def _proj_kernel(x_ref, gmix_ref, w_ref, gq_ref, gksl_ref, gkwn_ref, cos_ref, sin_ref, bd_ref,
                 gsgu_ref, spw_ref, spb_ref, gomix_ref,
                 qn_ref, qr_ref, kc_ref, vc_ref, ksl_ref, vsl_ref, kwn_ref, vwn_ref, gate_ref, omix_ref):
    tm = x_ref.shape[1]
    i = pl.program_id(1)
    x = x_ref[0]
    ms = jnp.mean(x * x, axis=-1, keepdims=True)
    h = (x * lax.rsqrt(ms + EPS) * gmix_ref[...]).astype(_bf16)
    bd = bd_ref[...]
    cos = cos_ref[...]
    sin = sin_ref[...]

    zq = _dot(h, w_ref[:, _C_Q:_C_Q + ATTN_WIDTH])
    qn = zq * lax.rsqrt(_group_mean(zq * zq, bd) + EPS) * gq_ref[...]
    cos4 = jnp.concatenate([cos] * 4, axis=1)
    sin4 = jnp.concatenate([sin] * 4, axis=1)
    qr = qn * cos4 + _swap_halves(qn) * sin4
    zero_half = jnp.zeros((tm, HEAD_DIM), _bf16)
    for hq in range(N_ATTN_HEADS):
        sl = slice(hq * HEAD_DIM, (hq + 1) * HEAD_DIM)
        qn_ref[0, hq] = jnp.concatenate([qn[:, sl].astype(_bf16), zero_half], axis=1)
        qr_ref[0, hq] = jnp.concatenate([qr[:, sl].astype(_bf16), zero_half], axis=1)

    kc_ref[0] = _dot(h, w_ref[:, _C_KC:_C_KC + KV_WIDTH]).astype(_bf16)
    vc_ref[0] = _dot(h, w_ref[:, _C_VC:_C_VC + KV_WIDTH]).astype(_bf16)

    ones_block = jnp.where(lax.broadcasted_iota(jnp.int32, (HEAD_DIM, LANES), 0) == 0, 1.0, 0.0)
    for col, out_ref in ((_C_VSL, vsl_ref), (_C_VWN, vwn_ref)):
        zv2 = _dot(h, w_ref[:, col:col + KV_WIDTH])
        for tt in range(tm // LANES):
            zt = zv2[tt * LANES:(tt + 1) * LANES, :].T
            for hh in range(N_KV_HEADS):
                out_ref[0, hh, tt] = jnp.concatenate(
                    [zt[hh * HEAD_DIM:(hh + 1) * HEAD_DIM], ones_block], axis=0).astype(_bf16)

    lane = lax.broadcasted_iota(jnp.int32, (tm, LANES), 1)
    pos = i * tm + lax.broadcasted_iota(jnp.int32, (tm, LANES), 0)
    blk_onehot = jnp.where(lane - HEAD_DIM == pos // SEL_BLOCK, 1.0, 0.0)
    for col, g_ref, out_ref, extra in ((_C_KSL, gksl_ref, ksl_ref, blk_onehot),
                                       (_C_KWN, gkwn_ref, kwn_ref, jnp.zeros((tm, LANES), _f32))):
        zk = _dot(h, w_ref[:, col:col + KV_WIDTH])
        kn = zk * lax.rsqrt(_group_mean(zk * zk, bd) + EPS) * g_ref[...]
        kr = kn * cos + _swap_halves(kn) * sin
        out_ref[0, 0] = jnp.where(lane < HEAD_DIM, kr, extra).astype(_bf16)
        out_ref[0, 1] = jnp.where(lane < HEAD_DIM, pltpu.roll(kr, HEAD_DIM, 1), extra).astype(_bf16)

    gate_ref[0] = jax.nn.sigmoid(_dot(h, w_ref[:, _C_GATE:_C_GATE + LANES]))

    row = lax.broadcasted_iota(jnp.int32, (CHUNK, 2 * CHUNK), 0)
    colw = lax.broadcasted_iota(jnp.int32, (CHUNK, 2 * CHUNK), 1) % CHUNK
    causal_w = colw <= row
    lane_c = lax.broadcasted_iota(jnp.int32, (CHUNK, LANES), 1)
    for c in range(tm // CHUNK):
        rows = slice(c * CHUNK, (c + 1) * CHUNK)
        hc = h[rows]
        zu = jax.nn.gelu(_dot(hc, w_ref[:, _C_ZU:_C_ZU + MIX_WIDTH]))
        zv = jax.nn.gelu(_dot(hc, w_ref[:, _C_ZV:_C_ZV + MIX_WIDTH]))
        vn = zv * lax.rsqrt(_group_mean(zv * zv, bd) + EPS) * gsgu_ref[...]
        sv_parts = []
        for p in range(N_MIX_GROUPS // 2):
            vp = vn[:, p * LANES:(p + 1) * LANES]
            rhs = jnp.concatenate([jnp.where(lane_c < HEAD_DIM, vp, 0.0),
                                   jnp.where(lane_c < HEAD_DIM, 0.0, vp)], axis=0).astype(_bf16)
            w_pair = jnp.where(causal_w, spw_ref[p], jnp.zeros((), _bf16))
            sv_parts.append(_dot(w_pair, rhs))
        sv = jnp.concatenate(sv_parts, axis=1) + spb_ref[...]
        om = zu * sv
        oms = jnp.mean(om * om, axis=-1, keepdims=True)
        omix_ref[0, rows] = (om * lax.rsqrt(oms + EPS) * gomix_ref[...]).astype(_bf16)


def _proj_call(x, gmix, w_in_p, gq, gksl, gkwn, cos, sin, bd, gsgu, spw, spb, gomix):
    B, S, _ = x.shape
    tm = PROJ_TILE
    nt = S // tm
    const2 = lambda b, i: (0, 0)
    const3 = lambda b, i: (0, 0, 0)
    tok3 = lambda b, i: (b, i, 0)
    head4 = lambda b, i: (b, 0, i, 0)
    out_shape = (
        jax.ShapeDtypeStruct((B, N_ATTN_HEADS, S, LANES), _bf16),
        jax.ShapeDtypeStruct((B, N_ATTN_HEADS, S, LANES), _bf16),
        jax.ShapeDtypeStruct((B, S, KV_WIDTH), _bf16),
        jax.ShapeDtypeStruct((B, S, KV_WIDTH), _bf16),
        jax.ShapeDtypeStruct((B, N_KV_HEADS, S, LANES), _bf16),
        jax.ShapeDtypeStruct((B, N_KV_HEADS, S // LANES, LANES, LANES), _bf16),
        jax.ShapeDtypeStruct((B, N_KV_HEADS, S, LANES), _bf16),
        jax.ShapeDtypeStruct((B, N_KV_HEADS, S // LANES, LANES, LANES), _bf16),
        jax.ShapeDtypeStruct((B, S, LANES), _f32),
        jax.ShapeDtypeStruct((B, S, MIX_WIDTH), _bf16),
    )
    q_spec = pl.BlockSpec((1, N_ATTN_HEADS, tm, LANES), head4)
    k_spec = pl.BlockSpec((1, N_KV_HEADS, tm, LANES), head4)
    t_spec = pl.BlockSpec((1, tm, LANES), tok3)
    vt_spec = pl.BlockSpec((1, N_KV_HEADS, tm // LANES, LANES, LANES), lambda b, i: (b, 0, i, 0, 0))
    return pl.pallas_call(
        _proj_kernel,
        grid=(B, nt),
        in_specs=[
            pl.BlockSpec((1, tm, D_MODEL), tok3),
            pl.BlockSpec((1, D_MODEL), const2),
            pl.BlockSpec((D_MODEL, _IN_COLS_PACKED), const2),
            pl.BlockSpec((1, ATTN_WIDTH), const2),
            pl.BlockSpec((1, KV_WIDTH), const2),
            pl.BlockSpec((1, KV_WIDTH), const2),
            pl.BlockSpec((tm, LANES), lambda b, i: (i, 0)),
            pl.BlockSpec((tm, LANES), lambda b, i: (i, 0)),
            pl.BlockSpec((LANES, LANES), const2),
            pl.BlockSpec((1, MIX_WIDTH), const2),
            pl.BlockSpec((N_MIX_GROUPS // 2, CHUNK, 2 * CHUNK), const3),
            pl.BlockSpec((CHUNK, MIX_WIDTH), const2),
            pl.BlockSpec((1, MIX_WIDTH), const2),
        ],
        out_specs=(q_spec, q_spec, t_spec, t_spec, k_spec, vt_spec, k_spec, vt_spec, t_spec,
                   pl.BlockSpec((1, tm, MIX_WIDTH), tok3)),
        out_shape=out_shape,
        compiler_params=pltpu.CompilerParams(
            dimension_semantics=("parallel", "parallel"), vmem_limit_bytes=VMEM_LIMIT),
        name="proj",
    )(x, gmix, w_in_p, gq, gksl, gkwn, cos, sin, bd, gsgu, spw, spb, gomix)


def _compress_kernel(kc_ref, vc_ref, pe_ref, w1_ref, w2_ref, gk_ref, kcmp_ref, vcmp_ref):
    n_rows = kc_ref.shape[1]
    zero_half = jnp.zeros((n_rows, HEAD_DIM), _f32)
    for t, (src_ref, out_ref) in enumerate(((kc_ref, kcmp_ref), (vc_ref, vcmp_ref))):
        g = src_ref[0].astype(_f32)
        first = _dot((g + pe_ref[t, 0:1]).astype(_bf16), w1_ref[t, 0])
        second = _dot((g + pe_ref[t, 1:2]).astype(_bf16), w1_ref[t, 1])
        hid = jax.nn.gelu(first + pltpu.roll(second, n_rows - 1, 0)).astype(_bf16)
        heads = [_dot(hid[:, hh * CMP_HIDDEN:(hh + 1) * CMP_HIDDEN], w2_ref[t]) for hh in range(N_KV_HEADS)]
        if t == 0:
            heads = [c * lax.rsqrt(jnp.mean(c * c, axis=-1, keepdims=True) + EPS) * gk_ref[...] for c in heads]
            for hh in range(N_KV_HEADS):
                out_ref[0, hh] = jnp.concatenate([heads[hh], zero_half], axis=1).astype(_bf16)
        else:
            for hh in range(N_KV_HEADS):
                vt = jnp.concatenate([heads[hh], zero_half], axis=1).T
                out_ref[0, hh] = vt.astype(_bf16)


def _compress_call(kc_g, vc_g, pe_rows, w1x, w2, gk0):
    B, n_rows, width = kc_g.shape
    return pl.pallas_call(
        _compress_kernel,
        grid=(B,),
        in_specs=[
            pl.BlockSpec((1, n_rows, width), lambda b: (b, 0, 0)),
            pl.BlockSpec((1, n_rows, width), lambda b: (b, 0, 0)),
            pl.BlockSpec((2, 2, width), lambda b: (0, 0, 0)),
            pl.BlockSpec((2, 2, width, N_KV_HEADS * CMP_HIDDEN), lambda b: (0, 0, 0, 0)),
            pl.BlockSpec((2, CMP_HIDDEN, HEAD_DIM), lambda b: (0, 0, 0)),
            pl.BlockSpec((1, HEAD_DIM), lambda b: (0, 0)),
        ],
        out_specs=(pl.BlockSpec((1, N_KV_HEADS, n_rows, LANES), lambda b: (b, 0, 0, 0)),
                   pl.BlockSpec((1, N_KV_HEADS, n_rows, LANES), lambda b: (b, 0, 0, 0))),
        out_shape=(jax.ShapeDtypeStruct((B, N_KV_HEADS, n_rows, LANES), _bf16),
                   jax.ShapeDtypeStruct((B, N_KV_HEADS, n_rows, LANES), _bf16)),
        compiler_params=pltpu.CompilerParams(
            dimension_semantics=("parallel",), vmem_limit_bytes=VMEM_LIMIT),
        name="compress",
    )(kc_g, vc_g, pe_rows, w1x, w2, gk0)


def _attn_kernel(qn_ref, qr_ref, kcmp_ref, vcmpt_ref, ksl_ref, vslt_ref, kwn_ref, vwnt_ref, gate_ref,
                 ovt_ref, place_ref, mask_ref, gout_ref, o_ref):
    tq = o_ref.shape[1]
    rows = GQA_GROUP * tq
    S = ksl_ref.shape[2]
    n_cmp_pad = kcmp_ref.shape[2]
    n_sel = S // SEL_BLOCK
    tiles_per_chunk = SEL_CHUNK // tq
    win_tiles = WINDOW // tq
    i = pl.program_id(1)
    q0 = i * tq

    qpos = q0 + lax.broadcasted_iota(jnp.int32, (1, rows), 1) % tq
    blk = lax.broadcasted_iota(jnp.int32, (n_sel, tq), 0)
    tpos = q0 + lax.broadcasted_iota(jnp.int32, (n_sel, tq), 1)
    gates_t = gate_ref[0].T

    def stacked_heads(ref, h):
        return ref[0, GQA_GROUP * h:GQA_GROUP * (h + 1)].reshape(rows, LANES)

    def tile_mask(key_tile, low_edge=False):
        kind = jnp.where(key_tile < i, MASK_FULL, jnp.where(key_tile == i, MASK_DIAG, MASK_NONE))
        if low_edge:
            kind = jnp.where(i >= win_tiles, MASK_LOW, kind)
        return mask_ref[kind]

    def window_branch(h):
        first_tile = jnp.maximum(i - win_tiles, 0)
        w0 = pl.multiple_of(first_tile * tq, tq)
        sw = _dot_nt(kwn_ref[0, h, pl.ds(w0, WINDOW + tq), :], stacked_heads(qr_ref, h))
        sw = jnp.concatenate(
            [sw[t * tq:(t + 1) * tq] + tile_mask(first_tile + t, low_edge=(t == 0))
             for t in range(win_tiles + 1)], axis=0)
        mw = jnp.max(sw, axis=0, keepdims=True)
        pw = jnp.exp(sw - mw).astype(_bf16)
        vt = jnp.concatenate([vwnt_ref[0, h, first_tile + t] for t in range(win_tiles + 1)], axis=1)
        acc = _dot(vt, pw)
        return acc[0:HEAD_DIM] / acc[HEAD_DIM:HEAD_DIM + 1]

    def compressed_and_select(h):
        qn = stacked_heads(qn_ref, h)
        qr = stacked_heads(qr_ref, h)

        s = _dot_nt(kcmp_ref[0, h], qn)
        n_idx = lax.broadcasted_iota(jnp.int32, (n_cmp_pad, 1), 0)
        valid_c = n_idx * CMP_STRIDE + (CMP_BLOCK - 1) <= qpos
        s = jnp.where(valid_c, s, NEG_BIG)
        m = jnp.max(s, axis=0, keepdims=True)
        e = jnp.where(valid_c, jnp.exp(s - m), 0.0)
        p_c = e / jnp.maximum(jnp.sum(e, axis=0, keepdims=True), 1e-20)
        o_cmp = _dot(vcmpt_ref[0, h], p_c.astype(_bf16))[0:HEAD_DIM]

        p_sum = p_c[:, 0:tq] + p_c[:, tq:2 * tq] + p_c[:, 2 * tq:3 * tq] + p_c[:, 3 * tq:4 * tq]
        p_hi = p_sum.astype(_bf16)
        p_lo = (p_sum - p_hi.astype(_f32)).astype(_bf16)
        imp = _dot(ovt_ref[...], p_hi) + _dot(ovt_ref[...], p_lo)
        cur = tpos // SEL_BLOCK
        forced = (blk == 0) | (blk == cur) | (blk == cur - 1)
        eligible = blk * SEL_BLOCK <= tpos
        score = jnp.where(forced, FORCE_SCORE, jnp.where(eligible, imp, -jnp.inf))
        rank = jnp.zeros((n_sel, tq), _f32)
        for jp in range(n_sel):
            other = score[jp:jp + 1, :]
            tie = jnp.where(blk > jp, 1.0, 0.0)
            rank = rank + jnp.where(other > score, 1.0, jnp.where(other == score, tie, 0.0))
        chosen = (rank < float(SEL_TOPN)) & eligible
        sel_bias = jnp.where(chosen, 0.0, NEG_BIG).astype(_bf16)
        bias_row = lax.dot_general(sel_bias, place_ref[...], (((0,), (0,)), ((), ())),
                                   preferred_element_type=_f32).astype(_bf16)
        return o_cmp, qr + jnp.concatenate([bias_row] * GQA_GROUP, axis=0)

    def step_body(c):
        span = SEL_CHUNK * (c + 1)
        o_wins = [window_branch(h) for h in range(N_KV_HEADS)]
        o_cmps, q_sels = zip(*[compressed_and_select(h) for h in range(N_KV_HEADS)])

        scores, m_sels = [], []
        for h in range(N_KV_HEADS):
            chunks, mx = [], None
            for j in range(c + 1):
                keys = slice(j * SEL_CHUNK, (j + 1) * SEL_CHUNK)
                sc = _dot_nt(ksl_ref[0, h, keys, :], q_sels[h])
                if j == c:
                    sc = jnp.concatenate(
                        [sc[t * tq:(t + 1) * tq] + tile_mask(tiles_per_chunk * c + t)
                         for t in range(tiles_per_chunk)], axis=0)
                chunks.append(sc)
                cm = jnp.max(sc, axis=0, keepdims=True)
                mx = cm if mx is None else jnp.maximum(mx, cm)
            scores.append(chunks)
            m_sels.append(mx)
        o_sels = []
        for h in range(N_KV_HEADS):
            p = jnp.concatenate([jnp.exp(sc - m_sels[h]).astype(_bf16) for sc in scores[h]], axis=0)
            vt = jnp.concatenate([vslt_ref[0, h, t] for t in range(span // tq)], axis=1)
            acc = _dot(vt, p)
            o_sels.append(acc[0:HEAD_DIM] / acc[HEAD_DIM:HEAD_DIM + 1])

        slabs = []
        for h in range(N_KV_HEADS):
            for g in range(GQA_GROUP):
                hq = GQA_GROUP * h + g
                cols = slice(g * tq, (g + 1) * tq)
                slabs.append(gates_t[hq:hq + 1] * o_cmps[h][:, cols]
                             + gates_t[N_ATTN_HEADS + hq:N_ATTN_HEADS + hq + 1] * o_sels[h][:, cols]
                             + gates_t[2 * N_ATTN_HEADS + hq:2 * N_ATTN_HEADS + hq + 1] * o_wins[h][:, cols])
        ot = jnp.concatenate(slabs, axis=0)
        ms = jnp.mean(ot * ot, axis=0, keepdims=True)
        o_ref[0] = (ot * lax.rsqrt(ms + EPS) * gout_ref[...]).T.astype(_bf16)

    cls = q0 // SEL_CHUNK
    for c in range(S // SEL_CHUNK):
        pl.when(cls == c)(functools.partial(step_body, c))


def _attn_call(qn, qr, kcmp, vcmp_t, ksl, vsl_t, kwn, vwn_t, gates, ovt, place, masks, gout_attn):
    B, _, S, _ = qn.shape
    tq = ATTN_Q_TILE
    assert tq == LANES and S % SEL_CHUNK == 0 and SEL_CHUNK % tq == 0 and WINDOW % tq == 0
    n_cmp_pad = kcmp.shape[2]
    n_sel = S // SEL_BLOCK
    rows = GQA_GROUP * tq
    per_b4 = lambda b, i: (b, 0, 0, 0)
    const2 = lambda b, i: (0, 0)
    vt_spec = pl.BlockSpec((1, N_KV_HEADS, S // LANES, LANES, LANES), lambda b, i: (b, 0, 0, 0, 0))
    return pl.pallas_call(
        _attn_kernel,
        grid=(B, S // tq),
        in_specs=[
            pl.BlockSpec((1, N_ATTN_HEADS, tq, LANES), lambda b, i: (b, 0, i, 0)),
            pl.BlockSpec((1, N_ATTN_HEADS, tq, LANES), lambda b, i: (b, 0, i, 0)),
            pl.BlockSpec((1, N_KV_HEADS, n_cmp_pad, LANES), per_b4),
            pl.BlockSpec((1, N_KV_HEADS, n_cmp_pad, LANES), per_b4),
            pl.BlockSpec((1, N_KV_HEADS, S, LANES), per_b4),
            vt_spec,
            pl.BlockSpec((1, N_KV_HEADS, S, LANES), per_b4),
            vt_spec,
            pl.BlockSpec((1, tq, LANES), lambda b, i: (b, i, 0)),
            pl.BlockSpec((n_sel, n_cmp_pad), const2),
            pl.BlockSpec((n_sel, LANES), const2),
            pl.BlockSpec((4, tq, rows), lambda b, i: (0, 0, 0)),
            pl.BlockSpec((ATTN_WIDTH, tq), const2),
        ],
        out_specs=pl.BlockSpec((1, tq, ATTN_WIDTH), lambda b, i: (b, i, 0)),
        out_shape=jax.ShapeDtypeStruct((B, S, ATTN_WIDTH), _bf16),
        compiler_params=pltpu.CompilerParams(
            dimension_semantics=("parallel", "arbitrary"), vmem_limit_bytes=VMEM_LIMIT),
        name="attn",
    )(qn, qr, kcmp, vcmp_t, ksl, vsl_t, kwn, vwn_t, gates, ovt, place, masks, gout_attn)


def _ffn_kernel(x_ref, oa_ref, om_ref, wo_ref, gffn_ref, w1_ref, w2_ref, out_ref, act_scr):
    x2 = (x_ref[...] + _dot(oa_ref[...], wo_ref[0:ATTN_WIDTH, :])
          + _dot(om_ref[...], wo_ref[ATTN_WIDTH:ATTN_WIDTH + MIX_WIDTH, :]))
    ms = jnp.mean(x2 * x2, axis=-1, keepdims=True)
    h = (x2 * lax.rsqrt(ms + EPS) * gffn_ref[...]).astype(_bf16)
    for c in range(D_FF // FF_CHUNK):
        cols = slice(c * FF_CHUNK, (c + 1) * FF_CHUNK)
        a = jnp.maximum(_dot(h, w1_ref[:, cols]), 0.0)
        act_scr[:, cols] = (a * a).astype(_bf16)
    out_ref[...] = x2 + _dot(act_scr[...], w2_ref[...])


def _ffn_call(x2d, oa, om, wo, gffn, w1, w2):
    T = x2d.shape[0]
    tm = FFN_TILE
    tok = lambda i: (i, 0)
    const = lambda i: (0, 0)
    return pl.pallas_call(
        _ffn_kernel,
        grid=(T // tm,),
        in_specs=[
            pl.BlockSpec((tm, D_MODEL), tok),
            pl.BlockSpec((tm, ATTN_WIDTH), tok),
            pl.BlockSpec((tm, MIX_WIDTH), tok),
            pl.BlockSpec((D_MODEL, D_MODEL), const),
            pl.BlockSpec((1, D_MODEL), const),
            pl.BlockSpec((D_MODEL, D_FF), const),
            pl.BlockSpec((D_FF, D_MODEL), const),
        ],
        out_specs=pl.BlockSpec((tm, D_MODEL), tok),
        out_shape=jax.ShapeDtypeStruct((T, D_MODEL), _f32),
        scratch_shapes=[pltpu.VMEM((tm, D_FF), _bf16)],
        compiler_params=pltpu.CompilerParams(
            dimension_semantics=("parallel",), vmem_limit_bytes=VMEM_LIMIT),
        name="ffn",
    )(x2d, oa, om, wo, gffn, w1, w2)


def _constants(S):
    half = HEAD_DIM // 2
    inv = ROPE_THETA ** (-jnp.arange(half, dtype=_f32) / half)
    ang = jnp.arange(S, dtype=_f32)[:, None] * inv[None, :]
    cos = jnp.cos(ang)
    sin = jnp.sin(ang)
    cos2 = jnp.concatenate([cos, cos, cos, cos], axis=1)
    sin2 = jnp.concatenate([-sin, sin, -sin, sin], axis=1)
    lane = np.arange(LANES)
    bd = (lane[:, None] // HEAD_DIM == lane[None, :] // HEAD_DIM).astype(np.float32) / HEAD_DIM
    n_cmp = (S - CMP_BLOCK) // CMP_STRIDE + 1
    n_sel = S // SEL_BLOCK
    n_cmp_pad = S // CMP_STRIDE
    cmp_start = np.arange(n_cmp)[:, None] * CMP_STRIDE
    sel_start = np.arange(n_sel)[None, :] * SEL_BLOCK
    overlap = np.clip(np.minimum(cmp_start + CMP_BLOCK, sel_start + SEL_BLOCK)
                      - np.maximum(cmp_start, sel_start), 0, None).astype(np.float32) / CMP_BLOCK
    ovt = np.zeros((n_sel, n_cmp_pad), np.float32)
    ovt[:, :n_cmp] = overlap.T
    place = np.zeros((n_sel, LANES), np.float32)
    place[np.arange(n_sel), HEAD_DIM + np.arange(n_sel)] = 1.0
    qq = (np.arange(GQA_GROUP * ATTN_Q_TILE) % ATTN_Q_TILE)[None, :]
    kk = np.arange(ATTN_Q_TILE)[:, None]
    masks = np.zeros((4, ATTN_Q_TILE, GQA_GROUP * ATTN_Q_TILE), np.float32)
    masks[MASK_DIAG] = np.where(kk <= qq, 0.0, NEG_BIG)
    masks[MASK_LOW] = np.where(kk > qq, 0.0, NEG_BIG)
    masks[MASK_NONE] = NEG_BIG
    return (cos2, sin2, jnp.asarray(bd, _bf16), jnp.asarray(ovt, _bf16), jnp.asarray(place, _bf16),
            jnp.asarray(masks, _f32))


def _pack_w_in(w):
    gate0 = ATTN_WIDTH + 6 * KV_WIDTH
    n_g = N_GATES * N_ATTN_HEADS
    gates = w[:, gate0:gate0 + n_g].reshape(D_MODEL, N_ATTN_HEADS, N_GATES)
    gates = jnp.transpose(gates, (0, 2, 1)).reshape(D_MODEL, n_g)
    pad = jnp.zeros((D_MODEL, _IN_COLS_PACKED - _C_GATE - n_g), w.dtype)
    return jnp.concatenate([w[:, :gate0], w[:, gate0 + n_g:], gates, pad], axis=1).astype(_bf16)


def _expand_cmp_w1(w1):
    w = w1.reshape(2, CMP_STRIDE, HEAD_DIM, CMP_HIDDEN)
    z = jnp.zeros_like(w)
    h0 = jnp.concatenate([w, z], axis=-1)
    h1 = jnp.concatenate([z, w], axis=-1)
    both = jnp.stack([h0, h1], axis=2)
    return both.reshape(2, CMP_STRIDE * KV_WIDTH, N_KV_HEADS * CMP_HIDDEN).astype(_bf16)


def _expand_pe(pe):
    p = pe.reshape(2, CMP_STRIDE, 1, HEAD_DIM)
    return jnp.broadcast_to(p, (2, CMP_STRIDE, N_KV_HEADS, HEAD_DIM)).reshape(2, CMP_STRIDE * KV_WIDTH)


def _forward(x, g_mix_norm, w_in, g_q, g_k, cmp_pe, cmp_w1, cmp_w2, g_sgu, sp_w, sp_b, g_out, w_out,
             g_ffn_norm, w_ff1, w_ff2):
    B, S, _ = x.shape
    l = 0
    scale = HEAD_DIM ** -0.5
    cos2, sin2, bd, ovt, place, masks = _constants(S)

    w_in_p = _pack_w_in(w_in[l])
    gq = (jnp.tile(g_q[l], N_ATTN_HEADS) * scale)[None, :]
    gksl = jnp.tile(g_k[l, 1], N_KV_HEADS)[None, :]
    gkwn = jnp.tile(g_k[l, 2], N_KV_HEADS)[None, :]
    gsgu = g_sgu[l].reshape(1, MIX_WIDTH)
    spw = sp_w[l].reshape(N_MIX_GROUPS // 2, 2, CHUNK, CHUNK)
    spw = jnp.transpose(spw, (0, 2, 1, 3)).reshape(N_MIX_GROUPS // 2, CHUNK, 2 * CHUNK).astype(_bf16)
    spb = jnp.repeat(sp_b[l].T, HEAD_DIM, axis=1)
    gout = g_out[l]

    qn, qr, kc, vc, ksl, vsl, kwn, vwn, gates, omix = _proj_call(
        x, g_mix_norm[l][None, :], w_in_p, gq, gksl, gkwn, cos2, sin2, bd, gsgu, spw, spb,
        gout[None, ATTN_WIDTH:])

    n_rows = S // CMP_STRIDE
    pe_rows = jnp.stack([_expand_pe(cmp_pe[l, 0]), _expand_pe(cmp_pe[l, 1])])
    w1x = jnp.stack([_expand_cmp_w1(cmp_w1[l, 0]), _expand_cmp_w1(cmp_w1[l, 1])])
    kcmp, vcmp = _compress_call(
        kc.reshape(B, n_rows, CMP_STRIDE * KV_WIDTH), vc.reshape(B, n_rows, CMP_STRIDE * KV_WIDTH),
        pe_rows, w1x, cmp_w2[l].astype(_bf16), g_k[l, 0][None, :])

    oattn = _attn_call(qn, qr, kcmp, vcmp, ksl, vsl, kwn, vwn, gates, ovt, place, masks,
                       jnp.broadcast_to(gout[:ATTN_WIDTH, None], (ATTN_WIDTH, ATTN_Q_TILE)))

    out = _ffn_call(x.reshape(B * S, D_MODEL), oattn.reshape(B * S, ATTN_WIDTH),
                    omix.reshape(B * S, MIX_WIDTH), w_out[l].astype(_bf16), g_ffn_norm[l][None, :],
                    w_ff1[l].astype(_bf16), w_ff2[l].astype(_bf16))
    return out.reshape(B, S, D_MODEL), (qn, qr, kc, vc, ksl, vsl, kwn, vwn, gates, omix, kcmp, vcmp, oattn)


def kernel(x, g_mix_norm, w_in, g_q, g_k, cmp_pe, cmp_w1, cmp_w2, g_sgu, sp_w, sp_b, g_out, w_out,
           g_ffn_norm, w_ff1, w_ff2):
    return _forward(x, g_mix_norm, w_in, g_q, g_k, cmp_pe, cmp_w1, cmp_w2, g_sgu, sp_w, sp_b, g_out, w_out,
                    g_ffn_norm, w_ff1, w_ff2)[0]
```

```python
import functools

import numpy as np
import jax
import jax.numpy as jnp
from jax import lax
from jax.experimental import pallas as pl
from jax.experimental.pallas import tpu as pltpu

D_MODEL = 1024
HEAD_DIM = 64
N_ATTN_HEADS = 8
N_MIX_GROUPS = 8
GQA_GROUP = 4
N_KV_HEADS = 2
ATTN_WIDTH = 512
MIX_WIDTH = 512
KV_WIDTH = 128
N_GATES = 3
CMP_BLOCK = 32
CMP_STRIDE = 16
CMP_HIDDEN = 256
SEL_BLOCK = 64
SEL_TOPN = 16
WINDOW = 512
CHUNK = 128
D_FF = 4 * D_MODEL
ROPE_THETA = 10000.0
EPS = 1e-6
FORCE_SCORE = 1e9

LANES = 128
NEG_BIG = -1e30
PROJ_TILE = 256
ATTN_Q_TILE = 128
SEL_CHUNK = 512
ATTN_UNIT = 256
STREAM_SKEW = 1
MASK_FULL, MASK_DIAG, MASK_LOW, MASK_NONE = 0, 1, 2, 3
FFN_TILE = 256
FF_CHUNK = 1024
VMEM_LIMIT = 56 * 1024 * 1024

_C_Q = 0
_C_KC = 512
_C_VC = 640
_C_KSL = 768
_C_VSL = 896
_C_KWN = 1024
_C_VWN = 1152
_C_ZU = 1280
_C_ZV = 1792
_C_GATE = 2304
_IN_COLS_PACKED = 2432

_bf16 = jnp.bfloat16
_f32 = jnp.float32


def _dot(a, b):
    return jnp.dot(a, b, preferred_element_type=_f32)


def _split_dot(v, m):
    hi = v.astype(_bf16)
    lo = (v - hi.astype(_f32)).astype(_bf16)
    return _dot(hi, m) + _dot(lo, m)


def _group_mean(sq, bd):
    parts = [_split_dot(sq[:, c * LANES:(c + 1) * LANES], bd) for c in range(sq.shape[1] // LANES)]
    return parts[0] if len(parts) == 1 else jnp.concatenate(parts, axis=1)


def _swap_halves(v):
    n = v.shape[1]
    lane = lax.broadcasted_iota(jnp.int32, v.shape, 1)
    return jnp.where((lane % HEAD_DIM) < HEAD_DIM // 2, pltpu.roll(v, n - HEAD_DIM // 2, 1),
                     pltpu.roll(v, HEAD_DIM // 2, 1))


def _proj_kernel(x_ref, gmix_ref, w_ref, gq_ref, gksl_ref, gkwn_ref, cos_ref, sin_ref, bd_ref,
                 gsgu_ref, spw_ref, spb_ref, gomix_ref,
                 qn_ref, qr_ref, kc_ref, vc_ref, ksl_ref, vsl_ref, kwn_ref, vwn_ref, gate_ref, omix_ref):
    tm = x_ref.shape[1]
    i = pl.program_id(1)
    x = x_ref[0]
    ms = jnp.mean(x * x, axis=-1, keepdims=True)
    h = (x * lax.rsqrt(ms + EPS) * gmix_ref[...]).astype(_bf16)
    bd = bd_ref[...]
    cos = cos_ref[...]
    sin = sin_ref[...]

    zq = _dot(h, w_ref[:, _C_Q:_C_Q + ATTN_WIDTH])
    qn = zq * lax.rsqrt(_group_mean(zq * zq, bd) + EPS) * gq_ref[...]
    cos4 = jnp.concatenate([cos] * 4, axis=1)
    sin4 = jnp.concatenate([sin] * 4, axis=1)
    qr = qn * cos4 + _swap_halves(qn) * sin4
    qn_t, qr_t = qn.T, qr.T
    for hq in range(N_ATTN_HEADS):
        sl = slice(hq * HEAD_DIM, (hq + 1) * HEAD_DIM)
        qn_ref[0, hq] = qn_t[sl].astype(_bf16)
        qr_ref[0, hq] = qr_t[sl].astype(_bf16)

    kc_ref[0] = _dot(h, w_ref[:, _C_KC:_C_KC + KV_WIDTH]).astype(_bf16)
    vc_ref[0] = _dot(h, w_ref[:, _C_VC:_C_VC + KV_WIDTH]).astype(_bf16)

    ones_block = jnp.where(lax.broadcasted_iota(jnp.int32, (HEAD_DIM, LANES), 0) == 0, 1.0, 0.0)
    for col, out_ref in ((_C_VSL, vsl_ref), (_C_VWN, vwn_ref)):
        zv2 = _dot(h, w_ref[:, col:col + KV_WIDTH])
        for tt in range(tm // LANES):
            zt = zv2[tt * LANES:(tt + 1) * LANES, :].T
            for hh in range(N_KV_HEADS):
                out_ref[0, hh, tt] = jnp.concatenate(
                    [zt[hh * HEAD_DIM:(hh + 1) * HEAD_DIM], ones_block], axis=0).astype(_bf16)

    lane = lax.broadcasted_iota(jnp.int32, (tm, LANES), 1)
    pos = i * tm + lax.broadcasted_iota(jnp.int32, (tm, LANES), 0)
    blk_onehot = jnp.where(lane - HEAD_DIM == pos // SEL_BLOCK, 1.0, 0.0)
    for col, g_ref, out_ref, extra in ((_C_KSL, gksl_ref, ksl_ref, blk_onehot),
                                       (_C_KWN, gkwn_ref, kwn_ref, jnp.zeros((tm, LANES), _f32))):
        zk = _dot(h, w_ref[:, col:col + KV_WIDTH])
        kn = zk * lax.rsqrt(_group_mean(zk * zk, bd) + EPS) * g_ref[...]
        kr = kn * cos + _swap_halves(kn) * sin
        out_ref[0, 0] = jnp.where(lane < HEAD_DIM, kr, extra).astype(_bf16)
        out_ref[0, 1] = jnp.where(lane < HEAD_DIM, pltpu.roll(kr, HEAD_DIM, 1), extra).astype(_bf16)

    gate_ref[0] = jax.nn.sigmoid(_dot(h, w_ref[:, _C_GATE:_C_GATE + LANES]))

    row = lax.broadcasted_iota(jnp.int32, (CHUNK, 2 * CHUNK), 0)
    colw = lax.broadcasted_iota(jnp.int32, (CHUNK, 2 * CHUNK), 1) % CHUNK
    causal_w = colw <= row
    lane_c = lax.broadcasted_iota(jnp.int32, (CHUNK, LANES), 1)
    for c in range(tm // CHUNK):
        rows = slice(c * CHUNK, (c + 1) * CHUNK)
        hc = h[rows]
        zu = jax.nn.gelu(_dot(hc, w_ref[:, _C_ZU:_C_ZU + MIX_WIDTH]))
        zv = jax.nn.gelu(_dot(hc, w_ref[:, _C_ZV:_C_ZV + MIX_WIDTH]))
        vn = zv * lax.rsqrt(_group_mean(zv * zv, bd) + EPS) * gsgu_ref[...]
        sv_parts = []
        for p in range(N_MIX_GROUPS // 2):
            vp = vn[:, p * LANES:(p + 1) * LANES]
            rhs = jnp.concatenate([jnp.where(lane_c < HEAD_DIM, vp, 0.0),
                                   jnp.where(lane_c < HEAD_DIM, 0.0, vp)], axis=0).astype(_bf16)
            w_pair = jnp.where(causal_w, spw_ref[p], jnp.zeros((), _bf16))
            sv_parts.append(_dot(w_pair, rhs))
        sv = jnp.concatenate(sv_parts, axis=1) + spb_ref[...]
        om = zu * sv
        oms = jnp.mean(om * om, axis=-1, keepdims=True)
        omix_ref[0, rows] = (om * lax.rsqrt(oms + EPS) * gomix_ref[...]).astype(_bf16)


def _proj_call(x, gmix, w_in_p, gq, gksl, gkwn, cos, sin, bd, gsgu, spw, spb, gomix):
    B, S, _ = x.shape
    tm = PROJ_TILE
    nt = S // tm
    const2 = lambda b, i: (0, 0)
    const3 = lambda b, i: (0, 0, 0)
    tok3 = lambda b, i: (b, i, 0)
    head4 = lambda b, i: (b, 0, i, 0)
    out_shape = (
        jax.ShapeDtypeStruct((B, N_ATTN_HEADS, HEAD_DIM, S), _bf16),
        jax.ShapeDtypeStruct((B, N_ATTN_HEADS, HEAD_DIM, S), _bf16),
        jax.ShapeDtypeStruct((B, S, KV_WIDTH), _bf16),
        jax.ShapeDtypeStruct((B, S, KV_WIDTH), _bf16),
        jax.ShapeDtypeStruct((B, N_KV_HEADS, S, LANES), _bf16),
        jax.ShapeDtypeStruct((B, N_KV_HEADS, S // LANES, LANES, LANES), _bf16),
        jax.ShapeDtypeStruct((B, N_KV_HEADS, S, LANES), _bf16),
        jax.ShapeDtypeStruct((B, N_KV_HEADS, S // LANES, LANES, LANES), _bf16),
        jax.ShapeDtypeStruct((B, S, LANES), _f32),
        jax.ShapeDtypeStruct((B, S, MIX_WIDTH), _bf16),
    )
    q_spec = pl.BlockSpec((1, N_ATTN_HEADS, HEAD_DIM, tm), lambda b, i: (b, 0, 0, i))
    k_spec = pl.BlockSpec((1, N_KV_HEADS, tm, LANES), head4)
    t_spec = pl.BlockSpec((1, tm, LANES), tok3)
    vt_spec = pl.BlockSpec((1, N_KV_HEADS, tm // LANES, LANES, LANES), lambda b, i: (b, 0, i, 0, 0))
    return pl.pallas_call(
        _proj_kernel,
        grid=(B, nt),
        in_specs=[
            pl.BlockSpec((1, tm, D_MODEL), tok3),
            pl.BlockSpec((1, D_MODEL), const2),
            pl.BlockSpec((D_MODEL, _IN_COLS_PACKED), const2),
            pl.BlockSpec((1, ATTN_WIDTH), const2),
            pl.BlockSpec((1, KV_WIDTH), const2),
            pl.BlockSpec((1, KV_WIDTH), const2),
            pl.BlockSpec((tm, LANES), lambda b, i: (i, 0)),
            pl.BlockSpec((tm, LANES), lambda b, i: (i, 0)),
            pl.BlockSpec((LANES, LANES), const2),
            pl.BlockSpec((1, MIX_WIDTH), const2),
            pl.BlockSpec((N_MIX_GROUPS // 2, CHUNK, 2 * CHUNK), const3),
            pl.BlockSpec((CHUNK, MIX_WIDTH), const2),
            pl.BlockSpec((1, MIX_WIDTH), const2),
        ],
        out_specs=(q_spec, q_spec, t_spec, t_spec, k_spec, vt_spec, k_spec, vt_spec, t_spec,
                   pl.BlockSpec((1, tm, MIX_WIDTH), tok3)),
        out_shape=out_shape,
        compiler_params=pltpu.CompilerParams(
            dimension_semantics=("parallel", "parallel"), vmem_limit_bytes=VMEM_LIMIT),
        name="proj",
    )(x, gmix, w_in_p, gq, gksl, gkwn, cos, sin, bd, gsgu, spw, spb, gomix)


def _compress_kernel(kc_ref, vc_ref, pe_ref, w1_ref, w2_ref, gk_ref, kcmp_ref, vcmp_ref):
    n_rows = kc_ref.shape[1]
    zero_half = jnp.zeros((n_rows, HEAD_DIM), _f32)
    for t, (src_ref, out_ref) in enumerate(((kc_ref, kcmp_ref), (vc_ref, vcmp_ref))):
        g = src_ref[0].astype(_f32)
        first = _dot((g + pe_ref[t, 0:1]).astype(_bf16), w1_ref[t, 0])
        second = _dot((g + pe_ref[t, 1:2]).astype(_bf16), w1_ref[t, 1])
        hid = jax.nn.gelu(first + pltpu.roll(second, n_rows - 1, 0)).astype(_bf16)
        heads = [_dot(hid[:, hh * CMP_HIDDEN:(hh + 1) * CMP_HIDDEN], w2_ref[t]) for hh in range(N_KV_HEADS)]
        if t == 0:
            heads = [c * lax.rsqrt(jnp.mean(c * c, axis=-1, keepdims=True) + EPS) * gk_ref[...] for c in heads]
            for hh in range(N_KV_HEADS):
                out_ref[0, hh] = jnp.concatenate([heads[hh], zero_half], axis=1).astype(_bf16)
        else:
            for hh in range(N_KV_HEADS):
                vt = jnp.concatenate([heads[hh], zero_half], axis=1).T
                out_ref[0, hh] = vt.astype(_bf16)


def _compress_call(kc_g, vc_g, pe_rows, w1x, w2, gk0):
    B, n_rows, width = kc_g.shape
    return pl.pallas_call(
        _compress_kernel,
        grid=(B,),
        in_specs=[
            pl.BlockSpec((1, n_rows, width), lambda b: (b, 0, 0)),
            pl.BlockSpec((1, n_rows, width), lambda b: (b, 0, 0)),
            pl.BlockSpec((2, 2, width), lambda b: (0, 0, 0)),
            pl.BlockSpec((2, 2, width, N_KV_HEADS * CMP_HIDDEN), lambda b: (0, 0, 0, 0)),
            pl.BlockSpec((2, CMP_HIDDEN, HEAD_DIM), lambda b: (0, 0, 0)),
            pl.BlockSpec((1, HEAD_DIM), lambda b: (0, 0)),
        ],
        out_specs=(pl.BlockSpec((1, N_KV_HEADS, n_rows, LANES), lambda b: (b, 0, 0, 0)),
                   pl.BlockSpec((1, N_KV_HEADS, n_rows, LANES), lambda b: (b, 0, 0, 0))),
        out_shape=(jax.ShapeDtypeStruct((B, N_KV_HEADS, n_rows, LANES), _bf16),
                   jax.ShapeDtypeStruct((B, N_KV_HEADS, n_rows, LANES), _bf16)),
        compiler_params=pltpu.CompilerParams(
            dimension_semantics=("parallel",), vmem_limit_bytes=VMEM_LIMIT),
        name="compress",
    )(kc_g, vc_g, pe_rows, w1x, w2, gk0)


def _interleave(streams):
    live = list(streams)
    while live:
        for g in list(live):
            try:
                next(g)
            except StopIteration:
                live.remove(g)


def _delayed(stream, units):
    for _ in range(units):
        yield
    yield from stream


def _attn_kernel(qn_ref, qr_ref, kcmp_ref, vcmpt_ref, ksl_ref, vslt_ref, kwn_ref, vwnt_ref, gate_ref,
                 ovt_ref, mask_ref, gout_ref, o_ref):
    tq = o_ref.shape[1]
    rows = GQA_GROUP * tq
    S = ksl_ref.shape[2]
    n_cmp_pad = kcmp_ref.shape[2]
    n_sel = S // SEL_BLOCK
    win_tiles = WINDOW // tq
    unit_tiles = ATTN_UNIT // tq
    i = pl.program_id(1)
    q0 = i * tq

    qpos = q0 + lax.broadcasted_iota(jnp.int32, (1, rows), 1) % tq
    blk = lax.broadcasted_iota(jnp.int32, (n_sel, tq), 0)
    tpos = q0 + lax.broadcasted_iota(jnp.int32, (n_sel, tq), 1)
    gates_t = gate_ref[0].T
    zero_rows = jnp.zeros((HEAD_DIM, rows), _bf16)

    def stacked_heads_t(ref, h):
        return jnp.concatenate([ref[0, GQA_GROUP * h + g] for g in range(GQA_GROUP)], axis=1)

    def tile_mask(key_tile, low_edge=False):
        kind = jnp.where(key_tile < i, MASK_FULL, jnp.where(key_tile == i, MASK_DIAG, MASK_NONE))
        if low_edge:
            kind = jnp.where(i >= win_tiles, MASK_LOW, kind)
        return mask_ref[kind]

    def attend(out, h, k_ref, vt_ref, q_t, first_tile, tile_masks):
        n_tiles = len(tile_masks)
        scores, mx = [], None
        for u0 in range(0, n_tiles, unit_tiles):
            nt = min(unit_tiles, n_tiles - u0)
            if isinstance(first_tile, int):
                keys = slice((first_tile + u0) * tq, (first_tile + u0 + nt) * tq)
            else:
                keys = pl.ds(pl.multiple_of((first_tile + u0) * tq, tq), nt * tq)
            sc = _dot(k_ref[0, h, keys, :], q_t)
            if any(tile_masks[u0 + t] is not None for t in range(nt)):
                sc = jnp.concatenate(
                    [sc[t * tq:(t + 1) * tq] if tile_masks[u0 + t] is None
                     else sc[t * tq:(t + 1) * tq] + tile_masks[u0 + t]() for t in range(nt)], axis=0)
            scores.append(sc)
            cm = jnp.max(sc, axis=0, keepdims=True)
            mx = cm if mx is None else jnp.maximum(mx, cm)
            yield
        probs = []
        for sc in scores:
            probs.append(jnp.exp2(sc - mx).astype(_bf16))
            yield
        vt = jnp.concatenate([vt_ref[0, h, first_tile + t] for t in range(n_tiles)], axis=1)
        acc = _dot(vt, jnp.concatenate(probs, axis=0))
        out[h] = acc[0:HEAD_DIM] / acc[HEAD_DIM:HEAD_DIM + 1]
        yield

    def window_stream(out, h, c):
        first_tile = jnp.maximum(i - win_tiles, 0)
        q_t = jnp.concatenate([stacked_heads_t(qr_ref, h), zero_rows], axis=0)
        if c * SEL_CHUNK >= WINDOW:
            masks = ([lambda: mask_ref[MASK_LOW]] + [None] * (win_tiles - 1) + [lambda: mask_ref[MASK_DIAG]])
        else:
            masks = [functools.partial(tile_mask, first_tile + t, low_edge=(t == 0))
                     for t in range(win_tiles + 1)]
        yield from attend(out, h, kwn_ref, vwnt_ref, q_t, first_tile, masks)

    def compressed_and_selected_stream(o_cmps, o_sels, h, c):
        qn = stacked_heads_t(qn_ref, h)
        qr = stacked_heads_t(qr_ref, h)

        s = _dot(kcmp_ref[0, h], jnp.concatenate([qn, zero_rows], axis=0))
        n_idx = lax.broadcasted_iota(jnp.int32, (n_cmp_pad, 1), 0)
        valid_c = n_idx * CMP_STRIDE + (CMP_BLOCK - 1) <= qpos
        s = jnp.where(valid_c, s, NEG_BIG)
        m = jnp.max(s, axis=0, keepdims=True)
        e = jnp.where(valid_c, jnp.exp2(s - m), 0.0)
        p_c = e / jnp.maximum(jnp.sum(e, axis=0, keepdims=True), 1e-20)
        o_cmps[h] = _dot(vcmpt_ref[0, h], p_c.astype(_bf16))[0:HEAD_DIM]
        yield

        eligible = blk * SEL_BLOCK <= tpos
        if SEL_CHUNK * (c + 1) <= SEL_TOPN * SEL_BLOCK:
            chosen = eligible
        else:
            p_sum = p_c[:, 0:tq] + p_c[:, tq:2 * tq] + p_c[:, 2 * tq:3 * tq] + p_c[:, 3 * tq:4 * tq]
            p_hi = p_sum.astype(_bf16)
            p_lo = (p_sum - p_hi.astype(_f32)).astype(_bf16)
            imp = _dot(ovt_ref[...], p_hi) + _dot(ovt_ref[...], p_lo)
            cur = tpos // SEL_BLOCK
            forced = (blk == 0) | (blk == cur) | (blk == cur - 1)
            score = jnp.where(forced, FORCE_SCORE, jnp.where(eligible, imp, -jnp.inf))
            rank = jnp.zeros((n_sel, tq), _f32)
            for jp in range(n_sel):
                other = score[jp:jp + 1, :]
                tie = jnp.where(blk > jp, 1.0, 0.0)
                rank = rank + jnp.where(other > score, 1.0, jnp.where(other == score, tie, 0.0))
            chosen = (rank < float(SEL_TOPN)) & eligible
        sel_bias = jnp.where(chosen, 0.0, NEG_BIG).astype(_bf16)
        yield

        q_t = jnp.concatenate([qr, jnp.concatenate([sel_bias] * GQA_GROUP, axis=1),
                               jnp.zeros((LANES - HEAD_DIM - n_sel, rows), _bf16)], axis=0)
        n_tiles = (SEL_CHUNK // tq) * (c + 1)
        first_edge = n_tiles - SEL_CHUNK // tq
        masks = [None if t < first_edge else functools.partial(tile_mask, t) for t in range(n_tiles)]
        yield from attend(o_sels, h, ksl_ref, vslt_ref, q_t, 0, masks)

    def step_body(c):
        o_wins, o_cmps, o_sels = {}, {}, {}
        _interleave([
            window_stream(o_wins, 0, c),
            compressed_and_selected_stream(o_cmps, o_sels, 0, c),
            _delayed(window_stream(o_wins, 1, c), STREAM_SKEW),
            _delayed(compressed_and_selected_stream(o_cmps, o_sels, 1, c), STREAM_SKEW),
        ])

        slabs = []
        for h in range(N_KV_HEADS):
            for g in range(GQA_GROUP):
                hq = GQA_GROUP * h + g
                cols = slice(g * tq, (g + 1) * tq)
                slabs.append(gates_t[hq:hq + 1] * o_cmps[h][:, cols]
                             + gates_t[N_ATTN_HEADS + hq:N_ATTN_HEADS + hq + 1] * o_sels[h][:, cols]
                             + gates_t[2 * N_ATTN_HEADS + hq:2 * N_ATTN_HEADS + hq + 1] * o_wins[h][:, cols])
        ot = jnp.concatenate(slabs, axis=0)
        ms = jnp.mean(ot * ot, axis=0, keepdims=True)
        o_ref[0] = (ot * lax.rsqrt(ms + EPS) * gout_ref[...]).T.astype(_bf16)

    cls = q0 // SEL_CHUNK
    for c in range(S // SEL_CHUNK):
        pl.when(cls == c)(functools.partial(step_body, c))


def _attn_call(qn_t, qr_t, kcmp, vcmp_t, ksl, vsl_t, kwn, vwn_t, gates, ovt, masks, gout_attn):
    B, _, _, S = qn_t.shape
    tq = ATTN_Q_TILE
    assert tq == LANES and S % SEL_CHUNK == 0 and SEL_CHUNK % tq == 0 and WINDOW % tq == 0
    assert ATTN_UNIT % tq == 0 and kcmp.shape[2] == LANES
    n_cmp_pad = kcmp.shape[2]
    n_sel = S // SEL_BLOCK
    rows = GQA_GROUP * tq
    per_b4 = lambda b, i: (b, 0, 0, 0)
    const2 = lambda b, i: (0, 0)
    vt_spec = pl.BlockSpec((1, N_KV_HEADS, S // LANES, LANES, LANES), lambda b, i: (b, 0, 0, 0, 0))
    return pl.pallas_call(
        _attn_kernel,
        grid=(B, S // tq),
        in_specs=[
            pl.BlockSpec((1, N_ATTN_HEADS, HEAD_DIM, tq), lambda b, i: (b, 0, 0, i)),
            pl.BlockSpec((1, N_ATTN_HEADS, HEAD_DIM, tq), lambda b, i: (b, 0, 0, i)),
            pl.BlockSpec((1, N_KV_HEADS, n_cmp_pad, LANES), per_b4),
            pl.BlockSpec((1, N_KV_HEADS, n_cmp_pad, LANES), per_b4),
            pl.BlockSpec((1, N_KV_HEADS, S, LANES), per_b4),
            vt_spec,
            pl.BlockSpec((1, N_KV_HEADS, S, LANES), per_b4),
            vt_spec,
            pl.BlockSpec((1, tq, LANES), lambda b, i: (b, i, 0)),
            pl.BlockSpec((n_sel, n_cmp_pad), const2),
            pl.BlockSpec((4, tq, rows), lambda b, i: (0, 0, 0)),
            pl.BlockSpec((ATTN_WIDTH, tq), const2),
        ],
        out_specs=pl.BlockSpec((1, tq, ATTN_WIDTH), lambda b, i: (b, i, 0)),
        out_shape=jax.ShapeDtypeStruct((B, S, ATTN_WIDTH), _bf16),
        compiler_params=pltpu.CompilerParams(
            dimension_semantics=("parallel", "arbitrary"), vmem_limit_bytes=VMEM_LIMIT),
        name="attn",
    )(qn_t, qr_t, kcmp, vcmp_t, ksl, vsl_t, kwn, vwn_t, gates, ovt, masks, gout_attn)


def _ffn_kernel(x_ref, oa_ref, om_ref, wo_ref, gffn_ref, w1_ref, w2_ref, out_ref, act_scr):
    x2 = (x_ref[...] + _dot(oa_ref[...], wo_ref[0:ATTN_WIDTH, :])
          + _dot(om_ref[...], wo_ref[ATTN_WIDTH:ATTN_WIDTH + MIX_WIDTH, :]))
    ms = jnp.mean(x2 * x2, axis=-1, keepdims=True)
    h = (x2 * lax.rsqrt(ms + EPS) * gffn_ref[...]).astype(_bf16)
    for c in range(D_FF // FF_CHUNK):
        cols = slice(c * FF_CHUNK, (c + 1) * FF_CHUNK)
        a = jnp.maximum(_dot(h, w1_ref[:, cols]), 0.0)
        act_scr[:, cols] = (a * a).astype(_bf16)
    out_ref[...] = x2 + _dot(act_scr[...], w2_ref[...])


def _ffn_call(x2d, oa, om, wo, gffn, w1, w2):
    T = x2d.shape[0]
    tm = FFN_TILE
    tok = lambda i: (i, 0)
    const = lambda i: (0, 0)
    return pl.pallas_call(
        _ffn_kernel,
        grid=(T // tm,),
        in_specs=[
            pl.BlockSpec((tm, D_MODEL), tok),
            pl.BlockSpec((tm, ATTN_WIDTH), tok),
            pl.BlockSpec((tm, MIX_WIDTH), tok),
            pl.BlockSpec((D_MODEL, D_MODEL), const),
            pl.BlockSpec((1, D_MODEL), const),
            pl.BlockSpec((D_MODEL, D_FF), const),
            pl.BlockSpec((D_FF, D_MODEL), const),
        ],
        out_specs=pl.BlockSpec((tm, D_MODEL), tok),
        out_shape=jax.ShapeDtypeStruct((T, D_MODEL), _f32),
        scratch_shapes=[pltpu.VMEM((tm, D_FF), _bf16)],
        compiler_params=pltpu.CompilerParams(
            dimension_semantics=("parallel",), vmem_limit_bytes=VMEM_LIMIT),
        name="ffn",
    )(x2d, oa, om, wo, gffn, w1, w2)


def _constants(S):
    half = HEAD_DIM // 2
    inv = ROPE_THETA ** (-jnp.arange(half, dtype=_f32) / half)
    ang = jnp.arange(S, dtype=_f32)[:, None] * inv[None, :]
    cos = jnp.cos(ang)
    sin = jnp.sin(ang)
    cos2 = jnp.concatenate([cos, cos, cos, cos], axis=1)
    sin2 = jnp.concatenate([-sin, sin, -sin, sin], axis=1)
    lane = np.arange(LANES)
    bd = (lane[:, None] // HEAD_DIM == lane[None, :] // HEAD_DIM).astype(np.float32) / HEAD_DIM
    n_cmp = (S - CMP_BLOCK) // CMP_STRIDE + 1
    n_sel = S // SEL_BLOCK
    n_cmp_pad = S // CMP_STRIDE
    cmp_start = np.arange(n_cmp)[:, None] * CMP_STRIDE
    sel_start = np.arange(n_sel)[None, :] * SEL_BLOCK
    overlap = np.clip(np.minimum(cmp_start + CMP_BLOCK, sel_start + SEL_BLOCK)
                      - np.maximum(cmp_start, sel_start), 0, None).astype(np.float32) / CMP_BLOCK
    ovt = np.zeros((n_sel, n_cmp_pad), np.float32)
    ovt[:, :n_cmp] = overlap.T
    qq = (np.arange(GQA_GROUP * ATTN_Q_TILE) % ATTN_Q_TILE)[None, :]
    kk = np.arange(ATTN_Q_TILE)[:, None]
    masks = np.zeros((4, ATTN_Q_TILE, GQA_GROUP * ATTN_Q_TILE), np.float32)
    masks[MASK_DIAG] = np.where(kk <= qq, 0.0, NEG_BIG)
    masks[MASK_LOW] = np.where(kk > qq, 0.0, NEG_BIG)
    masks[MASK_NONE] = NEG_BIG
    return cos2, sin2, jnp.asarray(bd, _bf16), jnp.asarray(ovt, _bf16), jnp.asarray(masks, _f32)


def _pack_w_in(w):
    gate0 = ATTN_WIDTH + 6 * KV_WIDTH
    n_g = N_GATES * N_ATTN_HEADS
    gates = w[:, gate0:gate0 + n_g].reshape(D_MODEL, N_ATTN_HEADS, N_GATES)
    gates = jnp.transpose(gates, (0, 2, 1)).reshape(D_MODEL, n_g)
    pad = jnp.zeros((D_MODEL, _IN_COLS_PACKED - _C_GATE - n_g), w.dtype)
    return jnp.concatenate([w[:, :gate0], w[:, gate0 + n_g:], gates, pad], axis=1).astype(_bf16)


def _expand_cmp_w1(w1):
    w = w1.reshape(2, CMP_STRIDE, HEAD_DIM, CMP_HIDDEN)
    z = jnp.zeros_like(w)
    h0 = jnp.concatenate([w, z], axis=-1)
    h1 = jnp.concatenate([z, w], axis=-1)
    both = jnp.stack([h0, h1], axis=2)
    return both.reshape(2, CMP_STRIDE * KV_WIDTH, N_KV_HEADS * CMP_HIDDEN).astype(_bf16)


def _expand_pe(pe):
    p = pe.reshape(2, CMP_STRIDE, 1, HEAD_DIM)
    return jnp.broadcast_to(p, (2, CMP_STRIDE, N_KV_HEADS, HEAD_DIM)).reshape(2, CMP_STRIDE * KV_WIDTH)


def _forward(x, g_mix_norm, w_in, g_q, g_k, cmp_pe, cmp_w1, cmp_w2, g_sgu, sp_w, sp_b, g_out, w_out,
             g_ffn_norm, w_ff1, w_ff2):
    B, S, _ = x.shape
    l = 0
    scale = float(HEAD_DIM ** -0.5 * np.log2(np.e))
    cos2, sin2, bd, ovt, masks = _constants(S)

    w_in_p = _pack_w_in(w_in[l])
    gq = (jnp.tile(g_q[l], N_ATTN_HEADS) * scale)[None, :]
    gksl = jnp.tile(g_k[l, 1], N_KV_HEADS)[None, :]
    gkwn = jnp.tile(g_k[l, 2], N_KV_HEADS)[None, :]
    gsgu = g_sgu[l].reshape(1, MIX_WIDTH)
    spw = sp_w[l].reshape(N_MIX_GROUPS // 2, 2, CHUNK, CHUNK)
    spw = jnp.transpose(spw, (0, 2, 1, 3)).reshape(N_MIX_GROUPS // 2, CHUNK, 2 * CHUNK).astype(_bf16)
    spb = jnp.repeat(sp_b[l].T, HEAD_DIM, axis=1)
    gout = g_out[l]

    qn, qr, kc, vc, ksl, vsl, kwn, vwn, gates, omix = _proj_call(
        x, g_mix_norm[l][None, :], w_in_p, gq, gksl, gkwn, cos2, sin2, bd, gsgu, spw, spb,
        gout[None, ATTN_WIDTH:])

    n_rows = S // CMP_STRIDE
    pe_rows = jnp.stack([_expand_pe(cmp_pe[l, 0]), _expand_pe(cmp_pe[l, 1])])
    w1x = jnp.stack([_expand_cmp_w1(cmp_w1[l, 0]), _expand_cmp_w1(cmp_w1[l, 1])])
    kcmp, vcmp = _compress_call(
        kc.reshape(B, n_rows, CMP_STRIDE * KV_WIDTH), vc.reshape(B, n_rows, CMP_STRIDE * KV_WIDTH),
        pe_rows, w1x, cmp_w2[l].astype(_bf16), g_k[l, 0][None, :])

    oattn = _attn_call(qn, qr, kcmp, vcmp, ksl, vsl, kwn, vwn, gates, ovt, masks,
                       jnp.broadcast_to(gout[:ATTN_WIDTH, None], (ATTN_WIDTH, ATTN_Q_TILE)))

    out = _ffn_call(x.reshape(B * S, D_MODEL), oattn.reshape(B * S, ATTN_WIDTH),
                    omix.reshape(B * S, MIX_WIDTH), w_out[l].astype(_bf16), g_ffn_norm[l][None, :],
                    w_ff1[l].astype(_bf16), w_ff2[l].astype(_bf16))
    return out.reshape(B, S, D_MODEL), (qn, qr, kc, vc, ksl, vsl, kwn, vwn, gates, omix, kcmp, vcmp, oattn)


def kernel(x, g_mix_norm, w_in, g_q, g_k, cmp_pe, cmp_w1, cmp_w2, g_sgu, sp_w, sp_b, g_out, w_out,
           g_ffn_norm, w_ff1, w_ff2):
    return _forward(x, g_mix_norm, w_in, g_q, g_k, cmp_pe, cmp_w1, cmp_w2, g_sgu, sp_w, sp_b, g_out, w_out,
                    g_ffn_norm, w_ff1, w_ff2)[0]
```

```python
import functools

import numpy as np
import jax
import jax.numpy as jnp
from jax import lax
from jax.experimental import pallas as pl
from jax.experimental.pallas import tpu as pltpu

D_MODEL = 1024
HEAD_DIM = 64
N_ATTN_HEADS = 8
N_MIX_GROUPS = 8
GQA_GROUP = 4
N_KV_HEADS = 2
ATTN_WIDTH = 512
MIX_WIDTH = 512
KV_WIDTH = 128
N_GATES = 3
CMP_BLOCK = 32
CMP_STRIDE = 16
CMP_HIDDEN = 256
SEL_BLOCK = 64
SEL_TOPN = 16
WINDOW = 512
CHUNK = 128
D_FF = 4 * D_MODEL
ROPE_THETA = 10000.0
EPS = 1e-6
FORCE_SCORE = 1e9

LANES = 128
NEG_BIG = -1e30
PROJ_TILE = 256
ATTN_Q_TILE = 128
SEL_CHUNK = 512
ATTN_UNIT = 256
STREAM_SKEW = 1
MASK_FULL, MASK_DIAG, MASK_LOW, MASK_NONE = 0, 1, 2, 3
FFN_TILE = 256
FF_CHUNK = 1024
VMEM_LIMIT = 56 * 1024 * 1024

_C_Q = 0
_C_KV = 512
_KV_KC, _KV_VC, _KV_KSL, _KV_VSL, _KV_KWN, _KV_VWN, _KV_GATE = (128 * n for n in range(7))
_KV_GROUP = 1024
_C_ZU = 1536
_C_ZV = 2048
_IN_COLS_PACKED = 2560
MXU_DIM = 256

_bf16 = jnp.bfloat16
_f32 = jnp.float32


def _dot(a, b):
    return jnp.dot(a, b, preferred_element_type=_f32)


def _split_dot(v, m):
    hi = v.astype(_bf16)
    lo = (v - hi.astype(_f32)).astype(_bf16)
    return _dot(hi, m) + _dot(lo, m)


def _group_mean(sq, bd_ref):
    cw = min(sq.shape[1], MXU_DIM)
    bd = bd_ref[0:cw, 0:cw]
    parts = [_split_dot(sq[:, c * cw:(c + 1) * cw], bd) for c in range(sq.shape[1] // cw)]
    return parts[0] if len(parts) == 1 else jnp.concatenate(parts, axis=1)


def _swap_halves(v):
    n = v.shape[1]
    lane = lax.broadcasted_iota(jnp.int32, v.shape, 1)
    return jnp.where((lane % HEAD_DIM) < HEAD_DIM // 2, pltpu.roll(v, n - HEAD_DIM // 2, 1),
                     pltpu.roll(v, HEAD_DIM // 2, 1))


def _proj_kernel(x_ref, gmix_ref, w_ref, gq_ref, gksl_ref, gkwn_ref, cos_ref, sin_ref, bd_ref,
                 gsgu_ref, spw_ref, spb_ref, gomix_ref,
                 qn_ref, qr_ref, kc_ref, vc_ref, ksl_ref, vsl_ref, kwn_ref, vwn_ref, gate_ref, omix_ref):
    tm = x_ref.shape[1]
    i = pl.program_id(1)
    x = x_ref[0]
    ms = jnp.mean(x * x, axis=-1, keepdims=True)
    h = (x * lax.rsqrt(ms + EPS) * gmix_ref[...]).astype(_bf16)
    cos = cos_ref[...]
    sin = sin_ref[...]

    def query_stream():
        zq = _dot(h, w_ref[:, _C_Q:_C_Q + ATTN_WIDTH])
        yield
        qn = zq * lax.rsqrt(_group_mean(zq * zq, bd_ref) + EPS) * gq_ref[...]
        yield
        cos4 = jnp.concatenate([cos] * 4, axis=1)
        sin4 = jnp.concatenate([sin] * 4, axis=1)
        qr = qn * cos4 + _swap_halves(qn) * sin4
        yield
        qn_t, qr_t = qn.T, qr.T
        for hq in range(N_ATTN_HEADS):
            sl = slice(hq * HEAD_DIM, (hq + 1) * HEAD_DIM)
            qn_ref[0, hq] = qn_t[sl].astype(_bf16)
            qr_ref[0, hq] = qr_t[sl].astype(_bf16)

    def kv_stream():
        zkv = _dot(h, w_ref[:, _C_KV:_C_KV + _KV_GROUP])
        yield

        def segment(off):
            return zkv[:, off:off + KV_WIDTH]

        kc_ref[0] = segment(_KV_KC).astype(_bf16)
        vc_ref[0] = segment(_KV_VC).astype(_bf16)
        gate_ref[0] = jax.nn.sigmoid(segment(_KV_GATE))

        ones_block = jnp.where(lax.broadcasted_iota(jnp.int32, (HEAD_DIM, LANES), 0) == 0, 1.0, 0.0)
        for off, out_ref in ((_KV_VSL, vsl_ref), (_KV_VWN, vwn_ref)):
            zv2 = segment(off)
            for tt in range(tm // LANES):
                zt = zv2[tt * LANES:(tt + 1) * LANES, :].T
                for hh in range(N_KV_HEADS):
                    out_ref[0, hh, tt] = jnp.concatenate(
                        [zt[hh * HEAD_DIM:(hh + 1) * HEAD_DIM], ones_block], axis=0).astype(_bf16)
            yield

        lane = lax.broadcasted_iota(jnp.int32, (tm, LANES), 1)
        pos = i * tm + lax.broadcasted_iota(jnp.int32, (tm, LANES), 0)
        blk_onehot = jnp.where(lane - HEAD_DIM == pos // SEL_BLOCK, 1.0, 0.0)
        for off, g_ref, out_ref, extra in ((_KV_KSL, gksl_ref, ksl_ref, blk_onehot),
                                           (_KV_KWN, gkwn_ref, kwn_ref, jnp.zeros((tm, LANES), _f32))):
            zk = segment(off)
            kn = zk * lax.rsqrt(_group_mean(zk * zk, bd_ref) + EPS) * g_ref[...]
            kr = kn * cos + _swap_halves(kn) * sin
            out_ref[0, 0] = jnp.where(lane < HEAD_DIM, kr, extra).astype(_bf16)
            out_ref[0, 1] = jnp.where(lane < HEAD_DIM, pltpu.roll(kr, HEAD_DIM, 1), extra).astype(_bf16)
            yield

    row = lax.broadcasted_iota(jnp.int32, (CHUNK, 2 * CHUNK), 0)
    colw = lax.broadcasted_iota(jnp.int32, (CHUNK, 2 * CHUNK), 1) % CHUNK
    causal_w = colw <= row
    lane_c = lax.broadcasted_iota(jnp.int32, (CHUNK, LANES), 1)

    def gmlp_stream(c):
        rows = slice(c * CHUNK, (c + 1) * CHUNK)
        hc = h[rows]
        zu = _dot(hc, w_ref[:, _C_ZU:_C_ZU + MIX_WIDTH])
        yield
        zu = jax.nn.gelu(zu)
        zv = _dot(hc, w_ref[:, _C_ZV:_C_ZV + MIX_WIDTH])
        yield
        zv = jax.nn.gelu(zv)
        yield
        vn = zv * lax.rsqrt(_group_mean(zv * zv, bd_ref) + EPS) * gsgu_ref[...]
        yield
        sv_parts = []
        for p in range(N_MIX_GROUPS // 2):
            vp = vn[:, p * LANES:(p + 1) * LANES]
            rhs = jnp.concatenate([jnp.where(lane_c < HEAD_DIM, vp, 0.0),
                                   jnp.where(lane_c < HEAD_DIM, 0.0, vp)], axis=0).astype(_bf16)
            w_pair = jnp.where(causal_w, spw_ref[p], jnp.zeros((), _bf16))
            sv_parts.append(_dot(w_pair, rhs))
        yield
        sv = jnp.concatenate(sv_parts, axis=1) + spb_ref[...]
        om = zu * sv
        oms = jnp.mean(om * om, axis=-1, keepdims=True)
        omix_ref[0, rows] = (om * lax.rsqrt(oms + EPS) * gomix_ref[...]).astype(_bf16)

    _interleave([query_stream(), kv_stream()] + [gmlp_stream(c) for c in range(tm // CHUNK)])


def _proj_call(x, gmix, w_in_p, gq, gksl, gkwn, cos, sin, bd, gsgu, spw, spb, gomix):
    B, S, _ = x.shape
    tm = PROJ_TILE
    nt = S // tm
    const2 = lambda b, i: (0, 0)
    const3 = lambda b, i: (0, 0, 0)
    tok3 = lambda b, i: (b, i, 0)
    head4 = lambda b, i: (b, 0, i, 0)
    out_shape = (
        jax.ShapeDtypeStruct((B, N_ATTN_HEADS, HEAD_DIM, S), _bf16),
        jax.ShapeDtypeStruct((B, N_ATTN_HEADS, HEAD_DIM, S), _bf16),
        jax.ShapeDtypeStruct((B, S, KV_WIDTH), _bf16),
        jax.ShapeDtypeStruct((B, S, KV_WIDTH), _bf16),
        jax.ShapeDtypeStruct((B, N_KV_HEADS, S, LANES), _bf16),
        jax.ShapeDtypeStruct((B, N_KV_HEADS, S // LANES, LANES, LANES), _bf16),
        jax.ShapeDtypeStruct((B, N_KV_HEADS, S, LANES), _bf16),
        jax.ShapeDtypeStruct((B, N_KV_HEADS, S // LANES, LANES, LANES), _bf16),
        jax.ShapeDtypeStruct((B, S, LANES), _f32),
        jax.ShapeDtypeStruct((B, S, MIX_WIDTH), _bf16),
    )
    q_spec = pl.BlockSpec((1, N_ATTN_HEADS, HEAD_DIM, tm), lambda b, i: (b, 0, 0, i))
    k_spec = pl.BlockSpec((1, N_KV_HEADS, tm, LANES), head4)
    t_spec = pl.BlockSpec((1, tm, LANES), tok3)
    vt_spec = pl.BlockSpec((1, N_KV_HEADS, tm // LANES, LANES, LANES), lambda b, i: (b, 0, i, 0, 0))
    return pl.pallas_call(
        _proj_kernel,
        grid=(B, nt),
        in_specs=[
            pl.BlockSpec((1, tm, D_MODEL), tok3),
            pl.BlockSpec((1, D_MODEL), const2),
            pl.BlockSpec((D_MODEL, _IN_COLS_PACKED), const2),
            pl.BlockSpec((1, ATTN_WIDTH), const2),
            pl.BlockSpec((1, KV_WIDTH), const2),
            pl.BlockSpec((1, KV_WIDTH), const2),
            pl.BlockSpec((tm, LANES), lambda b, i: (i, 0)),
            pl.BlockSpec((tm, LANES), lambda b, i: (i, 0)),
            pl.BlockSpec((MXU_DIM, MXU_DIM), const2),
            pl.BlockSpec((1, MIX_WIDTH), const2),
            pl.BlockSpec((N_MIX_GROUPS // 2, CHUNK, 2 * CHUNK), const3),
            pl.BlockSpec((CHUNK, MIX_WIDTH), const2),
            pl.BlockSpec((1, MIX_WIDTH), const2),
        ],
        out_specs=(q_spec, q_spec, t_spec, t_spec, k_spec, vt_spec, k_spec, vt_spec, t_spec,
                   pl.BlockSpec((1, tm, MIX_WIDTH), tok3)),
        out_shape=out_shape,
        compiler_params=pltpu.CompilerParams(
            dimension_semantics=("parallel", "parallel"), vmem_limit_bytes=VMEM_LIMIT),
        name="proj",
    )(x, gmix, w_in_p, gq, gksl, gkwn, cos, sin, bd, gsgu, spw, spb, gomix)


def _compress_kernel(kc_ref, vc_ref, pe_ref, w1_ref, w2_ref, gk_ref, kcmp_ref, vcmp_ref):
    n_rows = kc_ref.shape[1]
    zero_half = jnp.zeros((n_rows, HEAD_DIM), _f32)
    for t, (src_ref, out_ref) in enumerate(((kc_ref, kcmp_ref), (vc_ref, vcmp_ref))):
        g = src_ref[0].astype(_f32)
        first = _dot((g + pe_ref[t, 0:1]).astype(_bf16), w1_ref[t, 0])
        second = _dot((g + pe_ref[t, 1:2]).astype(_bf16), w1_ref[t, 1])
        hid = jax.nn.gelu(first + pltpu.roll(second, n_rows - 1, 0)).astype(_bf16)
        heads = [_dot(hid[:, hh * CMP_HIDDEN:(hh + 1) * CMP_HIDDEN], w2_ref[t]) for hh in range(N_KV_HEADS)]
        if t == 0:
            heads = [c * lax.rsqrt(jnp.mean(c * c, axis=-1, keepdims=True) + EPS) * gk_ref[...] for c in heads]
            for hh in range(N_KV_HEADS):
                out_ref[0, hh] = jnp.concatenate([heads[hh], zero_half], axis=1).astype(_bf16)
        else:
            for hh in range(N_KV_HEADS):
                vt = jnp.concatenate([heads[hh], zero_half], axis=1).T
                out_ref[0, hh] = vt.astype(_bf16)


def _compress_call(kc_g, vc_g, pe_rows, w1x, w2, gk0):
    B, n_rows, width = kc_g.shape
    return pl.pallas_call(
        _compress_kernel,
        grid=(B,),
        in_specs=[
            pl.BlockSpec((1, n_rows, width), lambda b: (b, 0, 0)),
            pl.BlockSpec((1, n_rows, width), lambda b: (b, 0, 0)),
            pl.BlockSpec((2, 2, width), lambda b: (0, 0, 0)),
            pl.BlockSpec((2, 2, width, N_KV_HEADS * CMP_HIDDEN), lambda b: (0, 0, 0, 0)),
            pl.BlockSpec((2, CMP_HIDDEN, HEAD_DIM), lambda b: (0, 0, 0)),
            pl.BlockSpec((1, HEAD_DIM), lambda b: (0, 0)),
        ],
        out_specs=(pl.BlockSpec((1, N_KV_HEADS, n_rows, LANES), lambda b: (b, 0, 0, 0)),
                   pl.BlockSpec((1, N_KV_HEADS, n_rows, LANES), lambda b: (b, 0, 0, 0))),
        out_shape=(jax.ShapeDtypeStruct((B, N_KV_HEADS, n_rows, LANES), _bf16),
                   jax.ShapeDtypeStruct((B, N_KV_HEADS, n_rows, LANES), _bf16)),
        compiler_params=pltpu.CompilerParams(
            dimension_semantics=("parallel",), vmem_limit_bytes=VMEM_LIMIT),
        name="compress",
    )(kc_g, vc_g, pe_rows, w1x, w2, gk0)


def _interleave(streams):
    live = list(streams)
    while live:
        for g in list(live):
            try:
                next(g)
            except StopIteration:
                live.remove(g)


def _delayed(stream, units):
    for _ in range(units):
        yield
    yield from stream


def _attn_kernel(qn_ref, qr_ref, kcmp_ref, vcmpt_ref, ksl_ref, vslt_ref, kwn_ref, vwnt_ref, gate_ref,
                 ovt_ref, mask_ref, gout_ref, o_ref):
    tq = o_ref.shape[1]
    rows = GQA_GROUP * tq
    S = ksl_ref.shape[2]
    n_cmp_pad = kcmp_ref.shape[2]
    n_sel = S // SEL_BLOCK
    win_tiles = WINDOW // tq
    unit_tiles = ATTN_UNIT // tq
    i = pl.program_id(1)
    q0 = i * tq

    qpos = q0 + lax.broadcasted_iota(jnp.int32, (1, rows), 1) % tq
    blk = lax.broadcasted_iota(jnp.int32, (n_sel, tq), 0)
    tpos = q0 + lax.broadcasted_iota(jnp.int32, (n_sel, tq), 1)
    gates_t = gate_ref[0].T
    zero_rows = jnp.zeros((HEAD_DIM, rows), _bf16)

    def stacked_heads_t(ref, h):
        return jnp.concatenate([ref[0, GQA_GROUP * h + g] for g in range(GQA_GROUP)], axis=1)

    def tile_mask(key_tile, low_edge=False):
        kind = jnp.where(key_tile < i, MASK_FULL, jnp.where(key_tile == i, MASK_DIAG, MASK_NONE))
        if low_edge:
            kind = jnp.where(i >= win_tiles, MASK_LOW, kind)
        return mask_ref[kind]

    def attend(out, h, k_ref, vt_ref, q_t, first_tile, tile_masks):
        n_tiles = len(tile_masks)
        scores, mx = [], None
        for u0 in range(0, n_tiles, unit_tiles):
            nt = min(unit_tiles, n_tiles - u0)
            if isinstance(first_tile, int):
                keys = slice((first_tile + u0) * tq, (first_tile + u0 + nt) * tq)
            else:
                keys = pl.ds(pl.multiple_of((first_tile + u0) * tq, tq), nt * tq)
            sc = _dot(k_ref[0, h, keys, :], q_t)
            if any(tile_masks[u0 + t] is not None for t in range(nt)):
                sc = jnp.concatenate(
                    [sc[t * tq:(t + 1) * tq] if tile_masks[u0 + t] is None
                     else sc[t * tq:(t + 1) * tq] + tile_masks[u0 + t]() for t in range(nt)], axis=0)
            scores.append(sc)
            cm = jnp.max(sc, axis=0, keepdims=True)
            mx = cm if mx is None else jnp.maximum(mx, cm)
            yield
        probs = []
        for sc in scores:
            probs.append(jnp.exp2(sc - mx).astype(_bf16))
            yield
        vt = jnp.concatenate([vt_ref[0, h, first_tile + t] for t in range(n_tiles)], axis=1)
        acc = _dot(vt, jnp.concatenate(probs, axis=0))
        out[h] = acc[0:HEAD_DIM] / acc[HEAD_DIM:HEAD_DIM + 1]
        yield

    def window_stream(out, h, c):
        first_tile = jnp.maximum(i - win_tiles, 0)
        q_t = jnp.concatenate([stacked_heads_t(qr_ref, h), zero_rows], axis=0)
        if c * SEL_CHUNK >= WINDOW:
            masks = ([lambda: mask_ref[MASK_LOW]] + [None] * (win_tiles - 1) + [lambda: mask_ref[MASK_DIAG]])
        else:
            masks = [functools.partial(tile_mask, first_tile + t, low_edge=(t == 0))
                     for t in range(win_tiles + 1)]
        yield from attend(out, h, kwn_ref, vwnt_ref, q_t, first_tile, masks)

    def compressed_and_selected_stream(o_cmps, o_sels, h, c):
        qn = stacked_heads_t(qn_ref, h)
        qr = stacked_heads_t(qr_ref, h)

        s = _dot(kcmp_ref[0, h], jnp.concatenate([qn, zero_rows], axis=0))
        n_idx = lax.broadcasted_iota(jnp.int32, (n_cmp_pad, 1), 0)
        valid_c = n_idx * CMP_STRIDE + (CMP_BLOCK - 1) <= qpos
        s = jnp.where(valid_c, s, NEG_BIG)
        m = jnp.max(s, axis=0, keepdims=True)
        e = jnp.where(valid_c, jnp.exp2(s - m), 0.0)
        p_c = e / jnp.maximum(jnp.sum(e, axis=0, keepdims=True), 1e-20)
        o_cmps[h] = _dot(vcmpt_ref[0, h], p_c.astype(_bf16))[0:HEAD_DIM]
        yield

        eligible = blk * SEL_BLOCK <= tpos
        if SEL_CHUNK * (c + 1) <= SEL_TOPN * SEL_BLOCK:
            chosen = eligible
        else:
            p_sum = p_c[:, 0:tq] + p_c[:, tq:2 * tq] + p_c[:, 2 * tq:3 * tq] + p_c[:, 3 * tq:4 * tq]
            p_hi = p_sum.astype(_bf16)
            p_lo = (p_sum - p_hi.astype(_f32)).astype(_bf16)
            imp = _dot(ovt_ref[...], p_hi) + _dot(ovt_ref[...], p_lo)
            cur = tpos // SEL_BLOCK
            forced = (blk == 0) | (blk == cur) | (blk == cur - 1)
            score = jnp.where(forced, FORCE_SCORE, jnp.where(eligible, imp, -jnp.inf))
            rank = jnp.zeros((n_sel, tq), _f32)
            for jp in range(n_sel):
                other = score[jp:jp + 1, :]
                tie = jnp.where(blk > jp, 1.0, 0.0)
                rank = rank + jnp.where(other > score, 1.0, jnp.where(other == score, tie, 0.0))
            chosen = (rank < float(SEL_TOPN)) & eligible
        sel_bias = jnp.where(chosen, 0.0, NEG_BIG).astype(_bf16)
        yield

        q_t = jnp.concatenate([qr, jnp.concatenate([sel_bias] * GQA_GROUP, axis=1),
                               jnp.zeros((LANES - HEAD_DIM - n_sel, rows), _bf16)], axis=0)
        n_tiles = (SEL_CHUNK // tq) * (c + 1)
        first_edge = n_tiles - SEL_CHUNK // tq
        masks = [None if t < first_edge else functools.partial(tile_mask, t) for t in range(n_tiles)]
        yield from attend(o_sels, h, ksl_ref, vslt_ref, q_t, 0, masks)

    def step_body(c):
        o_wins, o_cmps, o_sels = {}, {}, {}
        _interleave([
            window_stream(o_wins, 0, c),
            compressed_and_selected_stream(o_cmps, o_sels, 0, c),
            _delayed(window_stream(o_wins, 1, c), STREAM_SKEW),
            _delayed(compressed_and_selected_stream(o_cmps, o_sels, 1, c), STREAM_SKEW),
        ])

        slabs = []
        for h in range(N_KV_HEADS):
            for g in range(GQA_GROUP):
                hq = GQA_GROUP * h + g
                cols = slice(g * tq, (g + 1) * tq)
                slabs.append(gates_t[hq:hq + 1] * o_cmps[h][:, cols]
                             + gates_t[N_ATTN_HEADS + hq:N_ATTN_HEADS + hq + 1] * o_sels[h][:, cols]
                             + gates_t[2 * N_ATTN_HEADS + hq:2 * N_ATTN_HEADS + hq + 1] * o_wins[h][:, cols])
        ot = jnp.concatenate(slabs, axis=0)
        ms = jnp.mean(ot * ot, axis=0, keepdims=True)
        o_ref[0] = (ot * lax.rsqrt(ms + EPS) * gout_ref[...]).T.astype(_bf16)

    cls = q0 // SEL_CHUNK
    for c in range(S // SEL_CHUNK):
        pl.when(cls == c)(functools.partial(step_body, c))


def _attn_call(qn_t, qr_t, kcmp, vcmp_t, ksl, vsl_t, kwn, vwn_t, gates, ovt, masks, gout_attn):
    B, _, _, S = qn_t.shape
    tq = ATTN_Q_TILE
    assert tq == LANES and S % SEL_CHUNK == 0 and SEL_CHUNK % tq == 0 and WINDOW % tq == 0
    assert ATTN_UNIT % tq == 0 and kcmp.shape[2] == LANES
    n_cmp_pad = kcmp.shape[2]
    n_sel = S // SEL_BLOCK
    rows = GQA_GROUP * tq
    per_b4 = lambda b, i: (b, 0, 0, 0)
    const2 = lambda b, i: (0, 0)
    vt_spec = pl.BlockSpec((1, N_KV_HEADS, S // LANES, LANES, LANES), lambda b, i: (b, 0, 0, 0, 0))
    return pl.pallas_call(
        _attn_kernel,
        grid=(B, S // tq),
        in_specs=[
            pl.BlockSpec((1, N_ATTN_HEADS, HEAD_DIM, tq), lambda b, i: (b, 0, 0, i)),
            pl.BlockSpec((1, N_ATTN_HEADS, HEAD_DIM, tq), lambda b, i: (b, 0, 0, i)),
            pl.BlockSpec((1, N_KV_HEADS, n_cmp_pad, LANES), per_b4),
            pl.BlockSpec((1, N_KV_HEADS, n_cmp_pad, LANES), per_b4),
            pl.BlockSpec((1, N_KV_HEADS, S, LANES), per_b4),
            vt_spec,
            pl.BlockSpec((1, N_KV_HEADS, S, LANES), per_b4),
            vt_spec,
            pl.BlockSpec((1, tq, LANES), lambda b, i: (b, i, 0)),
            pl.BlockSpec((n_sel, n_cmp_pad), const2),
            pl.BlockSpec((4, tq, rows), lambda b, i: (0, 0, 0)),
            pl.BlockSpec((ATTN_WIDTH, tq), const2),
        ],
        out_specs=pl.BlockSpec((1, tq, ATTN_WIDTH), lambda b, i: (b, i, 0)),
        out_shape=jax.ShapeDtypeStruct((B, S, ATTN_WIDTH), _bf16),
        compiler_params=pltpu.CompilerParams(
            dimension_semantics=("parallel", "arbitrary"), vmem_limit_bytes=VMEM_LIMIT),
        name="attn",
    )(qn_t, qr_t, kcmp, vcmp_t, ksl, vsl_t, kwn, vwn_t, gates, ovt, masks, gout_attn)


def _ffn_kernel(x_ref, oa_ref, om_ref, wo_ref, gffn_ref, w1_ref, w2_ref, out_ref, act_scr):
    x2 = (x_ref[...] + _dot(oa_ref[...], wo_ref[0:ATTN_WIDTH, :])
          + _dot(om_ref[...], wo_ref[ATTN_WIDTH:ATTN_WIDTH + MIX_WIDTH, :]))
    ms = jnp.mean(x2 * x2, axis=-1, keepdims=True)
    h = (x2 * lax.rsqrt(ms + EPS) * gffn_ref[...]).astype(_bf16)
    for c in range(D_FF // FF_CHUNK):
        cols = slice(c * FF_CHUNK, (c + 1) * FF_CHUNK)
        a = jnp.maximum(_dot(h, w1_ref[:, cols]), 0.0)
        act_scr[:, cols] = (a * a).astype(_bf16)
    out_ref[...] = x2 + _dot(act_scr[...], w2_ref[...])


def _ffn_call(x2d, oa, om, wo, gffn, w1, w2):
    T = x2d.shape[0]
    tm = FFN_TILE
    tok = lambda i: (i, 0)
    const = lambda i: (0, 0)
    return pl.pallas_call(
        _ffn_kernel,
        grid=(T // tm,),
        in_specs=[
            pl.BlockSpec((tm, D_MODEL), tok),
            pl.BlockSpec((tm, ATTN_WIDTH), tok),
            pl.BlockSpec((tm, MIX_WIDTH), tok),
            pl.BlockSpec((D_MODEL, D_MODEL), const),
            pl.BlockSpec((1, D_MODEL), const),
            pl.BlockSpec((D_MODEL, D_FF), const),
            pl.BlockSpec((D_FF, D_MODEL), const),
        ],
        out_specs=pl.BlockSpec((tm, D_MODEL), tok),
        out_shape=jax.ShapeDtypeStruct((T, D_MODEL), _f32),
        scratch_shapes=[pltpu.VMEM((tm, D_FF), _bf16)],
        compiler_params=pltpu.CompilerParams(
            dimension_semantics=("parallel",), vmem_limit_bytes=VMEM_LIMIT),
        name="ffn",
    )(x2d, oa, om, wo, gffn, w1, w2)


def _constants(S):
    half = HEAD_DIM // 2
    inv = ROPE_THETA ** (-jnp.arange(half, dtype=_f32) / half)
    ang = jnp.arange(S, dtype=_f32)[:, None] * inv[None, :]
    cos = jnp.cos(ang)
    sin = jnp.sin(ang)
    cos2 = jnp.concatenate([cos, cos, cos, cos], axis=1)
    sin2 = jnp.concatenate([-sin, sin, -sin, sin], axis=1)
    mlane = np.arange(MXU_DIM)
    bd = (mlane[:, None] // HEAD_DIM == mlane[None, :] // HEAD_DIM).astype(np.float32) / HEAD_DIM
    n_cmp = (S - CMP_BLOCK) // CMP_STRIDE + 1
    n_sel = S // SEL_BLOCK
    n_cmp_pad = S // CMP_STRIDE
    cmp_start = np.arange(n_cmp)[:, None] * CMP_STRIDE
    sel_start = np.arange(n_sel)[None, :] * SEL_BLOCK
    overlap = np.clip(np.minimum(cmp_start + CMP_BLOCK, sel_start + SEL_BLOCK)
                      - np.maximum(cmp_start, sel_start), 0, None).astype(np.float32) / CMP_BLOCK
    ovt = np.zeros((n_sel, n_cmp_pad), np.float32)
    ovt[:, :n_cmp] = overlap.T
    qq = (np.arange(GQA_GROUP * ATTN_Q_TILE) % ATTN_Q_TILE)[None, :]
    kk = np.arange(ATTN_Q_TILE)[:, None]
    masks = np.zeros((4, ATTN_Q_TILE, GQA_GROUP * ATTN_Q_TILE), np.float32)
    masks[MASK_DIAG] = np.where(kk <= qq, 0.0, NEG_BIG)
    masks[MASK_LOW] = np.where(kk > qq, 0.0, NEG_BIG)
    masks[MASK_NONE] = NEG_BIG
    return cos2, sin2, jnp.asarray(bd, _bf16), jnp.asarray(ovt, _bf16), jnp.asarray(masks, _f32)


def _pack_w_in(w):
    gate0 = ATTN_WIDTH + 6 * KV_WIDTH
    n_g = N_GATES * N_ATTN_HEADS
    gates = w[:, gate0:gate0 + n_g].reshape(D_MODEL, N_ATTN_HEADS, N_GATES)
    gates = jnp.transpose(gates, (0, 2, 1)).reshape(D_MODEL, n_g)
    pad = jnp.zeros((D_MODEL, _C_KV + _KV_GROUP - gate0 - n_g), w.dtype)
    return jnp.concatenate([w[:, :gate0], gates, pad, w[:, gate0 + n_g:]], axis=1).astype(_bf16)


def _expand_cmp_w1(w1):
    w = w1.reshape(2, CMP_STRIDE, HEAD_DIM, CMP_HIDDEN)
    z = jnp.zeros_like(w)
    h0 = jnp.concatenate([w, z], axis=-1)
    h1 = jnp.concatenate([z, w], axis=-1)
    both = jnp.stack([h0, h1], axis=2)
    return both.reshape(2, CMP_STRIDE * KV_WIDTH, N_KV_HEADS * CMP_HIDDEN).astype(_bf16)


def _expand_pe(pe):
    p = pe.reshape(2, CMP_STRIDE, 1, HEAD_DIM)
    return jnp.broadcast_to(p, (2, CMP_STRIDE, N_KV_HEADS, HEAD_DIM)).reshape(2, CMP_STRIDE * KV_WIDTH)


def _forward(x, g_mix_norm, w_in, g_q, g_k, cmp_pe, cmp_w1, cmp_w2, g_sgu, sp_w, sp_b, g_out, w_out,
             g_ffn_norm, w_ff1, w_ff2):
    B, S, _ = x.shape
    l = 0
    scale = float(HEAD_DIM ** -0.5 * np.log2(np.e))
    cos2, sin2, bd, ovt, masks = _constants(S)

    w_in_p = _pack_w_in(w_in[l])
    gq = (jnp.tile(g_q[l], N_ATTN_HEADS) * scale)[None, :]
    gksl = jnp.tile(g_k[l, 1], N_KV_HEADS)[None, :]
    gkwn = jnp.tile(g_k[l, 2], N_KV_HEADS)[None, :]
    gsgu = g_sgu[l].reshape(1, MIX_WIDTH)
    spw = sp_w[l].reshape(N_MIX_GROUPS // 2, 2, CHUNK, CHUNK)
    spw = jnp.transpose(spw, (0, 2, 1, 3)).reshape(N_MIX_GROUPS // 2, CHUNK, 2 * CHUNK).astype(_bf16)
    spb = jnp.repeat(sp_b[l].T, HEAD_DIM, axis=1)
    gout = g_out[l]

    qn, qr, kc, vc, ksl, vsl, kwn, vwn, gates, omix = _proj_call(
        x, g_mix_norm[l][None, :], w_in_p, gq, gksl, gkwn, cos2, sin2, bd, gsgu, spw, spb,
        gout[None, ATTN_WIDTH:])

    n_rows = S // CMP_STRIDE
    pe_rows = jnp.stack([_expand_pe(cmp_pe[l, 0]), _expand_pe(cmp_pe[l, 1])])
    w1x = jnp.stack([_expand_cmp_w1(cmp_w1[l, 0]), _expand_cmp_w1(cmp_w1[l, 1])])
    kcmp, vcmp = _compress_call(
        kc.reshape(B, n_rows, CMP_STRIDE * KV_WIDTH), vc.reshape(B, n_rows, CMP_STRIDE * KV_WIDTH),
        pe_rows, w1x, cmp_w2[l].astype(_bf16), g_k[l, 0][None, :])

    oattn = _attn_call(qn, qr, kcmp, vcmp, ksl, vsl, kwn, vwn, gates, ovt, masks,
                       jnp.broadcast_to(gout[:ATTN_WIDTH, None], (ATTN_WIDTH, ATTN_Q_TILE)))

    out = _ffn_call(x.reshape(B * S, D_MODEL), oattn.reshape(B * S, ATTN_WIDTH),
                    omix.reshape(B * S, MIX_WIDTH), w_out[l].astype(_bf16), g_ffn_norm[l][None, :],
                    w_ff1[l].astype(_bf16), w_ff2[l].astype(_bf16))
    return out.reshape(B, S, D_MODEL), (qn, qr, kc, vc, ksl, vsl, kwn, vwn, gates, omix, kcmp, vcmp, oattn)


def kernel(x, g_mix_norm, w_in, g_q, g_k, cmp_pe, cmp_w1, cmp_w2, g_sgu, sp_w, sp_b, g_out, w_out,
           g_ffn_norm, w_ff1, w_ff2):
    return _forward(x, g_mix_norm, w_in, g_q, g_k, cmp_pe, cmp_w1, cmp_w2, g_sgu, sp_w, sp_b, g_out, w_out,
                    g_ffn_norm, w_ff1, w_ff2)[0]
```

```python
import functools

import numpy as np
import jax
import jax.numpy as jnp
from jax import lax
from jax.experimental import pallas as pl
from jax.experimental.pallas import tpu as pltpu

D_MODEL = 1024
HEAD_DIM = 64
N_ATTN_HEADS = 8
N_MIX_GROUPS = 8
GQA_GROUP = 4
N_KV_HEADS = 2
ATTN_WIDTH = 512
MIX_WIDTH = 512
KV_WIDTH = 128
N_GATES = 3
CMP_BLOCK = 32
CMP_STRIDE = 16
CMP_HIDDEN = 256
SEL_BLOCK = 64
SEL_TOPN = 16
WINDOW = 512
CHUNK = 128
D_FF = 4 * D_MODEL
ROPE_THETA = 10000.0
EPS = 1e-6
FORCE_SCORE = 1e9

LANES = 128
NEG_BIG = -1e30
PROJ_TILE = 256
PROJ_UNIT = 512
ATTN_Q_TILE = 128
SEL_CHUNK = 512
ATTN_UNIT = 256
STREAM_SKEW = 1
MASK_FULL, MASK_DIAG, MASK_LOW, MASK_NONE = 0, 1, 2, 3
FFN_TILE = 256
FF_CHUNK = 1024
VMEM_LIMIT = 56 * 1024 * 1024

_C_Q = 0
_C_KV = 512
_KV_KC, _KV_VC, _KV_KSL, _KV_VSL, _KV_KWN, _KV_VWN, _KV_GATE = (128 * n for n in range(7))
_KV_GROUP = 1024
_C_ZU = 1536
_C_ZV = 2048
_IN_COLS_PACKED = 2560
MXU_DIM = 256

_bf16 = jnp.bfloat16
_f32 = jnp.float32


def _dot(a, b):
    return jnp.dot(a, b, preferred_element_type=_f32)


def _split_dot(v, m):
    hi = v.astype(_bf16)
    lo = (v - hi.astype(_f32)).astype(_bf16)
    return _dot(hi, m) + _dot(lo, m)


def _group_mean(sq, bd_ref):
    cw = min(sq.shape[1], MXU_DIM)
    bd = bd_ref[0:cw, 0:cw]
    parts = [_split_dot(sq[:, c * cw:(c + 1) * cw], bd) for c in range(sq.shape[1] // cw)]
    return parts[0] if len(parts) == 1 else jnp.concatenate(parts, axis=1)


def _swap_halves(v):
    n = v.shape[1]
    lane = lax.broadcasted_iota(jnp.int32, v.shape, 1)
    return jnp.where((lane % HEAD_DIM) < HEAD_DIM // 2, pltpu.roll(v, n - HEAD_DIM // 2, 1),
                     pltpu.roll(v, HEAD_DIM // 2, 1))


def _proj_kernel(tiles_per_seq, x_ref, gmix_ref, w_ref, gq_ref, gksl_ref, gkwn_ref, cos_ref, sin_ref, bd_ref,
                 gsgu_ref, spw_ref, spb_ref, gomix_ref,
                 qn_ref, qr_ref, kc_ref, vc_ref, ksl_ref, vsl_ref, kwn_ref, vwn_ref, gate_ref, omix_ref,
                 h_even, h_odd, z_even, z_odd):
    t = pl.program_id(0)

    @pl.when(t == 0)
    def _():
        h_odd[...] = jnp.zeros(h_odd.shape, _bf16)
        z_odd[...] = jnp.zeros(z_odd.shape, _f32)

    pl.when(t % 2 == 0)(functools.partial(
        _proj_step, tiles_per_seq, x_ref, gmix_ref, w_ref, gq_ref, gksl_ref, gkwn_ref, cos_ref, sin_ref, bd_ref,
        gsgu_ref, spw_ref, spb_ref, gomix_ref, qn_ref, qr_ref, kc_ref, vc_ref, ksl_ref, vsl_ref, kwn_ref,
        vwn_ref, gate_ref, omix_ref, h_even, h_odd, z_even, z_odd))
    pl.when(t % 2 == 1)(functools.partial(
        _proj_step, tiles_per_seq, x_ref, gmix_ref, w_ref, gq_ref, gksl_ref, gkwn_ref, cos_ref, sin_ref, bd_ref,
        gsgu_ref, spw_ref, spb_ref, gomix_ref, qn_ref, qr_ref, kc_ref, vc_ref, ksl_ref, vsl_ref, kwn_ref,
        vwn_ref, gate_ref, omix_ref, h_odd, h_even, z_odd, z_even))


def _proj_step(tiles_per_seq, x_ref, gmix_ref, w_ref, gq_ref, gksl_ref, gkwn_ref, cos_ref, sin_ref, bd_ref,
               gsgu_ref, spw_ref, spb_ref, gomix_ref,
               qn_ref, qr_ref, kc_ref, vc_ref, ksl_ref, vsl_ref, kwn_ref, vwn_ref, gate_ref, omix_ref,
               h_next, h_prev, z_next, z_prev):
    tm = x_ref.shape[0]
    i = jnp.maximum(pl.program_id(0) - 2, 0) % tiles_per_seq
    cos = cos_ref[...]
    sin = sin_ref[...]

    def project_stream():
        h = h_prev[...]
        for lo in range(0, _IN_COLS_PACKED, PROJ_UNIT):
            z_next[:, lo:lo + PROJ_UNIT] = _dot(h, w_ref[:, lo:lo + PROJ_UNIT])
            yield

    def prenorm_stream():
        x = x_ref[...]
        ms = jnp.mean(x * x, axis=-1, keepdims=True)
        h_next[...] = (x * lax.rsqrt(ms + EPS) * gmix_ref[...]).astype(_bf16)
        yield

    def query_stream():
        zq = z_prev[:, _C_Q:_C_Q + ATTN_WIDTH]
        qn = zq * lax.rsqrt(_group_mean(zq * zq, bd_ref) + EPS) * gq_ref[...]
        yield
        cos4 = jnp.concatenate([cos] * 4, axis=1)
        sin4 = jnp.concatenate([sin] * 4, axis=1)
        qr = qn * cos4 + _swap_halves(qn) * sin4
        yield
        qn_t, qr_t = qn.T, qr.T
        for hq in range(N_ATTN_HEADS):
            sl = slice(hq * HEAD_DIM, (hq + 1) * HEAD_DIM)
            qn_ref[0, hq] = qn_t[sl].astype(_bf16)
            qr_ref[0, hq] = qr_t[sl].astype(_bf16)

    def kv_stream():
        def segment(off):
            return z_prev[:, _C_KV + off:_C_KV + off + KV_WIDTH]

        kc_ref[0] = segment(_KV_KC).astype(_bf16)
        vc_ref[0] = segment(_KV_VC).astype(_bf16)
        gate_ref[0] = jax.nn.sigmoid(segment(_KV_GATE))

        ones_block = jnp.where(lax.broadcasted_iota(jnp.int32, (HEAD_DIM, LANES), 0) == 0, 1.0, 0.0)
        for off, out_ref in ((_KV_VSL, vsl_ref), (_KV_VWN, vwn_ref)):
            zv2 = segment(off)
            for tt in range(tm // LANES):
                zt = zv2[tt * LANES:(tt + 1) * LANES, :].T
                for hh in range(N_KV_HEADS):
                    out_ref[0, hh, tt] = jnp.concatenate(
                        [zt[hh * HEAD_DIM:(hh + 1) * HEAD_DIM], ones_block], axis=0).astype(_bf16)
            yield

        lane = lax.broadcasted_iota(jnp.int32, (tm, LANES), 1)
        pos = i * tm + lax.broadcasted_iota(jnp.int32, (tm, LANES), 0)
        blk_onehot = jnp.where(lane - HEAD_DIM == pos // SEL_BLOCK, 1.0, 0.0)
        for off, g_ref, out_ref, extra in ((_KV_KSL, gksl_ref, ksl_ref, blk_onehot),
                                           (_KV_KWN, gkwn_ref, kwn_ref, jnp.zeros((tm, LANES), _f32))):
            zk = segment(off)
            kn = zk * lax.rsqrt(_group_mean(zk * zk, bd_ref) + EPS) * g_ref[...]
            kr = kn * cos + _swap_halves(kn) * sin
            out_ref[0, 0] = jnp.where(lane < HEAD_DIM, kr, extra).astype(_bf16)
            out_ref[0, 1] = jnp.where(lane < HEAD_DIM, pltpu.roll(kr, HEAD_DIM, 1), extra).astype(_bf16)
            yield

    row = lax.broadcasted_iota(jnp.int32, (CHUNK, 2 * CHUNK), 0)
    colw = lax.broadcasted_iota(jnp.int32, (CHUNK, 2 * CHUNK), 1) % CHUNK
    causal_w = colw <= row
    lane_c = lax.broadcasted_iota(jnp.int32, (CHUNK, LANES), 1)

    def gmlp_stream(c):
        rows = slice(c * CHUNK, (c + 1) * CHUNK)
        zu = jax.nn.gelu(z_prev[rows, _C_ZU:_C_ZU + MIX_WIDTH])
        yield
        zv = jax.nn.gelu(z_prev[rows, _C_ZV:_C_ZV + MIX_WIDTH])
        yield
        vn = zv * lax.rsqrt(_group_mean(zv * zv, bd_ref) + EPS) * gsgu_ref[...]
        yield
        sv_parts = []
        for p in range(N_MIX_GROUPS // 2):
            vp = vn[:, p * LANES:(p + 1) * LANES]
            rhs = jnp.concatenate([jnp.where(lane_c < HEAD_DIM, vp, 0.0),
                                   jnp.where(lane_c < HEAD_DIM, 0.0, vp)], axis=0).astype(_bf16)
            w_pair = jnp.where(causal_w, spw_ref[p], jnp.zeros((), _bf16))
            sv_parts.append(_dot(w_pair, rhs))
        yield
        sv = jnp.concatenate(sv_parts, axis=1) + spb_ref[...]
        om = zu * sv
        oms = jnp.mean(om * om, axis=-1, keepdims=True)
        omix_ref[0, rows] = (om * lax.rsqrt(oms + EPS) * gomix_ref[...]).astype(_bf16)

    _interleave([project_stream(), query_stream(), kv_stream()] + [gmlp_stream(c) for c in range(tm // CHUNK)]
                + [prenorm_stream()])


def _proj_call(x, gmix, w_in_p, gq, gksl, gkwn, cos, sin, bd, gsgu, spw, spb, gomix):
    B, S, _ = x.shape
    tm = PROJ_TILE
    nt = S // tm
    n_tiles = B * nt
    const2 = lambda t: (0, 0)
    const3 = lambda t: (0, 0, 0)

    def done(t):
        return jnp.maximum(t - 2, 0)

    tok3 = lambda t: (done(t) // nt, done(t) % nt, 0)
    head4 = lambda t: (done(t) // nt, 0, done(t) % nt, 0)
    out_shape = (
        jax.ShapeDtypeStruct((B, N_ATTN_HEADS, HEAD_DIM, S), _bf16),
        jax.ShapeDtypeStruct((B, N_ATTN_HEADS, HEAD_DIM, S), _bf16),
        jax.ShapeDtypeStruct((B, S, KV_WIDTH), _bf16),
        jax.ShapeDtypeStruct((B, S, KV_WIDTH), _bf16),
        jax.ShapeDtypeStruct((B, N_KV_HEADS, S, LANES), _bf16),
        jax.ShapeDtypeStruct((B, N_KV_HEADS, S // LANES, LANES, LANES), _bf16),
        jax.ShapeDtypeStruct((B, N_KV_HEADS, S, LANES), _bf16),
        jax.ShapeDtypeStruct((B, N_KV_HEADS, S // LANES, LANES, LANES), _bf16),
        jax.ShapeDtypeStruct((B, S, LANES), _f32),
        jax.ShapeDtypeStruct((B, S, MIX_WIDTH), _bf16),
    )
    q_spec = pl.BlockSpec((1, N_ATTN_HEADS, HEAD_DIM, tm), lambda t: (done(t) // nt, 0, 0, done(t) % nt))
    k_spec = pl.BlockSpec((1, N_KV_HEADS, tm, LANES), head4)
    t_spec = pl.BlockSpec((1, tm, LANES), tok3)
    vt_spec = pl.BlockSpec((1, N_KV_HEADS, tm // LANES, LANES, LANES),
                           lambda t: (done(t) // nt, 0, done(t) % nt, 0, 0))
    return pl.pallas_call(
        functools.partial(_proj_kernel, nt),
        grid=(n_tiles + 2,),
        in_specs=[
            pl.BlockSpec((tm, D_MODEL), lambda t: (jnp.minimum(t, n_tiles - 1), 0)),
            pl.BlockSpec((1, D_MODEL), const2),
            pl.BlockSpec((D_MODEL, _IN_COLS_PACKED), const2),
            pl.BlockSpec((1, ATTN_WIDTH), const2),
            pl.BlockSpec((1, KV_WIDTH), const2),
            pl.BlockSpec((1, KV_WIDTH), const2),
            pl.BlockSpec((tm, LANES), lambda t: (done(t) % nt, 0)),
            pl.BlockSpec((tm, LANES), lambda t: (done(t) % nt, 0)),
            pl.BlockSpec((MXU_DIM, MXU_DIM), const2),
            pl.BlockSpec((1, MIX_WIDTH), const2),
            pl.BlockSpec((N_MIX_GROUPS // 2, CHUNK, 2 * CHUNK), const3),
            pl.BlockSpec((CHUNK, MIX_WIDTH), const2),
            pl.BlockSpec((1, MIX_WIDTH), const2),
        ],
        out_specs=(q_spec, q_spec, t_spec, t_spec, k_spec, vt_spec, k_spec, vt_spec, t_spec,
                   pl.BlockSpec((1, tm, MIX_WIDTH), tok3)),
        out_shape=out_shape,
        scratch_shapes=[pltpu.VMEM((tm, D_MODEL), _bf16), pltpu.VMEM((tm, D_MODEL), _bf16),
                        pltpu.VMEM((tm, _IN_COLS_PACKED), _f32), pltpu.VMEM((tm, _IN_COLS_PACKED), _f32)],
        compiler_params=pltpu.CompilerParams(
            dimension_semantics=("arbitrary",), vmem_limit_bytes=VMEM_LIMIT),
        name="proj",
    )(x.reshape(B * S, D_MODEL), gmix, w_in_p, gq, gksl, gkwn, cos, sin, bd, gsgu, spw, spb, gomix)


def _compress_kernel(kc_ref, vc_ref, pe_ref, w1_ref, w2_ref, gk_ref, kcmp_ref, vcmp_ref):
    n_rows = kc_ref.shape[1]
    zero_half = jnp.zeros((n_rows, HEAD_DIM), _f32)
    for t, (src_ref, out_ref) in enumerate(((kc_ref, kcmp_ref), (vc_ref, vcmp_ref))):
        g = src_ref[0].astype(_f32)
        first = _dot((g + pe_ref[t, 0:1]).astype(_bf16), w1_ref[t, 0])
        second = _dot((g + pe_ref[t, 1:2]).astype(_bf16), w1_ref[t, 1])
        hid = jax.nn.gelu(first + pltpu.roll(second, n_rows - 1, 0)).astype(_bf16)
        heads = [_dot(hid[:, hh * CMP_HIDDEN:(hh + 1) * CMP_HIDDEN], w2_ref[t]) for hh in range(N_KV_HEADS)]
        if t == 0:
            heads = [c * lax.rsqrt(jnp.mean(c * c, axis=-1, keepdims=True) + EPS) * gk_ref[...] for c in heads]
            for hh in range(N_KV_HEADS):
                out_ref[0, hh] = jnp.concatenate([heads[hh], zero_half], axis=1).astype(_bf16)
        else:
            for hh in range(N_KV_HEADS):
                vt = jnp.concatenate([heads[hh], zero_half], axis=1).T
                out_ref[0, hh] = vt.astype(_bf16)


def _compress_call(kc_g, vc_g, pe_rows, w1x, w2, gk0):
    B, n_rows, width = kc_g.shape
    return pl.pallas_call(
        _compress_kernel,
        grid=(B,),
        in_specs=[
            pl.BlockSpec((1, n_rows, width), lambda b: (b, 0, 0)),
            pl.BlockSpec((1, n_rows, width), lambda b: (b, 0, 0)),
            pl.BlockSpec((2, 2, width), lambda b: (0, 0, 0)),
            pl.BlockSpec((2, 2, width, N_KV_HEADS * CMP_HIDDEN), lambda b: (0, 0, 0, 0)),
            pl.BlockSpec((2, CMP_HIDDEN, HEAD_DIM), lambda b: (0, 0, 0)),
            pl.BlockSpec((1, HEAD_DIM), lambda b: (0, 0)),
        ],
        out_specs=(pl.BlockSpec((1, N_KV_HEADS, n_rows, LANES), lambda b: (b, 0, 0, 0)),
                   pl.BlockSpec((1, N_KV_HEADS, n_rows, LANES), lambda b: (b, 0, 0, 0))),
        out_shape=(jax.ShapeDtypeStruct((B, N_KV_HEADS, n_rows, LANES), _bf16),
                   jax.ShapeDtypeStruct((B, N_KV_HEADS, n_rows, LANES), _bf16)),
        compiler_params=pltpu.CompilerParams(
            dimension_semantics=("parallel",), vmem_limit_bytes=VMEM_LIMIT),
        name="compress",
    )(kc_g, vc_g, pe_rows, w1x, w2, gk0)


def _interleave(streams):
    live = list(streams)
    while live:
        for g in list(live):
            try:
                next(g)
            except StopIteration:
                live.remove(g)


def _delayed(stream, units):
    for _ in range(units):
        yield
    yield from stream


def _attn_kernel(qn_ref, qr_ref, kcmp_ref, vcmpt_ref, ksl_ref, vslt_ref, kwn_ref, vwnt_ref, gate_ref,
                 ovt_ref, mask_ref, gout_ref, o_ref):
    tq = o_ref.shape[1]
    rows = GQA_GROUP * tq
    S = ksl_ref.shape[2]
    n_cmp_pad = kcmp_ref.shape[2]
    n_sel = S // SEL_BLOCK
    win_tiles = WINDOW // tq
    unit_tiles = ATTN_UNIT // tq
    i = pl.program_id(1)
    q0 = i * tq

    qpos = q0 + lax.broadcasted_iota(jnp.int32, (1, rows), 1) % tq
    blk = lax.broadcasted_iota(jnp.int32, (n_sel, tq), 0)
    tpos = q0 + lax.broadcasted_iota(jnp.int32, (n_sel, tq), 1)
    gates_t = gate_ref[0].T
    zero_rows = jnp.zeros((HEAD_DIM, rows), _bf16)

    def stacked_heads_t(ref, h):
        return jnp.concatenate([ref[0, GQA_GROUP * h + g] for g in range(GQA_GROUP)], axis=1)

    def tile_mask(key_tile, low_edge=False):
        kind = jnp.where(key_tile < i, MASK_FULL, jnp.where(key_tile == i, MASK_DIAG, MASK_NONE))
        if low_edge:
            kind = jnp.where(i >= win_tiles, MASK_LOW, kind)
        return mask_ref[kind]

    def attend(out, h, k_ref, vt_ref, q_t, first_tile, tile_masks):
        n_tiles = len(tile_masks)
        scores, mx = [], None
        for u0 in range(0, n_tiles, unit_tiles):
            nt = min(unit_tiles, n_tiles - u0)
            if isinstance(first_tile, int):
                keys = slice((first_tile + u0) * tq, (first_tile + u0 + nt) * tq)
            else:
                keys = pl.ds(pl.multiple_of((first_tile + u0) * tq, tq), nt * tq)
            sc = _dot(k_ref[0, h, keys, :], q_t)
            if any(tile_masks[u0 + t] is not None for t in range(nt)):
                sc = jnp.concatenate(
                    [sc[t * tq:(t + 1) * tq] if tile_masks[u0 + t] is None
                     else sc[t * tq:(t + 1) * tq] + tile_masks[u0 + t]() for t in range(nt)], axis=0)
            scores.append(sc)
            cm = jnp.max(sc, axis=0, keepdims=True)
            mx = cm if mx is None else jnp.maximum(mx, cm)
            yield
        probs = []
        for sc in scores:
            probs.append(jnp.exp2(sc - mx).astype(_bf16))
            yield
        vt = jnp.concatenate([vt_ref[0, h, first_tile + t] for t in range(n_tiles)], axis=1)
        acc = _dot(vt, jnp.concatenate(probs, axis=0))
        out[h] = acc[0:HEAD_DIM] / acc[HEAD_DIM:HEAD_DIM + 1]
        yield

    def window_stream(out, h, c):
        first_tile = jnp.maximum(i - win_tiles, 0)
        q_t = jnp.concatenate([stacked_heads_t(qr_ref, h), zero_rows], axis=0)
        if c * SEL_CHUNK >= WINDOW:
            masks = ([lambda: mask_ref[MASK_LOW]] + [None] * (win_tiles - 1) + [lambda: mask_ref[MASK_DIAG]])
        else:
            masks = [functools.partial(tile_mask, first_tile + t, low_edge=(t == 0))
                     for t in range(win_tiles + 1)]
        yield from attend(out, h, kwn_ref, vwnt_ref, q_t, first_tile, masks)

    def compressed_and_selected_stream(o_cmps, o_sels, h, c):
        qn = stacked_heads_t(qn_ref, h)
        qr = stacked_heads_t(qr_ref, h)

        s = _dot(kcmp_ref[0, h], jnp.concatenate([qn, zero_rows], axis=0))
        n_idx = lax.broadcasted_iota(jnp.int32, (n_cmp_pad, 1), 0)
        valid_c = n_idx * CMP_STRIDE + (CMP_BLOCK - 1) <= qpos
        s = jnp.where(valid_c, s, NEG_BIG)
        m = jnp.max(s, axis=0, keepdims=True)
        e = jnp.where(valid_c, jnp.exp2(s - m), 0.0)
        p_c = e / jnp.maximum(jnp.sum(e, axis=0, keepdims=True), 1e-20)
        o_cmps[h] = _dot(vcmpt_ref[0, h], p_c.astype(_bf16))[0:HEAD_DIM]
        yield

        eligible = blk * SEL_BLOCK <= tpos
        if SEL_CHUNK * (c + 1) <= SEL_TOPN * SEL_BLOCK:
            chosen = eligible
        else:
            p_sum = p_c[:, 0:tq] + p_c[:, tq:2 * tq] + p_c[:, 2 * tq:3 * tq] + p_c[:, 3 * tq:4 * tq]
            p_hi = p_sum.astype(_bf16)
            p_lo = (p_sum - p_hi.astype(_f32)).astype(_bf16)
            imp = _dot(ovt_ref[...], p_hi) + _dot(ovt_ref[...], p_lo)
            cur = tpos // SEL_BLOCK
            forced = (blk == 0) | (blk == cur) | (blk == cur - 1)
            score = jnp.where(forced, FORCE_SCORE, jnp.where(eligible, imp, -jnp.inf))
            rank = jnp.zeros((n_sel, tq), _f32)
            for jp in range(n_sel):
                other = score[jp:jp + 1, :]
                tie = jnp.where(blk > jp, 1.0, 0.0)
                rank = rank + jnp.where(other > score, 1.0, jnp.where(other == score, tie, 0.0))
            chosen = (rank < float(SEL_TOPN)) & eligible
        sel_bias = jnp.where(chosen, 0.0, NEG_BIG).astype(_bf16)
        yield

        q_t = jnp.concatenate([qr, jnp.concatenate([sel_bias] * GQA_GROUP, axis=1),
                               jnp.zeros((LANES - HEAD_DIM - n_sel, rows), _bf16)], axis=0)
        n_tiles = (SEL_CHUNK // tq) * (c + 1)
        first_edge = n_tiles - SEL_CHUNK // tq
        masks = [None if t < first_edge else functools.partial(tile_mask, t) for t in range(n_tiles)]
        yield from attend(o_sels, h, ksl_ref, vslt_ref, q_t, 0, masks)

    def step_body(c):
        o_wins, o_cmps, o_sels = {}, {}, {}
        _interleave([
            window_stream(o_wins, 0, c),
            compressed_and_selected_stream(o_cmps, o_sels, 0, c),
            _delayed(window_stream(o_wins, 1, c), STREAM_SKEW),
            _delayed(compressed_and_selected_stream(o_cmps, o_sels, 1, c), STREAM_SKEW),
        ])

        slabs = []
        for h in range(N_KV_HEADS):
            for g in range(GQA_GROUP):
                hq = GQA_GROUP * h + g
                cols = slice(g * tq, (g + 1) * tq)
                slabs.append(gates_t[hq:hq + 1] * o_cmps[h][:, cols]
                             + gates_t[N_ATTN_HEADS + hq:N_ATTN_HEADS + hq + 1] * o_sels[h][:, cols]
                             + gates_t[2 * N_ATTN_HEADS + hq:2 * N_ATTN_HEADS + hq + 1] * o_wins[h][:, cols])
        ot = jnp.concatenate(slabs, axis=0)
        ms = jnp.mean(ot * ot, axis=0, keepdims=True)
        o_ref[0] = (ot * lax.rsqrt(ms + EPS) * gout_ref[...]).T.astype(_bf16)

    cls = q0 // SEL_CHUNK
    for c in range(S // SEL_CHUNK):
        pl.when(cls == c)(functools.partial(step_body, c))


def _attn_call(qn_t, qr_t, kcmp, vcmp_t, ksl, vsl_t, kwn, vwn_t, gates, ovt, masks, gout_attn):
    B, _, _, S = qn_t.shape
    tq = ATTN_Q_TILE
    assert tq == LANES and S % SEL_CHUNK == 0 and SEL_CHUNK % tq == 0 and WINDOW % tq == 0
    assert ATTN_UNIT % tq == 0 and kcmp.shape[2] == LANES
    n_cmp_pad = kcmp.shape[2]
    n_sel = S // SEL_BLOCK
    rows = GQA_GROUP * tq
    per_b4 = lambda b, i: (b, 0, 0, 0)
    const2 = lambda b, i: (0, 0)
    vt_spec = pl.BlockSpec((1, N_KV_HEADS, S // LANES, LANES, LANES), lambda b, i: (b, 0, 0, 0, 0))
    return pl.pallas_call(
        _attn_kernel,
        grid=(B, S // tq),
        in_specs=[
            pl.BlockSpec((1, N_ATTN_HEADS, HEAD_DIM, tq), lambda b, i: (b, 0, 0, i)),
            pl.BlockSpec((1, N_ATTN_HEADS, HEAD_DIM, tq), lambda b, i: (b, 0, 0, i)),
            pl.BlockSpec((1, N_KV_HEADS, n_cmp_pad, LANES), per_b4),
            pl.BlockSpec((1, N_KV_HEADS, n_cmp_pad, LANES), per_b4),
            pl.BlockSpec((1, N_KV_HEADS, S, LANES), per_b4),
            vt_spec,
            pl.BlockSpec((1, N_KV_HEADS, S, LANES), per_b4),
            vt_spec,
            pl.BlockSpec((1, tq, LANES), lambda b, i: (b, i, 0)),
            pl.BlockSpec((n_sel, n_cmp_pad), const2),
            pl.BlockSpec((4, tq, rows), lambda b, i: (0, 0, 0)),
            pl.BlockSpec((ATTN_WIDTH, tq), const2),
        ],
        out_specs=pl.BlockSpec((1, tq, ATTN_WIDTH), lambda b, i: (b, i, 0)),
        out_shape=jax.ShapeDtypeStruct((B, S, ATTN_WIDTH), _bf16),
        compiler_params=pltpu.CompilerParams(
            dimension_semantics=("parallel", "arbitrary"), vmem_limit_bytes=VMEM_LIMIT),
        name="attn",
    )(qn_t, qr_t, kcmp, vcmp_t, ksl, vsl_t, kwn, vwn_t, gates, ovt, masks, gout_attn)


def _ffn_kernel(x_ref, oa_ref, om_ref, wo_ref, gffn_ref, w1_ref, w2_ref, out_ref, act_scr):
    x2 = (x_ref[...] + _dot(oa_ref[...], wo_ref[0:ATTN_WIDTH, :])
          + _dot(om_ref[...], wo_ref[ATTN_WIDTH:ATTN_WIDTH + MIX_WIDTH, :]))
    ms = jnp.mean(x2 * x2, axis=-1, keepdims=True)
    h = (x2 * lax.rsqrt(ms + EPS) * gffn_ref[...]).astype(_bf16)
    for c in range(D_FF // FF_CHUNK):
        cols = slice(c * FF_CHUNK, (c + 1) * FF_CHUNK)
        a = jnp.maximum(_dot(h, w1_ref[:, cols]), 0.0)
        act_scr[:, cols] = (a * a).astype(_bf16)
    out_ref[...] = x2 + _dot(act_scr[...], w2_ref[...])


def _ffn_call(x2d, oa, om, wo, gffn, w1, w2):
    T = x2d.shape[0]
    tm = FFN_TILE
    tok = lambda i: (i, 0)
    const = lambda i: (0, 0)
    return pl.pallas_call(
        _ffn_kernel,
        grid=(T // tm,),
        in_specs=[
            pl.BlockSpec((tm, D_MODEL), tok),
            pl.BlockSpec((tm, ATTN_WIDTH), tok),
            pl.BlockSpec((tm, MIX_WIDTH), tok),
            pl.BlockSpec((D_MODEL, D_MODEL), const),
            pl.BlockSpec((1, D_MODEL), const),
            pl.BlockSpec((D_MODEL, D_FF), const),
            pl.BlockSpec((D_FF, D_MODEL), const),
        ],
        out_specs=pl.BlockSpec((tm, D_MODEL), tok),
        out_shape=jax.ShapeDtypeStruct((T, D_MODEL), _f32),
        scratch_shapes=[pltpu.VMEM((tm, D_FF), _bf16)],
        compiler_params=pltpu.CompilerParams(
            dimension_semantics=("parallel",), vmem_limit_bytes=VMEM_LIMIT),
        name="ffn",
    )(x2d, oa, om, wo, gffn, w1, w2)


def _constants(S):
    half = HEAD_DIM // 2
    inv = ROPE_THETA ** (-jnp.arange(half, dtype=_f32) / half)
    ang = jnp.arange(S, dtype=_f32)[:, None] * inv[None, :]
    cos = jnp.cos(ang)
    sin = jnp.sin(ang)
    cos2 = jnp.concatenate([cos, cos, cos, cos], axis=1)
    sin2 = jnp.concatenate([-sin, sin, -sin, sin], axis=1)
    mlane = np.arange(MXU_DIM)
    bd = (mlane[:, None] // HEAD_DIM == mlane[None, :] // HEAD_DIM).astype(np.float32) / HEAD_DIM
    n_cmp = (S - CMP_BLOCK) // CMP_STRIDE + 1
    n_sel = S // SEL_BLOCK
    n_cmp_pad = S // CMP_STRIDE
    cmp_start = np.arange(n_cmp)[:, None] * CMP_STRIDE
    sel_start = np.arange(n_sel)[None, :] * SEL_BLOCK
    overlap = np.clip(np.minimum(cmp_start + CMP_BLOCK, sel_start + SEL_BLOCK)
                      - np.maximum(cmp_start, sel_start), 0, None).astype(np.float32) / CMP_BLOCK
    ovt = np.zeros((n_sel, n_cmp_pad), np.float32)
    ovt[:, :n_cmp] = overlap.T
    qq = (np.arange(GQA_GROUP * ATTN_Q_TILE) % ATTN_Q_TILE)[None, :]
    kk = np.arange(ATTN_Q_TILE)[:, None]
    masks = np.zeros((4, ATTN_Q_TILE, GQA_GROUP * ATTN_Q_TILE), np.float32)
    masks[MASK_DIAG] = np.where(kk <= qq, 0.0, NEG_BIG)
    masks[MASK_LOW] = np.where(kk > qq, 0.0, NEG_BIG)
    masks[MASK_NONE] = NEG_BIG
    return cos2, sin2, jnp.asarray(bd, _bf16), jnp.asarray(ovt, _bf16), jnp.asarray(masks, _f32)


def _pack_w_in(w):
    gate0 = ATTN_WIDTH + 6 * KV_WIDTH
    n_g = N_GATES * N_ATTN_HEADS
    gates = w[:, gate0:gate0 + n_g].reshape(D_MODEL, N_ATTN_HEADS, N_GATES)
    gates = jnp.transpose(gates, (0, 2, 1)).reshape(D_MODEL, n_g)
    pad = jnp.zeros((D_MODEL, _C_KV + _KV_GROUP - gate0 - n_g), w.dtype)
    return jnp.concatenate([w[:, :gate0], gates, pad, w[:, gate0 + n_g:]], axis=1).astype(_bf16)


def _expand_cmp_w1(w1):
    w = w1.reshape(2, CMP_STRIDE, HEAD_DIM, CMP_HIDDEN)
    z = jnp.zeros_like(w)
    h0 = jnp.concatenate([w, z], axis=-1)
    h1 = jnp.concatenate([z, w], axis=-1)
    both = jnp.stack([h0, h1], axis=2)
    return both.reshape(2, CMP_STRIDE * KV_WIDTH, N_KV_HEADS * CMP_HIDDEN).astype(_bf16)


def _expand_pe(pe):
    p = pe.reshape(2, CMP_STRIDE, 1, HEAD_DIM)
    return jnp.broadcast_to(p, (2, CMP_STRIDE, N_KV_HEADS, HEAD_DIM)).reshape(2, CMP_STRIDE * KV_WIDTH)


def _forward(x, g_mix_norm, w_in, g_q, g_k, cmp_pe, cmp_w1, cmp_w2, g_sgu, sp_w, sp_b, g_out, w_out,
             g_ffn_norm, w_ff1, w_ff2):
    B, S, _ = x.shape
    l = 0
    scale = float(HEAD_DIM ** -0.5 * np.log2(np.e))
    cos2, sin2, bd, ovt, masks = _constants(S)

    w_in_p = _pack_w_in(w_in[l])
    gq = (jnp.tile(g_q[l], N_ATTN_HEADS) * scale)[None, :]
    gksl = jnp.tile(g_k[l, 1], N_KV_HEADS)[None, :]
    gkwn = jnp.tile(g_k[l, 2], N_KV_HEADS)[None, :]
    gsgu = g_sgu[l].reshape(1, MIX_WIDTH)
    spw = sp_w[l].reshape(N_MIX_GROUPS // 2, 2, CHUNK, CHUNK)
    spw = jnp.transpose(spw, (0, 2, 1, 3)).reshape(N_MIX_GROUPS // 2, CHUNK, 2 * CHUNK).astype(_bf16)
    spb = jnp.repeat(sp_b[l].T, HEAD_DIM, axis=1)
    gout = g_out[l]

    qn, qr, kc, vc, ksl, vsl, kwn, vwn, gates, omix = _proj_call(
        x, g_mix_norm[l][None, :], w_in_p, gq, gksl, gkwn, cos2, sin2, bd, gsgu, spw, spb,
        gout[None, ATTN_WIDTH:])

    n_rows = S // CMP_STRIDE
    pe_rows = jnp.stack([_expand_pe(cmp_pe[l, 0]), _expand_pe(cmp_pe[l, 1])])
    w1x = jnp.stack([_expand_cmp_w1(cmp_w1[l, 0]), _expand_cmp_w1(cmp_w1[l, 1])])
    kcmp, vcmp = _compress_call(
        kc.reshape(B, n_rows, CMP_STRIDE * KV_WIDTH), vc.reshape(B, n_rows, CMP_STRIDE * KV_WIDTH),
        pe_rows, w1x, cmp_w2[l].astype(_bf16), g_k[l, 0][None, :])

    oattn = _attn_call(qn, qr, kcmp, vcmp, ksl, vsl, kwn, vwn, gates, ovt, masks,
                       jnp.broadcast_to(gout[:ATTN_WIDTH, None], (ATTN_WIDTH, ATTN_Q_TILE)))

    out = _ffn_call(x.reshape(B * S, D_MODEL), oattn.reshape(B * S, ATTN_WIDTH),
                    omix.reshape(B * S, MIX_WIDTH), w_out[l].astype(_bf16), g_ffn_norm[l][None, :],
                    w_ff1[l].astype(_bf16), w_ff2[l].astype(_bf16))
    return out.reshape(B, S, D_MODEL), (qn, qr, kc, vc, ksl, vsl, kwn, vwn, gates, omix, kcmp, vcmp, oattn)


def kernel(x, g_mix_norm, w_in, g_q, g_k, cmp_pe, cmp_w1, cmp_w2, g_sgu, sp_w, sp_b, g_out, w_out,
           g_ffn_norm, w_ff1, w_ff2):
    return _forward(x, g_mix_norm, w_in, g_q, g_k, cmp_pe, cmp_w1, cmp_w2, g_sgu, sp_w, sp_b, g_out, w_out,
                    g_ffn_norm, w_ff1, w_ff2)[0]
```

```python
import functools

import numpy as np
import jax
import jax.numpy as jnp
from jax import lax
from jax.experimental import pallas as pl
from jax.experimental.pallas import tpu as pltpu

D_MODEL = 1024
HEAD_DIM = 64
N_ATTN_HEADS = 8
N_MIX_GROUPS = 8
GQA_GROUP = 4
N_KV_HEADS = 2
ATTN_WIDTH = 512
MIX_WIDTH = 512
KV_WIDTH = 128
N_GATES = 3
CMP_BLOCK = 32
CMP_STRIDE = 16
CMP_HIDDEN = 256
SEL_BLOCK = 64
SEL_TOPN = 16
WINDOW = 512
CHUNK = 128
D_FF = 4 * D_MODEL
ROPE_THETA = 10000.0
EPS = 1e-6
FORCE_SCORE = 1e9

LANES = 128
NEG_BIG = -1e30
PROJ_TILE = 256
PROJ_UNIT = 512
ATTN_Q_TILE = 128
ATTN_SUBTILES = 2
TILE_SKEW = 4
SEL_CHUNK = 512
ATTN_UNIT = 256
STREAM_SKEW = 1
MASK_FULL, MASK_DIAG, MASK_LOW, MASK_NONE = 0, 1, 2, 3
FFN_TILE = 256
FF_CHUNK = 1024
VMEM_LIMIT = 56 * 1024 * 1024

_C_Q = 0
_C_KV = 512
_KV_KC, _KV_VC, _KV_KSL, _KV_VSL, _KV_KWN, _KV_VWN, _KV_GATE = (128 * n for n in range(7))
_KV_GROUP = 1024
_C_ZU = 1536
_C_ZV = 2048
_IN_COLS_PACKED = 2560
MXU_DIM = 256

_bf16 = jnp.bfloat16
_f32 = jnp.float32


def _dot(a, b):
    return jnp.dot(a, b, preferred_element_type=_f32)


def _split_dot(v, m):
    hi = v.astype(_bf16)
    lo = (v - hi.astype(_f32)).astype(_bf16)
    return _dot(hi, m) + _dot(lo, m)


def _group_mean(sq, bd_ref):
    cw = min(sq.shape[1], MXU_DIM)
    bd = bd_ref[0:cw, 0:cw]
    parts = [_split_dot(sq[:, c * cw:(c + 1) * cw], bd) for c in range(sq.shape[1] // cw)]
    return parts[0] if len(parts) == 1 else jnp.concatenate(parts, axis=1)


def _swap_halves(v):
    n = v.shape[1]
    lane = lax.broadcasted_iota(jnp.int32, v.shape, 1)
    return jnp.where((lane % HEAD_DIM) < HEAD_DIM // 2, pltpu.roll(v, n - HEAD_DIM // 2, 1),
                     pltpu.roll(v, HEAD_DIM // 2, 1))


def _proj_kernel(tiles_per_seq, x_ref, gmix_ref, w_ref, gq_ref, gksl_ref, gkwn_ref, cos_ref, sin_ref, bd_ref,
                 gsgu_ref, spw_ref, spb_ref, gomix_ref,
                 qn_ref, qr_ref, kc_ref, vc_ref, ksl_ref, vsl_ref, kwn_ref, vwn_ref, gate_ref, omix_ref,
                 h_even, h_odd, z_even, z_odd):
    t = pl.program_id(0)

    @pl.when(t == 0)
    def _():
        h_odd[...] = jnp.zeros(h_odd.shape, _bf16)
        z_odd[...] = jnp.zeros(z_odd.shape, _f32)

    pl.when(t % 2 == 0)(functools.partial(
        _proj_step, tiles_per_seq, x_ref, gmix_ref, w_ref, gq_ref, gksl_ref, gkwn_ref, cos_ref, sin_ref, bd_ref,
        gsgu_ref, spw_ref, spb_ref, gomix_ref, qn_ref, qr_ref, kc_ref, vc_ref, ksl_ref, vsl_ref, kwn_ref,
        vwn_ref, gate_ref, omix_ref, h_even, h_odd, z_even, z_odd))
    pl.when(t % 2 == 1)(functools.partial(
        _proj_step, tiles_per_seq, x_ref, gmix_ref, w_ref, gq_ref, gksl_ref, gkwn_ref, cos_ref, sin_ref, bd_ref,
        gsgu_ref, spw_ref, spb_ref, gomix_ref, qn_ref, qr_ref, kc_ref, vc_ref, ksl_ref, vsl_ref, kwn_ref,
        vwn_ref, gate_ref, omix_ref, h_odd, h_even, z_odd, z_even))


def _proj_step(tiles_per_seq, x_ref, gmix_ref, w_ref, gq_ref, gksl_ref, gkwn_ref, cos_ref, sin_ref, bd_ref,
               gsgu_ref, spw_ref, spb_ref, gomix_ref,
               qn_ref, qr_ref, kc_ref, vc_ref, ksl_ref, vsl_ref, kwn_ref, vwn_ref, gate_ref, omix_ref,
               h_next, h_prev, z_next, z_prev):
    tm = x_ref.shape[0]
    i = jnp.maximum(pl.program_id(0) - 2, 0) % tiles_per_seq
    cos = cos_ref[...]
    sin = sin_ref[...]

    def project_stream():
        h = h_prev[...]
        for lo in range(0, _IN_COLS_PACKED, PROJ_UNIT):
            z_next[:, lo:lo + PROJ_UNIT] = _dot(h, w_ref[:, lo:lo + PROJ_UNIT])
            yield

    def prenorm_stream():
        x = x_ref[...]
        ms = jnp.mean(x * x, axis=-1, keepdims=True)
        h_next[...] = (x * lax.rsqrt(ms + EPS) * gmix_ref[...]).astype(_bf16)
        yield

    def query_stream():
        zq = z_prev[:, _C_Q:_C_Q + ATTN_WIDTH]
        qn = zq * lax.rsqrt(_group_mean(zq * zq, bd_ref) + EPS) * gq_ref[...]
        yield
        cos4 = jnp.concatenate([cos] * 4, axis=1)
        sin4 = jnp.concatenate([sin] * 4, axis=1)
        qr = qn * cos4 + _swap_halves(qn) * sin4
        yield
        qn_t, qr_t = qn.T, qr.T
        for hq in range(N_ATTN_HEADS):
            sl = slice(hq * HEAD_DIM, (hq + 1) * HEAD_DIM)
            qn_ref[0, hq] = qn_t[sl].astype(_bf16)
            qr_ref[0, hq] = qr_t[sl].astype(_bf16)

    def kv_stream():
        def segment(off):
            return z_prev[:, _C_KV + off:_C_KV + off + KV_WIDTH]

        kc_ref[0] = segment(_KV_KC).astype(_bf16)
        vc_ref[0] = segment(_KV_VC).astype(_bf16)
        gate_ref[0] = jax.nn.sigmoid(segment(_KV_GATE))

        ones_block = jnp.where(lax.broadcasted_iota(jnp.int32, (HEAD_DIM, LANES), 0) == 0, 1.0, 0.0)
        for off, out_ref in ((_KV_VSL, vsl_ref), (_KV_VWN, vwn_ref)):
            zv2 = segment(off)
            for tt in range(tm // LANES):
                zt = zv2[tt * LANES:(tt + 1) * LANES, :].T
                for hh in range(N_KV_HEADS):
                    out_ref[0, hh, tt] = jnp.concatenate(
                        [zt[hh * HEAD_DIM:(hh + 1) * HEAD_DIM], ones_block], axis=0).astype(_bf16)
            yield

        lane = lax.broadcasted_iota(jnp.int32, (tm, LANES), 1)
        pos = i * tm + lax.broadcasted_iota(jnp.int32, (tm, LANES), 0)
        blk_onehot = jnp.where(lane - HEAD_DIM == pos // SEL_BLOCK, 1.0, 0.0)
        for off, g_ref, out_ref, extra in ((_KV_KSL, gksl_ref, ksl_ref, blk_onehot),
                                           (_KV_KWN, gkwn_ref, kwn_ref, jnp.zeros((tm, LANES), _f32))):
            zk = segment(off)
            kn = zk * lax.rsqrt(_group_mean(zk * zk, bd_ref) + EPS) * g_ref[...]
            kr = kn * cos + _swap_halves(kn) * sin
            out_ref[0, 0] = jnp.where(lane < HEAD_DIM, kr, extra).astype(_bf16)
            out_ref[0, 1] = jnp.where(lane < HEAD_DIM, pltpu.roll(kr, HEAD_DIM, 1), extra).astype(_bf16)
            yield

    row = lax.broadcasted_iota(jnp.int32, (CHUNK, 2 * CHUNK), 0)
    colw = lax.broadcasted_iota(jnp.int32, (CHUNK, 2 * CHUNK), 1) % CHUNK
    causal_w = colw <= row
    lane_c = lax.broadcasted_iota(jnp.int32, (CHUNK, LANES), 1)

    def gmlp_stream(c):
        rows = slice(c * CHUNK, (c + 1) * CHUNK)
        zu = jax.nn.gelu(z_prev[rows, _C_ZU:_C_ZU + MIX_WIDTH])
        yield
        zv = jax.nn.gelu(z_prev[rows, _C_ZV:_C_ZV + MIX_WIDTH])
        yield
        vn = zv * lax.rsqrt(_group_mean(zv * zv, bd_ref) + EPS) * gsgu_ref[...]
        yield
        sv_parts = []
        for p in range(N_MIX_GROUPS // 2):
            vp = vn[:, p * LANES:(p + 1) * LANES]
            rhs = jnp.concatenate([jnp.where(lane_c < HEAD_DIM, vp, 0.0),
                                   jnp.where(lane_c < HEAD_DIM, 0.0, vp)], axis=0).astype(_bf16)
            w_pair = jnp.where(causal_w, spw_ref[p], jnp.zeros((), _bf16))
            sv_parts.append(_dot(w_pair, rhs))
        yield
        sv = jnp.concatenate(sv_parts, axis=1) + spb_ref[...]
        om = zu * sv
        oms = jnp.mean(om * om, axis=-1, keepdims=True)
        omix_ref[0, rows] = (om * lax.rsqrt(oms + EPS) * gomix_ref[...]).astype(_bf16)

    _interleave([project_stream(), query_stream(), kv_stream()] + [gmlp_stream(c) for c in range(tm // CHUNK)]
                + [prenorm_stream()])


def _proj_call(x, gmix, w_in_p, gq, gksl, gkwn, cos, sin, bd, gsgu, spw, spb, gomix):
    B, S, _ = x.shape
    tm = PROJ_TILE
    nt = S // tm
    n_tiles = B * nt
    const2 = lambda t: (0, 0)
    const3 = lambda t: (0, 0, 0)

    def done(t):
        return jnp.maximum(t - 2, 0)

    tok3 = lambda t: (done(t) // nt, done(t) % nt, 0)
    head4 = lambda t: (done(t) // nt, 0, done(t) % nt, 0)
    out_shape = (
        jax.ShapeDtypeStruct((B, N_ATTN_HEADS, HEAD_DIM, S), _bf16),
        jax.ShapeDtypeStruct((B, N_ATTN_HEADS, HEAD_DIM, S), _bf16),
        jax.ShapeDtypeStruct((B, S, KV_WIDTH), _bf16),
        jax.ShapeDtypeStruct((B, S, KV_WIDTH), _bf16),
        jax.ShapeDtypeStruct((B, N_KV_HEADS, S, LANES), _bf16),
        jax.ShapeDtypeStruct((B, N_KV_HEADS, S // LANES, LANES, LANES), _bf16),
        jax.ShapeDtypeStruct((B, N_KV_HEADS, S, LANES), _bf16),
        jax.ShapeDtypeStruct((B, N_KV_HEADS, S // LANES, LANES, LANES), _bf16),
        jax.ShapeDtypeStruct((B, S, LANES), _f32),
        jax.ShapeDtypeStruct((B, S, MIX_WIDTH), _bf16),
    )
    q_spec = pl.BlockSpec((1, N_ATTN_HEADS, HEAD_DIM, tm), lambda t: (done(t) // nt, 0, 0, done(t) % nt))
    k_spec = pl.BlockSpec((1, N_KV_HEADS, tm, LANES), head4)
    t_spec = pl.BlockSpec((1, tm, LANES), tok3)
    vt_spec = pl.BlockSpec((1, N_KV_HEADS, tm // LANES, LANES, LANES),
                           lambda t: (done(t) // nt, 0, done(t) % nt, 0, 0))
    return pl.pallas_call(
        functools.partial(_proj_kernel, nt),
        grid=(n_tiles + 2,),
        in_specs=[
            pl.BlockSpec((tm, D_MODEL), lambda t: (jnp.minimum(t, n_tiles - 1), 0)),
            pl.BlockSpec((1, D_MODEL), const2),
            pl.BlockSpec((D_MODEL, _IN_COLS_PACKED), const2),
            pl.BlockSpec((1, ATTN_WIDTH), const2),
            pl.BlockSpec((1, KV_WIDTH), const2),
            pl.BlockSpec((1, KV_WIDTH), const2),
            pl.BlockSpec((tm, LANES), lambda t: (done(t) % nt, 0)),
            pl.BlockSpec((tm, LANES), lambda t: (done(t) % nt, 0)),
            pl.BlockSpec((MXU_DIM, MXU_DIM), const2),
            pl.BlockSpec((1, MIX_WIDTH), const2),
            pl.BlockSpec((N_MIX_GROUPS // 2, CHUNK, 2 * CHUNK), const3),
            pl.BlockSpec((CHUNK, MIX_WIDTH), const2),
            pl.BlockSpec((1, MIX_WIDTH), const2),
        ],
        out_specs=(q_spec, q_spec, t_spec, t_spec, k_spec, vt_spec, k_spec, vt_spec, t_spec,
                   pl.BlockSpec((1, tm, MIX_WIDTH), tok3)),
        out_shape=out_shape,
        scratch_shapes=[pltpu.VMEM((tm, D_MODEL), _bf16), pltpu.VMEM((tm, D_MODEL), _bf16),
                        pltpu.VMEM((tm, _IN_COLS_PACKED), _f32), pltpu.VMEM((tm, _IN_COLS_PACKED), _f32)],
        compiler_params=pltpu.CompilerParams(
            dimension_semantics=("arbitrary",), vmem_limit_bytes=VMEM_LIMIT),
        name="proj",
    )(x.reshape(B * S, D_MODEL), gmix, w_in_p, gq, gksl, gkwn, cos, sin, bd, gsgu, spw, spb, gomix)


def _compress_kernel(kc_ref, vc_ref, pe_ref, w1_ref, w2_ref, gk_ref, kcmp_ref, vcmp_ref):
    n_rows = kc_ref.shape[1]
    zero_half = jnp.zeros((n_rows, HEAD_DIM), _f32)
    for t, (src_ref, out_ref) in enumerate(((kc_ref, kcmp_ref), (vc_ref, vcmp_ref))):
        g = src_ref[0].astype(_f32)
        first = _dot((g + pe_ref[t, 0:1]).astype(_bf16), w1_ref[t, 0])
        second = _dot((g + pe_ref[t, 1:2]).astype(_bf16), w1_ref[t, 1])
        hid = jax.nn.gelu(first + pltpu.roll(second, n_rows - 1, 0)).astype(_bf16)
        heads = [_dot(hid[:, hh * CMP_HIDDEN:(hh + 1) * CMP_HIDDEN], w2_ref[t]) for hh in range(N_KV_HEADS)]
        if t == 0:
            heads = [c * lax.rsqrt(jnp.mean(c * c, axis=-1, keepdims=True) + EPS) * gk_ref[...] for c in heads]
            for hh in range(N_KV_HEADS):
                out_ref[0, hh] = jnp.concatenate([heads[hh], zero_half], axis=1).astype(_bf16)
        else:
            for hh in range(N_KV_HEADS):
                vt = jnp.concatenate([heads[hh], zero_half], axis=1).T
                out_ref[0, hh] = vt.astype(_bf16)


def _compress_call(kc_g, vc_g, pe_rows, w1x, w2, gk0):
    B, n_rows, width = kc_g.shape
    return pl.pallas_call(
        _compress_kernel,
        grid=(B,),
        in_specs=[
            pl.BlockSpec((1, n_rows, width), lambda b: (b, 0, 0)),
            pl.BlockSpec((1, n_rows, width), lambda b: (b, 0, 0)),
            pl.BlockSpec((2, 2, width), lambda b: (0, 0, 0)),
            pl.BlockSpec((2, 2, width, N_KV_HEADS * CMP_HIDDEN), lambda b: (0, 0, 0, 0)),
            pl.BlockSpec((2, CMP_HIDDEN, HEAD_DIM), lambda b: (0, 0, 0)),
            pl.BlockSpec((1, HEAD_DIM), lambda b: (0, 0)),
        ],
        out_specs=(pl.BlockSpec((1, N_KV_HEADS, n_rows, LANES), lambda b: (b, 0, 0, 0)),
                   pl.BlockSpec((1, N_KV_HEADS, n_rows, LANES), lambda b: (b, 0, 0, 0))),
        out_shape=(jax.ShapeDtypeStruct((B, N_KV_HEADS, n_rows, LANES), _bf16),
                   jax.ShapeDtypeStruct((B, N_KV_HEADS, n_rows, LANES), _bf16)),
        compiler_params=pltpu.CompilerParams(
            dimension_semantics=("parallel",), vmem_limit_bytes=VMEM_LIMIT),
        name="compress",
    )(kc_g, vc_g, pe_rows, w1x, w2, gk0)


def _interleave(streams):
    live = list(streams)
    while live:
        for g in list(live):
            try:
                next(g)
            except StopIteration:
                live.remove(g)


def _delayed(stream, units):
    for _ in range(units):
        yield
    yield from stream


def _attn_kernel(qn_ref, qr_ref, kcmp_ref, vcmpt_ref, ksl_ref, vslt_ref, kwn_ref, vwnt_ref, gate_ref,
                 ovt_ref, mask_ref, gout_ref, o_ref):
    refs = (qn_ref, qr_ref, kcmp_ref, vcmpt_ref, ksl_ref, vslt_ref, kwn_ref, vwnt_ref, gate_ref,
            ovt_ref, mask_ref, gout_ref, o_ref)
    S = ksl_ref.shape[2]

    def step_body(c):
        tiles = [_attn_tile(sub, c, *refs) for sub in range(ATTN_SUBTILES)]
        _interleave([_delayed(stream, sub * TILE_SKEW) for sub, (streams, _) in enumerate(tiles)
                     for stream in streams])
        for _, finish in tiles:
            finish()

    cls = (pl.program_id(1) * ATTN_SUBTILES * ATTN_Q_TILE) // SEL_CHUNK
    for c in range(S // SEL_CHUNK):
        pl.when(cls == c)(functools.partial(step_body, c))


def _attn_tile(sub, c, qn_ref, qr_ref, kcmp_ref, vcmpt_ref, ksl_ref, vslt_ref, kwn_ref, vwnt_ref, gate_ref,
               ovt_ref, mask_ref, gout_ref, o_ref):
    tq = ATTN_Q_TILE
    sub_cols = slice(sub * tq, (sub + 1) * tq)
    rows = GQA_GROUP * tq
    S = ksl_ref.shape[2]
    n_cmp_pad = kcmp_ref.shape[2]
    n_sel = S // SEL_BLOCK
    win_tiles = WINDOW // tq
    unit_tiles = ATTN_UNIT // tq
    i = pl.program_id(1) * ATTN_SUBTILES + sub
    q0 = i * tq

    qpos = q0 + lax.broadcasted_iota(jnp.int32, (1, rows), 1) % tq
    blk = lax.broadcasted_iota(jnp.int32, (n_sel, tq), 0)
    tpos = q0 + lax.broadcasted_iota(jnp.int32, (n_sel, tq), 1)
    gates_t = gate_ref[0, sub_cols, :].T
    zero_rows = jnp.zeros((HEAD_DIM, rows), _bf16)

    def stacked_heads_t(ref, h):
        return jnp.concatenate([ref[0, GQA_GROUP * h + g, :, sub_cols] for g in range(GQA_GROUP)], axis=1)

    def tile_mask(key_tile, low_edge=False):
        kind = jnp.where(key_tile < i, MASK_FULL, jnp.where(key_tile == i, MASK_DIAG, MASK_NONE))
        if low_edge:
            kind = jnp.where(i >= win_tiles, MASK_LOW, kind)
        return mask_ref[kind]

    def attend(out, h, k_ref, vt_ref, q_t, first_tile, tile_masks):
        n_tiles = len(tile_masks)
        scores, mx = [], None
        for u0 in range(0, n_tiles, unit_tiles):
            nt = min(unit_tiles, n_tiles - u0)
            if isinstance(first_tile, int):
                keys = slice((first_tile + u0) * tq, (first_tile + u0 + nt) * tq)
            else:
                keys = pl.ds(pl.multiple_of((first_tile + u0) * tq, tq), nt * tq)
            sc = _dot(k_ref[0, h, keys, :], q_t)
            if any(tile_masks[u0 + t] is not None for t in range(nt)):
                sc = jnp.concatenate(
                    [sc[t * tq:(t + 1) * tq] if tile_masks[u0 + t] is None
                     else sc[t * tq:(t + 1) * tq] + tile_masks[u0 + t]() for t in range(nt)], axis=0)
            scores.append(sc)
            cm = jnp.max(sc, axis=0, keepdims=True)
            mx = cm if mx is None else jnp.maximum(mx, cm)
            yield
        probs = []
        for sc in scores:
            probs.append(jnp.exp2(sc - mx).astype(_bf16))
            yield
        vt = jnp.concatenate([vt_ref[0, h, first_tile + t] for t in range(n_tiles)], axis=1)
        acc = _dot(vt, jnp.concatenate(probs, axis=0))
        out[h] = acc[0:HEAD_DIM] / acc[HEAD_DIM:HEAD_DIM + 1]
        yield

    def window_stream(out, h, c):
        first_tile = jnp.maximum(i - win_tiles, 0)
        q_t = jnp.concatenate([stacked_heads_t(qr_ref, h), zero_rows], axis=0)
        if c * SEL_CHUNK >= WINDOW:
            masks = ([lambda: mask_ref[MASK_LOW]] + [None] * (win_tiles - 1) + [lambda: mask_ref[MASK_DIAG]])
        else:
            masks = [functools.partial(tile_mask, first_tile + t, low_edge=(t == 0))
                     for t in range(win_tiles + 1)]
        yield from attend(out, h, kwn_ref, vwnt_ref, q_t, first_tile, masks)

    def compressed_and_selected_stream(o_cmps, o_sels, h, c):
        qn = stacked_heads_t(qn_ref, h)
        qr = stacked_heads_t(qr_ref, h)

        s = _dot(kcmp_ref[0, h], jnp.concatenate([qn, zero_rows], axis=0))
        n_idx = lax.broadcasted_iota(jnp.int32, (n_cmp_pad, 1), 0)
        valid_c = n_idx * CMP_STRIDE + (CMP_BLOCK - 1) <= qpos
        s = jnp.where(valid_c, s, NEG_BIG)
        m = jnp.max(s, axis=0, keepdims=True)
        e = jnp.where(valid_c, jnp.exp2(s - m), 0.0)
        p_c = e / jnp.maximum(jnp.sum(e, axis=0, keepdims=True), 1e-20)
        o_cmps[h] = _dot(vcmpt_ref[0, h], p_c.astype(_bf16))[0:HEAD_DIM]
        yield

        eligible = blk * SEL_BLOCK <= tpos
        if SEL_CHUNK * (c + 1) <= SEL_TOPN * SEL_BLOCK:
            chosen = eligible
        else:
            p_sum = p_c[:, 0:tq] + p_c[:, tq:2 * tq] + p_c[:, 2 * tq:3 * tq] + p_c[:, 3 * tq:4 * tq]
            p_hi = p_sum.astype(_bf16)
            p_lo = (p_sum - p_hi.astype(_f32)).astype(_bf16)
            imp = _dot(ovt_ref[...], p_hi) + _dot(ovt_ref[...], p_lo)
            cur = tpos // SEL_BLOCK
            forced = (blk == 0) | (blk == cur) | (blk == cur - 1)
            score = jnp.where(forced, FORCE_SCORE, jnp.where(eligible, imp, -jnp.inf))
            rank = jnp.zeros((n_sel, tq), _f32)
            for jp in range(n_sel):
                other = score[jp:jp + 1, :]
                tie = jnp.where(blk > jp, 1.0, 0.0)
                rank = rank + jnp.where(other > score, 1.0, jnp.where(other == score, tie, 0.0))
            chosen = (rank < float(SEL_TOPN)) & eligible
        sel_bias = jnp.where(chosen, 0.0, NEG_BIG).astype(_bf16)
        yield

        q_t = jnp.concatenate([qr, jnp.concatenate([sel_bias] * GQA_GROUP, axis=1),
                               jnp.zeros((LANES - HEAD_DIM - n_sel, rows), _bf16)], axis=0)
        n_tiles = (SEL_CHUNK // tq) * (c + 1)
        first_edge = n_tiles - SEL_CHUNK // tq
        masks = [None if t < first_edge else functools.partial(tile_mask, t) for t in range(n_tiles)]
        yield from attend(o_sels, h, ksl_ref, vslt_ref, q_t, 0, masks)

    o_wins, o_cmps, o_sels = {}, {}, {}
    streams = [
        window_stream(o_wins, 0, c),
        compressed_and_selected_stream(o_cmps, o_sels, 0, c),
        _delayed(window_stream(o_wins, 1, c), STREAM_SKEW),
        _delayed(compressed_and_selected_stream(o_cmps, o_sels, 1, c), STREAM_SKEW),
    ]

    def finish():
        slabs = []
        for h in range(N_KV_HEADS):
            for g in range(GQA_GROUP):
                hq = GQA_GROUP * h + g
                cols = slice(g * tq, (g + 1) * tq)
                slabs.append(gates_t[hq:hq + 1] * o_cmps[h][:, cols]
                             + gates_t[N_ATTN_HEADS + hq:N_ATTN_HEADS + hq + 1] * o_sels[h][:, cols]
                             + gates_t[2 * N_ATTN_HEADS + hq:2 * N_ATTN_HEADS + hq + 1] * o_wins[h][:, cols])
        ot = jnp.concatenate(slabs, axis=0)
        ms = jnp.mean(ot * ot, axis=0, keepdims=True)
        o_ref[0, sub_cols, :] = (ot * lax.rsqrt(ms + EPS) * gout_ref[...]).T.astype(_bf16)

    return streams, finish


def _attn_call(qn_t, qr_t, kcmp, vcmp_t, ksl, vsl_t, kwn, vwn_t, gates, ovt, masks, gout_attn):
    B, _, _, S = qn_t.shape
    tq = ATTN_Q_TILE
    assert tq == LANES and S % SEL_CHUNK == 0 and SEL_CHUNK % tq == 0 and WINDOW % tq == 0
    assert ATTN_UNIT % tq == 0 and kcmp.shape[2] == LANES
    step = tq * ATTN_SUBTILES
    assert SEL_CHUNK % step == 0
    n_cmp_pad = kcmp.shape[2]
    n_sel = S // SEL_BLOCK
    rows = GQA_GROUP * tq
    per_b4 = lambda b, i: (b, 0, 0, 0)
    const2 = lambda b, i: (0, 0)
    vt_spec = pl.BlockSpec((1, N_KV_HEADS, S // LANES, LANES, LANES), lambda b, i: (b, 0, 0, 0, 0))
    return pl.pallas_call(
        _attn_kernel,
        grid=(B, S // step),
        in_specs=[
            pl.BlockSpec((1, N_ATTN_HEADS, HEAD_DIM, step), lambda b, i: (b, 0, 0, i)),
            pl.BlockSpec((1, N_ATTN_HEADS, HEAD_DIM, step), lambda b, i: (b, 0, 0, i)),
            pl.BlockSpec((1, N_KV_HEADS, n_cmp_pad, LANES), per_b4),
            pl.BlockSpec((1, N_KV_HEADS, n_cmp_pad, LANES), per_b4),
            pl.BlockSpec((1, N_KV_HEADS, S, LANES), per_b4),
            vt_spec,
            pl.BlockSpec((1, N_KV_HEADS, S, LANES), per_b4),
            vt_spec,
            pl.BlockSpec((1, step, LANES), lambda b, i: (b, i, 0)),
            pl.BlockSpec((n_sel, n_cmp_pad), const2),
            pl.BlockSpec((4, tq, rows), lambda b, i: (0, 0, 0)),
            pl.BlockSpec((ATTN_WIDTH, tq), const2),
        ],
        out_specs=pl.BlockSpec((1, step, ATTN_WIDTH), lambda b, i: (b, i, 0)),
        out_shape=jax.ShapeDtypeStruct((B, S, ATTN_WIDTH), _bf16),
        compiler_params=pltpu.CompilerParams(
            dimension_semantics=("parallel", "arbitrary"), vmem_limit_bytes=VMEM_LIMIT),
        name="attn",
    )(qn_t, qr_t, kcmp, vcmp_t, ksl, vsl_t, kwn, vwn_t, gates, ovt, masks, gout_attn)


def _ffn_kernel(x_ref, oa_ref, om_ref, wo_ref, gffn_ref, w1_ref, w2_ref, out_ref, act_scr):
    x2 = (x_ref[...] + _dot(oa_ref[...], wo_ref[0:ATTN_WIDTH, :])
          + _dot(om_ref[...], wo_ref[ATTN_WIDTH:ATTN_WIDTH + MIX_WIDTH, :]))
    ms = jnp.mean(x2 * x2, axis=-1, keepdims=True)
    h = (x2 * lax.rsqrt(ms + EPS) * gffn_ref[...]).astype(_bf16)
    for c in range(D_FF // FF_CHUNK):
        cols = slice(c * FF_CHUNK, (c + 1) * FF_CHUNK)
        a = jnp.maximum(_dot(h, w1_ref[:, cols]), 0.0)
        act_scr[:, cols] = (a * a).astype(_bf16)
    out_ref[...] = x2 + _dot(act_scr[...], w2_ref[...])


def _ffn_call(x2d, oa, om, wo, gffn, w1, w2):
    T = x2d.shape[0]
    tm = FFN_TILE
    tok = lambda i: (i, 0)
    const = lambda i: (0, 0)
    return pl.pallas_call(
        _ffn_kernel,
        grid=(T // tm,),
        in_specs=[
            pl.BlockSpec((tm, D_MODEL), tok),
            pl.BlockSpec((tm, ATTN_WIDTH), tok),
            pl.BlockSpec((tm, MIX_WIDTH), tok),
            pl.BlockSpec((D_MODEL, D_MODEL), const),
            pl.BlockSpec((1, D_MODEL), const),
            pl.BlockSpec((D_MODEL, D_FF), const),
            pl.BlockSpec((D_FF, D_MODEL), const),
        ],
        out_specs=pl.BlockSpec((tm, D_MODEL), tok),
        out_shape=jax.ShapeDtypeStruct((T, D_MODEL), _f32),
        scratch_shapes=[pltpu.VMEM((tm, D_FF), _bf16)],
        compiler_params=pltpu.CompilerParams(
            dimension_semantics=("parallel",), vmem_limit_bytes=VMEM_LIMIT),
        name="ffn",
    )(x2d, oa, om, wo, gffn, w1, w2)


def _constants(S):
    half = HEAD_DIM // 2
    inv = ROPE_THETA ** (-jnp.arange(half, dtype=_f32) / half)
    ang = jnp.arange(S, dtype=_f32)[:, None] * inv[None, :]
    cos = jnp.cos(ang)
    sin = jnp.sin(ang)
    cos2 = jnp.concatenate([cos, cos, cos, cos], axis=1)
    sin2 = jnp.concatenate([-sin, sin, -sin, sin], axis=1)
    mlane = np.arange(MXU_DIM)
    bd = (mlane[:, None] // HEAD_DIM == mlane[None, :] // HEAD_DIM).astype(np.float32) / HEAD_DIM
    n_cmp = (S - CMP_BLOCK) // CMP_STRIDE + 1
    n_sel = S // SEL_BLOCK
    n_cmp_pad = S // CMP_STRIDE
    cmp_start = np.arange(n_cmp)[:, None] * CMP_STRIDE
    sel_start = np.arange(n_sel)[None, :] * SEL_BLOCK
    overlap = np.clip(np.minimum(cmp_start + CMP_BLOCK, sel_start + SEL_BLOCK)
                      - np.maximum(cmp_start, sel_start), 0, None).astype(np.float32) / CMP_BLOCK
    ovt = np.zeros((n_sel, n_cmp_pad), np.float32)
    ovt[:, :n_cmp] = overlap.T
    qq = (np.arange(GQA_GROUP * ATTN_Q_TILE) % ATTN_Q_TILE)[None, :]
    kk = np.arange(ATTN_Q_TILE)[:, None]
    masks = np.zeros((4, ATTN_Q_TILE, GQA_GROUP * ATTN_Q_TILE), np.float32)
    masks[MASK_DIAG] = np.where(kk <= qq, 0.0, NEG_BIG)
    masks[MASK_LOW] = np.where(kk > qq, 0.0, NEG_BIG)
    masks[MASK_NONE] = NEG_BIG
    return cos2, sin2, jnp.asarray(bd, _bf16), jnp.asarray(ovt, _bf16), jnp.asarray(masks, _f32)


def _pack_w_in(w):
    gate0 = ATTN_WIDTH + 6 * KV_WIDTH
    n_g = N_GATES * N_ATTN_HEADS
    gates = w[:, gate0:gate0 + n_g].reshape(D_MODEL, N_ATTN_HEADS, N_GATES)
    gates = jnp.transpose(gates, (0, 2, 1)).reshape(D_MODEL, n_g)
    pad = jnp.zeros((D_MODEL, _C_KV + _KV_GROUP - gate0 - n_g), w.dtype)
    return jnp.concatenate([w[:, :gate0], gates, pad, w[:, gate0 + n_g:]], axis=1).astype(_bf16)


def _expand_cmp_w1(w1):
    w = w1.reshape(2, CMP_STRIDE, HEAD_DIM, CMP_HIDDEN)
    z = jnp.zeros_like(w)
    h0 = jnp.concatenate([w, z], axis=-1)
    h1 = jnp.concatenate([z, w], axis=-1)
    both = jnp.stack([h0, h1], axis=2)
    return both.reshape(2, CMP_STRIDE * KV_WIDTH, N_KV_HEADS * CMP_HIDDEN).astype(_bf16)


def _expand_pe(pe):
    p = pe.reshape(2, CMP_STRIDE, 1, HEAD_DIM)
    return jnp.broadcast_to(p, (2, CMP_STRIDE, N_KV_HEADS, HEAD_DIM)).reshape(2, CMP_STRIDE * KV_WIDTH)


def _forward(x, g_mix_norm, w_in, g_q, g_k, cmp_pe, cmp_w1, cmp_w2, g_sgu, sp_w, sp_b, g_out, w_out,
             g_ffn_norm, w_ff1, w_ff2):
    B, S, _ = x.shape
    l = 0
    scale = float(HEAD_DIM ** -0.5 * np.log2(np.e))
    cos2, sin2, bd, ovt, masks = _constants(S)

    w_in_p = _pack_w_in(w_in[l])
    gq = (jnp.tile(g_q[l], N_ATTN_HEADS) * scale)[None, :]
    gksl = jnp.tile(g_k[l, 1], N_KV_HEADS)[None, :]
    gkwn = jnp.tile(g_k[l, 2], N_KV_HEADS)[None, :]
    gsgu = g_sgu[l].reshape(1, MIX_WIDTH)
    spw = sp_w[l].reshape(N_MIX_GROUPS // 2, 2, CHUNK, CHUNK)
    spw = jnp.transpose(spw, (0, 2, 1, 3)).reshape(N_MIX_GROUPS // 2, CHUNK, 2 * CHUNK).astype(_bf16)
    spb = jnp.repeat(sp_b[l].T, HEAD_DIM, axis=1)
    gout = g_out[l]

    qn, qr, kc, vc, ksl, vsl, kwn, vwn, gates, omix = _proj_call(
        x, g_mix_norm[l][None, :], w_in_p, gq, gksl, gkwn, cos2, sin2, bd, gsgu, spw, spb,
        gout[None, ATTN_WIDTH:])

    n_rows = S // CMP_STRIDE
    pe_rows = jnp.stack([_expand_pe(cmp_pe[l, 0]), _expand_pe(cmp_pe[l, 1])])
    w1x = jnp.stack([_expand_cmp_w1(cmp_w1[l, 0]), _expand_cmp_w1(cmp_w1[l, 1])])
    kcmp, vcmp = _compress_call(
        kc.reshape(B, n_rows, CMP_STRIDE * KV_WIDTH), vc.reshape(B, n_rows, CMP_STRIDE * KV_WIDTH),
        pe_rows, w1x, cmp_w2[l].astype(_bf16), g_k[l, 0][None, :])

    oattn = _attn_call(qn, qr, kcmp, vcmp, ksl, vsl, kwn, vwn, gates, ovt, masks,
                       jnp.broadcast_to(gout[:ATTN_WIDTH, None], (ATTN_WIDTH, ATTN_Q_TILE)))

    out = _ffn_call(x.reshape(B * S, D_MODEL), oattn.reshape(B * S, ATTN_WIDTH),
                    omix.reshape(B * S, MIX_WIDTH), w_out[l].astype(_bf16), g_ffn_norm[l][None, :],
                    w_ff1[l].astype(_bf16), w_ff2[l].astype(_bf16))
    return out.reshape(B, S, D_MODEL), (qn, qr, kc, vc, ksl, vsl, kwn, vwn, gates, omix, kcmp, vcmp, oattn)


def kernel(x, g_mix_norm, w_in, g_q, g_k, cmp_pe, cmp_w1, cmp_w2, g_sgu, sp_w, sp_b, g_out, w_out,
           g_ffn_norm, w_ff1, w_ff2):
    return _forward(x, g_mix_norm, w_in, g_q, g_k, cmp_pe, cmp_w1, cmp_w2, g_sgu, sp_w, sp_b, g_out, w_out,
                    g_ffn_norm, w_ff1, w_ff2)[0]
```

```python
import functools

import numpy as np
import jax
import jax.numpy as jnp
from jax import lax
from jax.experimental import pallas as pl
from jax.experimental.pallas import tpu as pltpu

D_MODEL = 1024
HEAD_DIM = 64
N_ATTN_HEADS = 8
N_MIX_GROUPS = 8
GQA_GROUP = 4
N_KV_HEADS = 2
ATTN_WIDTH = 512
MIX_WIDTH = 512
KV_WIDTH = 128
N_GATES = 3
CMP_BLOCK = 32
CMP_STRIDE = 16
CMP_HIDDEN = 256
SEL_BLOCK = 64
SEL_TOPN = 16
WINDOW = 512
CHUNK = 128
D_FF = 4 * D_MODEL
ROPE_THETA = 10000.0
EPS = 1e-6
FORCE_SCORE = 1e9

LANES = 128
NEG_BIG = -1e30
PROJ_TILE = 256
PROJ_UNIT = 512
ATTN_Q_TILE = 128
ATTN_SUBTILES = 2
TILE_SKEW = 4
SEL_CHUNK = 512
ATTN_UNIT = 256
STREAM_SKEW = 1
MASK_FULL, MASK_DIAG, MASK_LOW, MASK_NONE = 0, 1, 2, 3
FFN_TILE = 256
FF_CHUNK = 1024
VMEM_LIMIT = 56 * 1024 * 1024

_C_Q = 0
_C_KV = 512
_KV_KC, _KV_VC, _KV_KSL, _KV_VSL, _KV_KWN, _KV_VWN, _KV_GATE = (128 * n for n in range(7))
_KV_GROUP = 1024
_C_ZU = 1536
_C_ZV = 2048
_IN_COLS_PACKED = 2560
MXU_DIM = 256

_bf16 = jnp.bfloat16
_f32 = jnp.float32


def _dot(a, b):
    return jnp.dot(a, b, preferred_element_type=_f32)


def _group_mean(sq, bd_ref):
    cw = min(sq.shape[1], MXU_DIM)
    bd = bd_ref[0:cw, 0:cw]
    parts = [_dot(sq[:, c * cw:(c + 1) * cw].astype(_bf16), bd) for c in range(sq.shape[1] // cw)]
    return parts[0] if len(parts) == 1 else jnp.concatenate(parts, axis=1)


def _swap_halves(v):
    n = v.shape[1]
    lane = lax.broadcasted_iota(jnp.int32, v.shape, 1)
    return jnp.where((lane % HEAD_DIM) < HEAD_DIM // 2, pltpu.roll(v, n - HEAD_DIM // 2, 1),
                     pltpu.roll(v, HEAD_DIM // 2, 1))


def _proj_kernel(tiles_per_seq, x_ref, gmix_ref, w_ref, gq_ref, gksl_ref, gkwn_ref, cos_ref, sin_ref, bd_ref,
                 gsgu_ref, spw_ref, spb_ref, gomix_ref,
                 qn_ref, qr_ref, kc_ref, vc_ref, ksl_ref, vsl_ref, kwn_ref, vwn_ref, gate_ref, omix_ref,
                 h_even, h_odd, z_even, z_odd, regroup_scr):
    t = pl.program_id(0)

    @pl.when(t == 0)
    def _():
        h_odd[...] = jnp.zeros(h_odd.shape, _bf16)
        z_odd[...] = jnp.zeros(z_odd.shape, _f32)

    pl.when(t % 2 == 0)(functools.partial(
        _proj_step, tiles_per_seq, x_ref, gmix_ref, w_ref, gq_ref, gksl_ref, gkwn_ref, cos_ref, sin_ref, bd_ref,
        gsgu_ref, spw_ref, spb_ref, gomix_ref, qn_ref, qr_ref, kc_ref, vc_ref, ksl_ref, vsl_ref, kwn_ref,
        vwn_ref, gate_ref, omix_ref, h_even, h_odd, z_even, z_odd, regroup_scr))
    pl.when(t % 2 == 1)(functools.partial(
        _proj_step, tiles_per_seq, x_ref, gmix_ref, w_ref, gq_ref, gksl_ref, gkwn_ref, cos_ref, sin_ref, bd_ref,
        gsgu_ref, spw_ref, spb_ref, gomix_ref, qn_ref, qr_ref, kc_ref, vc_ref, ksl_ref, vsl_ref, kwn_ref,
        vwn_ref, gate_ref, omix_ref, h_odd, h_even, z_odd, z_even, regroup_scr))


def _proj_step(tiles_per_seq, x_ref, gmix_ref, w_ref, gq_ref, gksl_ref, gkwn_ref, cos_ref, sin_ref, bd_ref,
               gsgu_ref, spw_ref, spb_ref, gomix_ref,
               qn_ref, qr_ref, kc_ref, vc_ref, ksl_ref, vsl_ref, kwn_ref, vwn_ref, gate_ref, omix_ref,
               h_next, h_prev, z_next, z_prev, regroup_scr):
    tm = x_ref.shape[0]
    i = jnp.maximum(pl.program_id(0) - 2, 0) % tiles_per_seq
    cos = cos_ref[...]
    sin = sin_ref[...]

    def project_stream():
        h = h_prev[...]
        for lo in range(0, _IN_COLS_PACKED, PROJ_UNIT):
            z_next[:, lo:lo + PROJ_UNIT] = _dot(h, w_ref[:, lo:lo + PROJ_UNIT])
            yield

    def prenorm_stream():
        x = x_ref[...]
        ms = jnp.mean(x * x, axis=-1, keepdims=True)
        h_next[...] = (x * lax.rsqrt(ms + EPS) * gmix_ref[...]).astype(_bf16)
        yield

    def query_stream():
        zq = z_prev[:, _C_Q:_C_Q + ATTN_WIDTH]
        qn = zq * lax.rsqrt(_group_mean(zq * zq, bd_ref) + EPS) * gq_ref[...]
        yield
        cos4 = jnp.concatenate([cos] * 4, axis=1)
        sin4 = jnp.concatenate([sin] * 4, axis=1)
        qr = qn * cos4 + _swap_halves(qn) * sin4
        yield
        qn_t, qr_t = qn.T, qr.T
        for hq in range(N_ATTN_HEADS):
            sl = slice(hq * HEAD_DIM, (hq + 1) * HEAD_DIM)
            qn_ref[0, hq] = qn_t[sl].astype(_bf16)
            qr_ref[0, hq] = qr_t[sl].astype(_bf16)

    def kv_stream():
        def segment(off):
            return z_prev[:, _C_KV + off:_C_KV + off + KV_WIDTH]

        for n, (off, out_ref) in enumerate(((_KV_KC, kc_ref), (_KV_VC, vc_ref))):
            regroup_scr[n] = segment(off)
            for tok in range(CMP_STRIDE):
                every_16th = regroup_scr[n, pl.ds(tok, tm // CMP_STRIDE, stride=CMP_STRIDE), :]
                out_ref[0, :, tok * KV_WIDTH:(tok + 1) * KV_WIDTH] = every_16th.astype(_bf16)
        gate_ref[0] = jax.nn.sigmoid(segment(_KV_GATE))

        ones_block = jnp.where(lax.broadcasted_iota(jnp.int32, (HEAD_DIM, LANES), 0) == 0, 1.0, 0.0)
        for off, out_ref in ((_KV_VSL, vsl_ref), (_KV_VWN, vwn_ref)):
            zv2 = segment(off)
            for tt in range(tm // LANES):
                zt = zv2[tt * LANES:(tt + 1) * LANES, :].T
                for hh in range(N_KV_HEADS):
                    out_ref[0, hh, tt] = jnp.concatenate(
                        [zt[hh * HEAD_DIM:(hh + 1) * HEAD_DIM], ones_block], axis=0).astype(_bf16)
            yield

        lane = lax.broadcasted_iota(jnp.int32, (tm, LANES), 1)
        pos = i * tm + lax.broadcasted_iota(jnp.int32, (tm, LANES), 0)
        blk_onehot = jnp.where(lane - HEAD_DIM == pos // SEL_BLOCK, 1.0, 0.0)
        for off, g_ref, out_ref, extra in ((_KV_KSL, gksl_ref, ksl_ref, blk_onehot),
                                           (_KV_KWN, gkwn_ref, kwn_ref, jnp.zeros((tm, LANES), _f32))):
            zk = segment(off)
            kn = zk * lax.rsqrt(_group_mean(zk * zk, bd_ref) + EPS) * g_ref[...]
            kr = kn * cos + _swap_halves(kn) * sin
            out_ref[0, 0] = jnp.where(lane < HEAD_DIM, kr, extra).astype(_bf16)
            out_ref[0, 1] = jnp.where(lane < HEAD_DIM, pltpu.roll(kr, HEAD_DIM, 1), extra).astype(_bf16)
            yield

    row = lax.broadcasted_iota(jnp.int32, (CHUNK, 2 * CHUNK), 0)
    colw = lax.broadcasted_iota(jnp.int32, (CHUNK, 2 * CHUNK), 1) % CHUNK
    causal_w = colw <= row
    lane_c = lax.broadcasted_iota(jnp.int32, (CHUNK, LANES), 1)

    def gmlp_stream(c):
        rows = slice(c * CHUNK, (c + 1) * CHUNK)
        zu = jax.nn.gelu(z_prev[rows, _C_ZU:_C_ZU + MIX_WIDTH])
        yield
        zv = jax.nn.gelu(z_prev[rows, _C_ZV:_C_ZV + MIX_WIDTH])
        yield
        vn = zv * lax.rsqrt(_group_mean(zv * zv, bd_ref) + EPS) * gsgu_ref[...]
        yield
        sv_parts = []
        for p in range(N_MIX_GROUPS // 2):
            vp = vn[:, p * LANES:(p + 1) * LANES]
            rhs = jnp.concatenate([jnp.where(lane_c < HEAD_DIM, vp, 0.0),
                                   jnp.where(lane_c < HEAD_DIM, 0.0, vp)], axis=0).astype(_bf16)
            w_pair = jnp.where(causal_w, spw_ref[p], jnp.zeros((), _bf16))
            sv_parts.append(_dot(w_pair, rhs))
        yield
        sv = jnp.concatenate(sv_parts, axis=1) + spb_ref[...]
        om = zu * sv
        oms = jnp.mean(om * om, axis=-1, keepdims=True)
        omix_ref[0, rows] = (om * lax.rsqrt(oms + EPS) * gomix_ref[...]).astype(_bf16)

    _interleave([project_stream(), query_stream(), kv_stream()] + [gmlp_stream(c) for c in range(tm // CHUNK)]
                + [prenorm_stream()])


def _proj_call(x, gmix, w_in_p, gq, gksl, gkwn, cos, sin, bd, gsgu, spw, spb, gomix):
    B, S, _ = x.shape
    tm = PROJ_TILE
    nt = S // tm
    n_tiles = B * nt
    const2 = lambda t: (0, 0)
    const3 = lambda t: (0, 0, 0)

    def done(t):
        return jnp.maximum(t - 2, 0)

    tok3 = lambda t: (done(t) // nt, done(t) % nt, 0)
    head4 = lambda t: (done(t) // nt, 0, done(t) % nt, 0)
    out_shape = (
        jax.ShapeDtypeStruct((B, N_ATTN_HEADS, HEAD_DIM, S), _bf16),
        jax.ShapeDtypeStruct((B, N_ATTN_HEADS, HEAD_DIM, S), _bf16),
        jax.ShapeDtypeStruct((B, S // CMP_STRIDE, CMP_STRIDE * KV_WIDTH), _bf16),
        jax.ShapeDtypeStruct((B, S // CMP_STRIDE, CMP_STRIDE * KV_WIDTH), _bf16),
        jax.ShapeDtypeStruct((B, N_KV_HEADS, S, LANES), _bf16),
        jax.ShapeDtypeStruct((B, N_KV_HEADS, S // LANES, LANES, LANES), _bf16),
        jax.ShapeDtypeStruct((B, N_KV_HEADS, S, LANES), _bf16),
        jax.ShapeDtypeStruct((B, N_KV_HEADS, S // LANES, LANES, LANES), _bf16),
        jax.ShapeDtypeStruct((B, S, LANES), _f32),
        jax.ShapeDtypeStruct((B, S, MIX_WIDTH), _bf16),
    )
    q_spec = pl.BlockSpec((1, N_ATTN_HEADS, HEAD_DIM, tm), lambda t: (done(t) // nt, 0, 0, done(t) % nt))
    k_spec = pl.BlockSpec((1, N_KV_HEADS, tm, LANES), head4)
    t_spec = pl.BlockSpec((1, tm, LANES), tok3)
    g_spec = pl.BlockSpec((1, tm // CMP_STRIDE, CMP_STRIDE * KV_WIDTH), tok3)
    vt_spec = pl.BlockSpec((1, N_KV_HEADS, tm // LANES, LANES, LANES),
                           lambda t: (done(t) // nt, 0, done(t) % nt, 0, 0))
    return pl.pallas_call(
        functools.partial(_proj_kernel, nt),
        grid=(n_tiles + 2,),
        in_specs=[
            pl.BlockSpec((tm, D_MODEL), lambda t: (jnp.minimum(t, n_tiles - 1), 0)),
            pl.BlockSpec((1, D_MODEL), const2),
            pl.BlockSpec((D_MODEL, _IN_COLS_PACKED), const2),
            pl.BlockSpec((1, ATTN_WIDTH), const2),
            pl.BlockSpec((1, KV_WIDTH), const2),
            pl.BlockSpec((1, KV_WIDTH), const2),
            pl.BlockSpec((tm, LANES), lambda t: (done(t) % nt, 0)),
            pl.BlockSpec((tm, LANES), lambda t: (done(t) % nt, 0)),
            pl.BlockSpec((MXU_DIM, MXU_DIM), const2),
            pl.BlockSpec((1, MIX_WIDTH), const2),
            pl.BlockSpec((N_MIX_GROUPS // 2, CHUNK, 2 * CHUNK), const3),
            pl.BlockSpec((CHUNK, MIX_WIDTH), const2),
            pl.BlockSpec((1, MIX_WIDTH), const2),
        ],
        out_specs=(q_spec, q_spec, g_spec, g_spec, k_spec, vt_spec, k_spec, vt_spec, t_spec,
                   pl.BlockSpec((1, tm, MIX_WIDTH), tok3)),
        out_shape=out_shape,
        scratch_shapes=[pltpu.VMEM((tm, D_MODEL), _bf16), pltpu.VMEM((tm, D_MODEL), _bf16),
                        pltpu.VMEM((tm, _IN_COLS_PACKED), _f32), pltpu.VMEM((tm, _IN_COLS_PACKED), _f32),
                        pltpu.VMEM((2, tm, KV_WIDTH), _f32)],
        compiler_params=pltpu.CompilerParams(
            dimension_semantics=("arbitrary",), vmem_limit_bytes=VMEM_LIMIT),
        name="proj",
    )(x.reshape(B * S, D_MODEL), gmix, w_in_p, gq, gksl, gkwn, cos, sin, bd, gsgu, spw, spb, gomix)


def _compress_kernel(kc_ref, vc_ref, pe_ref, w1_ref, w2_ref, gk_ref, kcmp_ref, vcmp_ref):
    n_rows = kc_ref.shape[1]
    zero_half = jnp.zeros((n_rows, HEAD_DIM), _f32)
    for t, (src_ref, out_ref) in enumerate(((kc_ref, kcmp_ref), (vc_ref, vcmp_ref))):
        g = src_ref[0].astype(_f32)
        first = _dot((g + pe_ref[t, 0:1]).astype(_bf16), w1_ref[t, 0])
        second = _dot((g + pe_ref[t, 1:2]).astype(_bf16), w1_ref[t, 1])
        hid = jax.nn.gelu(first + pltpu.roll(second, n_rows - 1, 0)).astype(_bf16)
        heads = [_dot(hid[:, hh * CMP_HIDDEN:(hh + 1) * CMP_HIDDEN], w2_ref[t]) for hh in range(N_KV_HEADS)]
        if t == 0:
            heads = [c * lax.rsqrt(jnp.mean(c * c, axis=-1, keepdims=True) + EPS) * gk_ref[...] for c in heads]
            for hh in range(N_KV_HEADS):
                out_ref[0, hh] = jnp.concatenate([heads[hh], zero_half], axis=1).astype(_bf16)
        else:
            for hh in range(N_KV_HEADS):
                vt = jnp.concatenate([heads[hh], zero_half], axis=1).T
                out_ref[0, hh] = vt.astype(_bf16)


def _compress_call(kc_g, vc_g, pe_rows, w1x, w2, gk0):
    B, n_rows, width = kc_g.shape
    return pl.pallas_call(
        _compress_kernel,
        grid=(B,),
        in_specs=[
            pl.BlockSpec((1, n_rows, width), lambda b: (b, 0, 0)),
            pl.BlockSpec((1, n_rows, width), lambda b: (b, 0, 0)),
            pl.BlockSpec((2, 2, width), lambda b: (0, 0, 0)),
            pl.BlockSpec((2, 2, width, N_KV_HEADS * CMP_HIDDEN), lambda b: (0, 0, 0, 0)),
            pl.BlockSpec((2, CMP_HIDDEN, HEAD_DIM), lambda b: (0, 0, 0)),
            pl.BlockSpec((1, HEAD_DIM), lambda b: (0, 0)),
        ],
        out_specs=(pl.BlockSpec((1, N_KV_HEADS, n_rows, LANES), lambda b: (b, 0, 0, 0)),
                   pl.BlockSpec((1, N_KV_HEADS, n_rows, LANES), lambda b: (b, 0, 0, 0))),
        out_shape=(jax.ShapeDtypeStruct((B, N_KV_HEADS, n_rows, LANES), _bf16),
                   jax.ShapeDtypeStruct((B, N_KV_HEADS, n_rows, LANES), _bf16)),
        compiler_params=pltpu.CompilerParams(
            dimension_semantics=("parallel",), vmem_limit_bytes=VMEM_LIMIT),
        name="compress",
    )(kc_g, vc_g, pe_rows, w1x, w2, gk0)


def _interleave(streams):
    live = list(streams)
    while live:
        for g in list(live):
            try:
                next(g)
            except StopIteration:
                live.remove(g)


def _delayed(stream, units):
    for _ in range(units):
        yield
    yield from stream


def _attn_kernel(qn_ref, qr_ref, kcmp_ref, vcmpt_ref, ksl_ref, vslt_ref, kwn_ref, vwnt_ref, gate_ref,
                 ovt_ref, mask_ref, gout_ref, o_ref):
    refs = (qn_ref, qr_ref, kcmp_ref, vcmpt_ref, ksl_ref, vslt_ref, kwn_ref, vwnt_ref, gate_ref,
            ovt_ref, mask_ref, gout_ref, o_ref)
    S = ksl_ref.shape[2]

    def step_body(c):
        tiles = [_attn_tile(sub, c, *refs) for sub in range(ATTN_SUBTILES)]
        _interleave([_delayed(stream, sub * TILE_SKEW) for sub, (streams, _) in enumerate(tiles)
                     for stream in streams])
        for _, finish in tiles:
            finish()

    cls = (pl.program_id(1) * ATTN_SUBTILES * ATTN_Q_TILE) // SEL_CHUNK
    for c in range(S // SEL_CHUNK):
        pl.when(cls == c)(functools.partial(step_body, c))


def _attn_tile(sub, c, qn_ref, qr_ref, kcmp_ref, vcmpt_ref, ksl_ref, vslt_ref, kwn_ref, vwnt_ref, gate_ref,
               ovt_ref, mask_ref, gout_ref, o_ref):
    tq = ATTN_Q_TILE
    sub_cols = slice(sub * tq, (sub + 1) * tq)
    rows = GQA_GROUP * tq
    S = ksl_ref.shape[2]
    n_cmp_pad = kcmp_ref.shape[2]
    n_sel = S // SEL_BLOCK
    win_tiles = WINDOW // tq
    unit_tiles = ATTN_UNIT // tq
    i = pl.program_id(1) * ATTN_SUBTILES + sub
    q0 = i * tq

    qpos = q0 + lax.broadcasted_iota(jnp.int32, (1, rows), 1) % tq
    blk = lax.broadcasted_iota(jnp.int32, (n_sel, tq), 0)
    tpos = q0 + lax.broadcasted_iota(jnp.int32, (n_sel, tq), 1)
    gates_t = gate_ref[0, sub_cols, :].T
    zero_rows = jnp.zeros((HEAD_DIM, rows), _bf16)

    def stacked_heads_t(ref, h):
        return jnp.concatenate([ref[0, GQA_GROUP * h + g, :, sub_cols] for g in range(GQA_GROUP)], axis=1)

    def tile_mask(key_tile, low_edge=False):
        kind = jnp.where(key_tile < i, MASK_FULL, jnp.where(key_tile == i, MASK_DIAG, MASK_NONE))
        if low_edge:
            kind = jnp.where(i >= win_tiles, MASK_LOW, kind)
        return mask_ref[kind]

    def attend(out, h, k_ref, vt_ref, q_t, first_tile, tile_masks):
        n_tiles = len(tile_masks)
        scores, mx = [], None
        for u0 in range(0, n_tiles, unit_tiles):
            nt = min(unit_tiles, n_tiles - u0)
            if isinstance(first_tile, int):
                keys = slice((first_tile + u0) * tq, (first_tile + u0 + nt) * tq)
            else:
                keys = pl.ds(pl.multiple_of((first_tile + u0) * tq, tq), nt * tq)
            sc = _dot(k_ref[0, h, keys, :], q_t)
            if any(tile_masks[u0 + t] is not None for t in range(nt)):
                sc = jnp.concatenate(
                    [sc[t * tq:(t + 1) * tq] if tile_masks[u0 + t] is None
                     else sc[t * tq:(t + 1) * tq] + tile_masks[u0 + t]() for t in range(nt)], axis=0)
            scores.append(sc)
            cm = jnp.max(sc, axis=0, keepdims=True)
            mx = cm if mx is None else jnp.maximum(mx, cm)
            yield
        probs = []
        for sc in scores:
            probs.append(jnp.exp2(sc - mx).astype(_bf16))
            yield
        vt = jnp.concatenate([vt_ref[0, h, first_tile + t] for t in range(n_tiles)], axis=1)
        acc = _dot(vt, jnp.concatenate(probs, axis=0))
        out[h] = acc[0:HEAD_DIM] / acc[HEAD_DIM:HEAD_DIM + 1]
        yield

    def window_stream(out, h, c):
        first_tile = jnp.maximum(i - win_tiles, 0)
        q_t = jnp.concatenate([stacked_heads_t(qr_ref, h), zero_rows], axis=0)
        if c * SEL_CHUNK >= WINDOW:
            masks = ([lambda: mask_ref[MASK_LOW]] + [None] * (win_tiles - 1) + [lambda: mask_ref[MASK_DIAG]])
        else:
            masks = [functools.partial(tile_mask, first_tile + t, low_edge=(t == 0))
                     for t in range(win_tiles + 1)]
        yield from attend(out, h, kwn_ref, vwnt_ref, q_t, first_tile, masks)

    def compressed_and_selected_stream(o_cmps, o_sels, h, c):
        qn = stacked_heads_t(qn_ref, h)
        qr = stacked_heads_t(qr_ref, h)

        s = _dot(kcmp_ref[0, h], jnp.concatenate([qn, zero_rows], axis=0))
        n_idx = lax.broadcasted_iota(jnp.int32, (n_cmp_pad, 1), 0)
        valid_c = n_idx * CMP_STRIDE + (CMP_BLOCK - 1) <= qpos
        s = jnp.where(valid_c, s, NEG_BIG)
        m = jnp.max(s, axis=0, keepdims=True)
        e = jnp.where(valid_c, jnp.exp2(s - m), 0.0)
        p_c = e / jnp.maximum(jnp.sum(e, axis=0, keepdims=True), 1e-20)
        o_cmps[h] = _dot(vcmpt_ref[0, h], p_c.astype(_bf16))[0:HEAD_DIM]
        yield

        eligible = blk * SEL_BLOCK <= tpos
        if SEL_CHUNK * (c + 1) <= SEL_TOPN * SEL_BLOCK:
            chosen = eligible
        else:
            p_sum = p_c[:, 0:tq] + p_c[:, tq:2 * tq] + p_c[:, 2 * tq:3 * tq] + p_c[:, 3 * tq:4 * tq]
            p_hi = p_sum.astype(_bf16)
            p_lo = (p_sum - p_hi.astype(_f32)).astype(_bf16)
            imp = _dot(ovt_ref[...], p_hi) + _dot(ovt_ref[...], p_lo)
            cur = tpos // SEL_BLOCK
            forced = (blk == 0) | (blk == cur) | (blk == cur - 1)
            score = jnp.where(forced, FORCE_SCORE, jnp.where(eligible, imp, -jnp.inf))
            rank = jnp.zeros((n_sel, tq), _f32)
            for jp in range(n_sel):
                other = score[jp:jp + 1, :]
                tie = jnp.where(blk > jp, 1.0, 0.0)
                rank = rank + jnp.where(other > score, 1.0, jnp.where(other == score, tie, 0.0))
            chosen = (rank < float(SEL_TOPN)) & eligible
        sel_bias = jnp.where(chosen, 0.0, NEG_BIG).astype(_bf16)
        yield

        q_t = jnp.concatenate([qr, jnp.concatenate([sel_bias] * GQA_GROUP, axis=1),
                               jnp.zeros((LANES - HEAD_DIM - n_sel, rows), _bf16)], axis=0)
        n_tiles = (SEL_CHUNK // tq) * (c + 1)
        first_edge = n_tiles - SEL_CHUNK // tq
        masks = [None if t < first_edge else functools.partial(tile_mask, t) for t in range(n_tiles)]
        yield from attend(o_sels, h, ksl_ref, vslt_ref, q_t, 0, masks)

    o_wins, o_cmps, o_sels = {}, {}, {}
    streams = [
        window_stream(o_wins, 0, c),
        compressed_and_selected_stream(o_cmps, o_sels, 0, c),
        _delayed(window_stream(o_wins, 1, c), STREAM_SKEW),
        _delayed(compressed_and_selected_stream(o_cmps, o_sels, 1, c), STREAM_SKEW),
    ]

    def finish():
        slabs = []
        for h in range(N_KV_HEADS):
            for g in range(GQA_GROUP):
                hq = GQA_GROUP * h + g
                cols = slice(g * tq, (g + 1) * tq)
                slabs.append(gates_t[hq:hq + 1] * o_cmps[h][:, cols]
                             + gates_t[N_ATTN_HEADS + hq:N_ATTN_HEADS + hq + 1] * o_sels[h][:, cols]
                             + gates_t[2 * N_ATTN_HEADS + hq:2 * N_ATTN_HEADS + hq + 1] * o_wins[h][:, cols])
        ot = jnp.concatenate(slabs, axis=0)
        ms = jnp.mean(ot * ot, axis=0, keepdims=True)
        o_ref[0, sub_cols, :] = (ot * lax.rsqrt(ms + EPS) * gout_ref[...]).T.astype(_bf16)

    return streams, finish


def _attn_call(qn_t, qr_t, kcmp, vcmp_t, ksl, vsl_t, kwn, vwn_t, gates, ovt, masks, gout_attn):
    B, _, _, S = qn_t.shape
    tq = ATTN_Q_TILE
    assert tq == LANES and S % SEL_CHUNK == 0 and SEL_CHUNK % tq == 0 and WINDOW % tq == 0
    assert ATTN_UNIT % tq == 0 and kcmp.shape[2] == LANES
    step = tq * ATTN_SUBTILES
    assert SEL_CHUNK % step == 0
    n_cmp_pad = kcmp.shape[2]
    n_sel = S // SEL_BLOCK
    rows = GQA_GROUP * tq
    per_b4 = lambda b, i: (b, 0, 0, 0)
    const2 = lambda b, i: (0, 0)
    vt_spec = pl.BlockSpec((1, N_KV_HEADS, S // LANES, LANES, LANES), lambda b, i: (b, 0, 0, 0, 0))
    return pl.pallas_call(
        _attn_kernel,
        grid=(B, S // step),
        in_specs=[
            pl.BlockSpec((1, N_ATTN_HEADS, HEAD_DIM, step), lambda b, i: (b, 0, 0, i)),
            pl.BlockSpec((1, N_ATTN_HEADS, HEAD_DIM, step), lambda b, i: (b, 0, 0, i)),
            pl.BlockSpec((1, N_KV_HEADS, n_cmp_pad, LANES), per_b4),
            pl.BlockSpec((1, N_KV_HEADS, n_cmp_pad, LANES), per_b4),
            pl.BlockSpec((1, N_KV_HEADS, S, LANES), per_b4),
            vt_spec,
            pl.BlockSpec((1, N_KV_HEADS, S, LANES), per_b4),
            vt_spec,
            pl.BlockSpec((1, step, LANES), lambda b, i: (b, i, 0)),
            pl.BlockSpec((n_sel, n_cmp_pad), const2),
            pl.BlockSpec((4, tq, rows), lambda b, i: (0, 0, 0)),
            pl.BlockSpec((ATTN_WIDTH, tq), const2),
        ],
        out_specs=pl.BlockSpec((1, step, ATTN_WIDTH), lambda b, i: (b, i, 0)),
        out_shape=jax.ShapeDtypeStruct((B, S, ATTN_WIDTH), _bf16),
        compiler_params=pltpu.CompilerParams(
            dimension_semantics=("parallel", "arbitrary"), vmem_limit_bytes=VMEM_LIMIT),
        name="attn",
    )(qn_t, qr_t, kcmp, vcmp_t, ksl, vsl_t, kwn, vwn_t, gates, ovt, masks, gout_attn)


def _ffn_kernel(x_ref, oa_ref, om_ref, wo_ref, gffn_ref, w1_ref, w2_ref, out_ref, act_scr):
    x2 = (x_ref[...] + _dot(oa_ref[...], wo_ref[0:ATTN_WIDTH, :])
          + _dot(om_ref[...], wo_ref[ATTN_WIDTH:ATTN_WIDTH + MIX_WIDTH, :]))
    ms = jnp.mean(x2 * x2, axis=-1, keepdims=True)
    h = (x2 * lax.rsqrt(ms + EPS) * gffn_ref[...]).astype(_bf16)
    for c in range(D_FF // FF_CHUNK):
        cols = slice(c * FF_CHUNK, (c + 1) * FF_CHUNK)
        a = jnp.maximum(_dot(h, w1_ref[:, cols]), 0.0)
        act_scr[:, cols] = (a * a).astype(_bf16)
    out_ref[...] = x2 + _dot(act_scr[...], w2_ref[...])


def _ffn_call(x2d, oa, om, wo, gffn, w1, w2):
    T = x2d.shape[0]
    tm = FFN_TILE
    tok = lambda i: (i, 0)
    const = lambda i: (0, 0)
    return pl.pallas_call(
        _ffn_kernel,
        grid=(T // tm,),
        in_specs=[
            pl.BlockSpec((tm, D_MODEL), tok),
            pl.BlockSpec((tm, ATTN_WIDTH), tok),
            pl.BlockSpec((tm, MIX_WIDTH), tok),
            pl.BlockSpec((D_MODEL, D_MODEL), const),
            pl.BlockSpec((1, D_MODEL), const),
            pl.BlockSpec((D_MODEL, D_FF), const),
            pl.BlockSpec((D_FF, D_MODEL), const),
        ],
        out_specs=pl.BlockSpec((tm, D_MODEL), tok),
        out_shape=jax.ShapeDtypeStruct((T, D_MODEL), _f32),
        scratch_shapes=[pltpu.VMEM((tm, D_FF), _bf16)],
        compiler_params=pltpu.CompilerParams(
            dimension_semantics=("parallel",), vmem_limit_bytes=VMEM_LIMIT),
        name="ffn",
    )(x2d, oa, om, wo, gffn, w1, w2)


def _constants(S):
    half = HEAD_DIM // 2
    inv = ROPE_THETA ** (-jnp.arange(half, dtype=_f32) / half)
    ang = jnp.arange(S, dtype=_f32)[:, None] * inv[None, :]
    cos = jnp.cos(ang)
    sin = jnp.sin(ang)
    cos2 = jnp.concatenate([cos, cos, cos, cos], axis=1)
    sin2 = jnp.concatenate([-sin, sin, -sin, sin], axis=1)
    mlane = np.arange(MXU_DIM)
    bd = (mlane[:, None] // HEAD_DIM == mlane[None, :] // HEAD_DIM).astype(np.float32) / HEAD_DIM
    n_cmp = (S - CMP_BLOCK) // CMP_STRIDE + 1
    n_sel = S // SEL_BLOCK
    n_cmp_pad = S // CMP_STRIDE
    cmp_start = np.arange(n_cmp)[:, None] * CMP_STRIDE
    sel_start = np.arange(n_sel)[None, :] * SEL_BLOCK
    overlap = np.clip(np.minimum(cmp_start + CMP_BLOCK, sel_start + SEL_BLOCK)
                      - np.maximum(cmp_start, sel_start), 0, None).astype(np.float32) / CMP_BLOCK
    ovt = np.zeros((n_sel, n_cmp_pad), np.float32)
    ovt[:, :n_cmp] = overlap.T
    qq = (np.arange(GQA_GROUP * ATTN_Q_TILE) % ATTN_Q_TILE)[None, :]
    kk = np.arange(ATTN_Q_TILE)[:, None]
    masks = np.zeros((4, ATTN_Q_TILE, GQA_GROUP * ATTN_Q_TILE), np.float32)
    masks[MASK_DIAG] = np.where(kk <= qq, 0.0, NEG_BIG)
    masks[MASK_LOW] = np.where(kk > qq, 0.0, NEG_BIG)
    masks[MASK_NONE] = NEG_BIG
    return cos2, sin2, jnp.asarray(bd, _bf16), jnp.asarray(ovt, _bf16), jnp.asarray(masks, _f32)


def _pack_w_in(w):
    gate0 = ATTN_WIDTH + 6 * KV_WIDTH
    n_g = N_GATES * N_ATTN_HEADS
    gates = w[:, gate0:gate0 + n_g].reshape(D_MODEL, N_ATTN_HEADS, N_GATES)
    gates = jnp.transpose(gates, (0, 2, 1)).reshape(D_MODEL, n_g)
    pad = jnp.zeros((D_MODEL, _C_KV + _KV_GROUP - gate0 - n_g), w.dtype)
    return jnp.concatenate([w[:, :gate0], gates, pad, w[:, gate0 + n_g:]], axis=1).astype(_bf16)


def _expand_cmp_w1(w1):
    w = w1.reshape(2, CMP_STRIDE, HEAD_DIM, CMP_HIDDEN)
    z = jnp.zeros_like(w)
    h0 = jnp.concatenate([w, z], axis=-1)
    h1 = jnp.concatenate([z, w], axis=-1)
    both = jnp.stack([h0, h1], axis=2)
    return both.reshape(2, CMP_STRIDE * KV_WIDTH, N_KV_HEADS * CMP_HIDDEN).astype(_bf16)


def _expand_pe(pe):
    p = pe.reshape(2, CMP_STRIDE, 1, HEAD_DIM)
    return jnp.broadcast_to(p, (2, CMP_STRIDE, N_KV_HEADS, HEAD_DIM)).reshape(2, CMP_STRIDE * KV_WIDTH)


def _forward(x, g_mix_norm, w_in, g_q, g_k, cmp_pe, cmp_w1, cmp_w2, g_sgu, sp_w, sp_b, g_out, w_out,
             g_ffn_norm, w_ff1, w_ff2):
    B, S, _ = x.shape
    l = 0
    scale = float(HEAD_DIM ** -0.5 * np.log2(np.e))
    cos2, sin2, bd, ovt, masks = _constants(S)

    w_in_p = _pack_w_in(w_in[l])
    gq = (jnp.tile(g_q[l], N_ATTN_HEADS) * scale)[None, :]
    gksl = jnp.tile(g_k[l, 1], N_KV_HEADS)[None, :]
    gkwn = jnp.tile(g_k[l, 2], N_KV_HEADS)[None, :]
    gsgu = g_sgu[l].reshape(1, MIX_WIDTH)
    spw = sp_w[l].reshape(N_MIX_GROUPS // 2, 2, CHUNK, CHUNK)
    spw = jnp.transpose(spw, (0, 2, 1, 3)).reshape(N_MIX_GROUPS // 2, CHUNK, 2 * CHUNK).astype(_bf16)
    spb = jnp.repeat(sp_b[l].T, HEAD_DIM, axis=1)
    gout = g_out[l]

    qn, qr, kc, vc, ksl, vsl, kwn, vwn, gates, omix = _proj_call(
        x, g_mix_norm[l][None, :], w_in_p, gq, gksl, gkwn, cos2, sin2, bd, gsgu, spw, spb,
        gout[None, ATTN_WIDTH:])

    pe_rows = jnp.stack([_expand_pe(cmp_pe[l, 0]), _expand_pe(cmp_pe[l, 1])])
    w1x = jnp.stack([_expand_cmp_w1(cmp_w1[l, 0]), _expand_cmp_w1(cmp_w1[l, 1])])
    kcmp, vcmp = _compress_call(kc, vc, pe_rows, w1x, cmp_w2[l].astype(_bf16), g_k[l, 0][None, :])

    oattn = _attn_call(qn, qr, kcmp, vcmp, ksl, vsl, kwn, vwn, gates, ovt, masks,
                       jnp.broadcast_to(gout[:ATTN_WIDTH, None], (ATTN_WIDTH, ATTN_Q_TILE)))

    out = _ffn_call(x.reshape(B * S, D_MODEL), oattn.reshape(B * S, ATTN_WIDTH),
                    omix.reshape(B * S, MIX_WIDTH), w_out[l].astype(_bf16), g_ffn_norm[l][None, :],
                    w_ff1[l].astype(_bf16), w_ff2[l].astype(_bf16))
    return out.reshape(B, S, D_MODEL), (qn, qr, kc, vc, ksl, vsl, kwn, vwn, gates, omix, kcmp, vcmp, oattn)


def kernel(x, g_mix_norm, w_in, g_q, g_k, cmp_pe, cmp_w1, cmp_w2, g_sgu, sp_w, sp_b, g_out, w_out,
           g_ffn_norm, w_ff1, w_ff2):
    return _forward(x, g_mix_norm, w_in, g_q, g_k, cmp_pe, cmp_w1, cmp_w2, g_sgu, sp_w, sp_b, g_out, w_out,
                    g_ffn_norm, w_ff1, w_ff2)[0]
```

```python
import functools

import numpy as np
import jax
import jax.numpy as jnp
from jax import lax
from jax.experimental import pallas as pl
from jax.experimental.pallas import tpu as pltpu

D_MODEL = 1024
HEAD_DIM = 64
N_ATTN_HEADS = 8
N_MIX_GROUPS = 8
GQA_GROUP = 4
N_KV_HEADS = 2
ATTN_WIDTH = 512
MIX_WIDTH = 512
KV_WIDTH = 128
N_GATES = 3
CMP_BLOCK = 32
CMP_STRIDE = 16
CMP_HIDDEN = 256
SEL_BLOCK = 64
SEL_TOPN = 16
WINDOW = 512
CHUNK = 128
D_FF = 4 * D_MODEL
ROPE_THETA = 10000.0
EPS = 1e-6
FORCE_SCORE = 1e9

LANES = 128
NEG_BIG = -1e30
PROJ_TILE = 256
PROJ_UNIT = 512
ATTN_Q_TILE = 128
ATTN_SUBTILES = 2
TILE_SKEW = 4
ATTN_UNIT = 256
STREAM_SKEW = 1
MASK_DIAG, MASK_LOW = 0, 1
FFN_TILE = 256
FF_CHUNK = 1024
VMEM_LIMIT = 56 * 1024 * 1024

_C_Q = 0
_C_KV = 512
_KV_KC, _KV_VC, _KV_KSL, _KV_VSL, _KV_KWN, _KV_VWN, _KV_GATE = (128 * n for n in range(7))
_KV_GROUP = 1024
_C_ZU = 1536
_C_ZV = 2048
_IN_COLS_PACKED = 2560
MXU_DIM = 256

_bf16 = jnp.bfloat16
_f32 = jnp.float32


def _dot(a, b):
    return jnp.dot(a, b, preferred_element_type=_f32)


def _group_mean(sq, bd_ref):
    cw = min(sq.shape[1], MXU_DIM)
    bd = bd_ref[0:cw, 0:cw]
    parts = [_dot(sq[:, c * cw:(c + 1) * cw].astype(_bf16), bd) for c in range(sq.shape[1] // cw)]
    return parts[0] if len(parts) == 1 else jnp.concatenate(parts, axis=1)


def _swap_halves(v):
    n = v.shape[1]
    lane = lax.broadcasted_iota(jnp.int32, v.shape, 1)
    return jnp.where((lane % HEAD_DIM) < HEAD_DIM // 2, pltpu.roll(v, n - HEAD_DIM // 2, 1),
                     pltpu.roll(v, HEAD_DIM // 2, 1))


def _proj_kernel(tiles_per_seq, x_ref, gmix_ref, w_ref, gq_ref, gksl_ref, gkwn_ref, cos_ref, sin_ref, bd_ref,
                 gsgu_ref, spw_ref, spb_ref, gomix_ref,
                 qn_ref, qr_ref, kc_ref, vc_ref, ksl_ref, vsl_ref, kwn_ref, vwn_ref, gate_ref, omix_ref,
                 h_even, h_odd, z_even, z_odd, regroup_scr):
    t = pl.program_id(0)

    @pl.when(t == 0)
    def _():
        h_odd[...] = jnp.zeros(h_odd.shape, _bf16)
        z_odd[...] = jnp.zeros(z_odd.shape, _f32)

    pl.when(t % 2 == 0)(functools.partial(
        _proj_step, tiles_per_seq, x_ref, gmix_ref, w_ref, gq_ref, gksl_ref, gkwn_ref, cos_ref, sin_ref, bd_ref,
        gsgu_ref, spw_ref, spb_ref, gomix_ref, qn_ref, qr_ref, kc_ref, vc_ref, ksl_ref, vsl_ref, kwn_ref,
        vwn_ref, gate_ref, omix_ref, h_even, h_odd, z_even, z_odd, regroup_scr))
    pl.when(t % 2 == 1)(functools.partial(
        _proj_step, tiles_per_seq, x_ref, gmix_ref, w_ref, gq_ref, gksl_ref, gkwn_ref, cos_ref, sin_ref, bd_ref,
        gsgu_ref, spw_ref, spb_ref, gomix_ref, qn_ref, qr_ref, kc_ref, vc_ref, ksl_ref, vsl_ref, kwn_ref,
        vwn_ref, gate_ref, omix_ref, h_odd, h_even, z_odd, z_even, regroup_scr))


def _proj_step(tiles_per_seq, x_ref, gmix_ref, w_ref, gq_ref, gksl_ref, gkwn_ref, cos_ref, sin_ref, bd_ref,
               gsgu_ref, spw_ref, spb_ref, gomix_ref,
               qn_ref, qr_ref, kc_ref, vc_ref, ksl_ref, vsl_ref, kwn_ref, vwn_ref, gate_ref, omix_ref,
               h_next, h_prev, z_next, z_prev, regroup_scr):
    tm = x_ref.shape[0]
    i = jnp.maximum(pl.program_id(0) - 2, 0) % tiles_per_seq
    cos = cos_ref[...]
    sin = sin_ref[...]

    def project_stream():
        h = h_prev[...]
        for lo in range(0, _IN_COLS_PACKED, PROJ_UNIT):
            z_next[:, lo:lo + PROJ_UNIT] = _dot(h, w_ref[:, lo:lo + PROJ_UNIT])
            yield

    def prenorm_stream():
        x = x_ref[...]
        ms = jnp.mean(x * x, axis=-1, keepdims=True)
        h_next[...] = (x * lax.rsqrt(ms + EPS) * gmix_ref[...]).astype(_bf16)
        yield

    def query_stream():
        zq = z_prev[:, _C_Q:_C_Q + ATTN_WIDTH]
        qn = zq * lax.rsqrt(_group_mean(zq * zq, bd_ref) + EPS) * gq_ref[...]
        yield
        cos4 = jnp.concatenate([cos] * 4, axis=1)
        sin4 = jnp.concatenate([sin] * 4, axis=1)
        qr = qn * cos4 + _swap_halves(qn) * sin4
        yield
        qn_t, qr_t = qn.T, qr.T
        for hq in range(N_ATTN_HEADS):
            sl = slice(hq * HEAD_DIM, (hq + 1) * HEAD_DIM)
            qn_ref[0, hq] = qn_t[sl].astype(_bf16)
            qr_ref[0, hq] = qr_t[sl].astype(_bf16)

    def kv_stream():
        def segment(off):
            return z_prev[:, _C_KV + off:_C_KV + off + KV_WIDTH]

        for n, (off, out_ref) in enumerate(((_KV_KC, kc_ref), (_KV_VC, vc_ref))):
            regroup_scr[n] = segment(off)
            for tok in range(CMP_STRIDE):
                every_16th = regroup_scr[n, pl.ds(tok, tm // CMP_STRIDE, stride=CMP_STRIDE), :]
                out_ref[0, :, tok * KV_WIDTH:(tok + 1) * KV_WIDTH] = every_16th.astype(_bf16)
        gate_ref[0] = jax.nn.sigmoid(segment(_KV_GATE))

        ones_block = jnp.where(lax.broadcasted_iota(jnp.int32, (HEAD_DIM, LANES), 0) == 0, 1.0, 0.0)
        for off, out_ref in ((_KV_VSL, vsl_ref), (_KV_VWN, vwn_ref)):
            zv2 = segment(off)
            for tt in range(tm // LANES):
                zt = zv2[tt * LANES:(tt + 1) * LANES, :].T
                for hh in range(N_KV_HEADS):
                    out_ref[0, hh, tt] = jnp.concatenate(
                        [zt[hh * HEAD_DIM:(hh + 1) * HEAD_DIM], ones_block], axis=0).astype(_bf16)
            yield

        lane = lax.broadcasted_iota(jnp.int32, (tm, LANES), 1)
        pos = i * tm + lax.broadcasted_iota(jnp.int32, (tm, LANES), 0)
        blk_onehot = jnp.where(lane - HEAD_DIM == pos // SEL_BLOCK, 1.0, 0.0)
        for off, g_ref, out_ref, extra in ((_KV_KSL, gksl_ref, ksl_ref, blk_onehot),
                                           (_KV_KWN, gkwn_ref, kwn_ref, jnp.zeros((tm, LANES), _f32))):
            zk = segment(off)
            kn = zk * lax.rsqrt(_group_mean(zk * zk, bd_ref) + EPS) * g_ref[...]
            kr = kn * cos + _swap_halves(kn) * sin
            out_ref[0, 0] = jnp.where(lane < HEAD_DIM, kr, extra).astype(_bf16)
            out_ref[0, 1] = jnp.where(lane < HEAD_DIM, pltpu.roll(kr, HEAD_DIM, 1), extra).astype(_bf16)
            yield

    row = lax.broadcasted_iota(jnp.int32, (CHUNK, 2 * CHUNK), 0)
    colw = lax.broadcasted_iota(jnp.int32, (CHUNK, 2 * CHUNK), 1) % CHUNK
    causal_w = colw <= row
    lane_c = lax.broadcasted_iota(jnp.int32, (CHUNK, LANES), 1)

    def gmlp_stream(c):
        rows = slice(c * CHUNK, (c + 1) * CHUNK)
        zu = jax.nn.gelu(z_prev[rows, _C_ZU:_C_ZU + MIX_WIDTH])
        yield
        zv = jax.nn.gelu(z_prev[rows, _C_ZV:_C_ZV + MIX_WIDTH])
        yield
        vn = zv * lax.rsqrt(_group_mean(zv * zv, bd_ref) + EPS) * gsgu_ref[...]
        yield
        sv_parts = []
        for p in range(N_MIX_GROUPS // 2):
            vp = vn[:, p * LANES:(p + 1) * LANES]
            rhs = jnp.concatenate([jnp.where(lane_c < HEAD_DIM, vp, 0.0),
                                   jnp.where(lane_c < HEAD_DIM, 0.0, vp)], axis=0).astype(_bf16)
            w_pair = jnp.where(causal_w, spw_ref[p], jnp.zeros((), _bf16))
            sv_parts.append(_dot(w_pair, rhs))
        yield
        sv = jnp.concatenate(sv_parts, axis=1) + spb_ref[...]
        om = zu * sv
        oms = jnp.mean(om * om, axis=-1, keepdims=True)
        omix_ref[0, rows] = (om * lax.rsqrt(oms + EPS) * gomix_ref[...]).astype(_bf16)

    _interleave([project_stream(), query_stream(), kv_stream()] + [gmlp_stream(c) for c in range(tm // CHUNK)]
                + [prenorm_stream()])


def _proj_call(x, gmix, w_in_p, gq, gksl, gkwn, cos, sin, bd, gsgu, spw, spb, gomix):
    B, S, _ = x.shape
    tm = PROJ_TILE
    nt = S // tm
    n_tiles = B * nt
    const2 = lambda t: (0, 0)
    const3 = lambda t: (0, 0, 0)

    def done(t):
        return jnp.maximum(t - 2, 0)

    tok3 = lambda t: (done(t) // nt, done(t) % nt, 0)
    head4 = lambda t: (done(t) // nt, 0, done(t) % nt, 0)
    out_shape = (
        jax.ShapeDtypeStruct((B, N_ATTN_HEADS, HEAD_DIM, S), _bf16),
        jax.ShapeDtypeStruct((B, N_ATTN_HEADS, HEAD_DIM, S), _bf16),
        jax.ShapeDtypeStruct((B, S // CMP_STRIDE, CMP_STRIDE * KV_WIDTH), _bf16),
        jax.ShapeDtypeStruct((B, S // CMP_STRIDE, CMP_STRIDE * KV_WIDTH), _bf16),
        jax.ShapeDtypeStruct((B, N_KV_HEADS, S, LANES), _bf16),
        jax.ShapeDtypeStruct((B, N_KV_HEADS, S // LANES, LANES, LANES), _bf16),
        jax.ShapeDtypeStruct((B, N_KV_HEADS, S, LANES), _bf16),
        jax.ShapeDtypeStruct((B, N_KV_HEADS, S // LANES, LANES, LANES), _bf16),
        jax.ShapeDtypeStruct((B, S, LANES), _f32),
        jax.ShapeDtypeStruct((B, S, MIX_WIDTH), _bf16),
    )
    q_spec = pl.BlockSpec((1, N_ATTN_HEADS, HEAD_DIM, tm), lambda t: (done(t) // nt, 0, 0, done(t) % nt))
    k_spec = pl.BlockSpec((1, N_KV_HEADS, tm, LANES), head4)
    t_spec = pl.BlockSpec((1, tm, LANES), tok3)
    g_spec = pl.BlockSpec((1, tm // CMP_STRIDE, CMP_STRIDE * KV_WIDTH), tok3)
    vt_spec = pl.BlockSpec((1, N_KV_HEADS, tm // LANES, LANES, LANES),
                           lambda t: (done(t) // nt, 0, done(t) % nt, 0, 0))
    return pl.pallas_call(
        functools.partial(_proj_kernel, nt),
        grid=(n_tiles + 2,),
        in_specs=[
            pl.BlockSpec((tm, D_MODEL), lambda t: (jnp.minimum(t, n_tiles - 1), 0)),
            pl.BlockSpec((1, D_MODEL), const2),
            pl.BlockSpec((D_MODEL, _IN_COLS_PACKED), const2),
            pl.BlockSpec((1, ATTN_WIDTH), const2),
            pl.BlockSpec((1, KV_WIDTH), const2),
            pl.BlockSpec((1, KV_WIDTH), const2),
            pl.BlockSpec((tm, LANES), lambda t: (done(t) % nt, 0)),
            pl.BlockSpec((tm, LANES), lambda t: (done(t) % nt, 0)),
            pl.BlockSpec((MXU_DIM, MXU_DIM), const2),
            pl.BlockSpec((1, MIX_WIDTH), const2),
            pl.BlockSpec((N_MIX_GROUPS // 2, CHUNK, 2 * CHUNK), const3),
            pl.BlockSpec((CHUNK, MIX_WIDTH), const2),
            pl.BlockSpec((1, MIX_WIDTH), const2),
        ],
        out_specs=(q_spec, q_spec, g_spec, g_spec, k_spec, vt_spec, k_spec, vt_spec, t_spec,
                   pl.BlockSpec((1, tm, MIX_WIDTH), tok3)),
        out_shape=out_shape,
        scratch_shapes=[pltpu.VMEM((tm, D_MODEL), _bf16), pltpu.VMEM((tm, D_MODEL), _bf16),
                        pltpu.VMEM((tm, _IN_COLS_PACKED), _f32), pltpu.VMEM((tm, _IN_COLS_PACKED), _f32),
                        pltpu.VMEM((2, tm, KV_WIDTH), _f32)],
        compiler_params=pltpu.CompilerParams(
            dimension_semantics=("arbitrary",), vmem_limit_bytes=VMEM_LIMIT),
        name="proj",
    )(x.reshape(B * S, D_MODEL), gmix, w_in_p, gq, gksl, gkwn, cos, sin, bd, gsgu, spw, spb, gomix)


def _compress_kernel(kc_ref, vc_ref, pe_ref, w1_ref, w2_ref, gk_ref, kcmp_ref, vcmp_ref):
    n_rows = kc_ref.shape[1]
    zero_half = jnp.zeros((n_rows, HEAD_DIM), _f32)
    for t, (src_ref, out_ref) in enumerate(((kc_ref, kcmp_ref), (vc_ref, vcmp_ref))):
        g = src_ref[0].astype(_f32)
        first = _dot((g + pe_ref[t, 0:1]).astype(_bf16), w1_ref[t, 0])
        second = _dot((g + pe_ref[t, 1:2]).astype(_bf16), w1_ref[t, 1])
        hid = jax.nn.gelu(first + pltpu.roll(second, n_rows - 1, 0)).astype(_bf16)
        heads = [_dot(hid[:, hh * CMP_HIDDEN:(hh + 1) * CMP_HIDDEN], w2_ref[t]) for hh in range(N_KV_HEADS)]
        if t == 0:
            heads = [c * lax.rsqrt(jnp.mean(c * c, axis=-1, keepdims=True) + EPS) * gk_ref[...] for c in heads]
            for hh in range(N_KV_HEADS):
                out_ref[0, hh] = jnp.concatenate([heads[hh], zero_half], axis=1).astype(_bf16)
        else:
            for hh in range(N_KV_HEADS):
                vt = jnp.concatenate([heads[hh], zero_half], axis=1).T
                out_ref[0, hh] = vt.astype(_bf16)


def _compress_call(kc_g, vc_g, pe_rows, w1x, w2, gk0):
    B, n_rows, width = kc_g.shape
    return pl.pallas_call(
        _compress_kernel,
        grid=(B,),
        in_specs=[
            pl.BlockSpec((1, n_rows, width), lambda b: (b, 0, 0)),
            pl.BlockSpec((1, n_rows, width), lambda b: (b, 0, 0)),
            pl.BlockSpec((2, 2, width), lambda b: (0, 0, 0)),
            pl.BlockSpec((2, 2, width, N_KV_HEADS * CMP_HIDDEN), lambda b: (0, 0, 0, 0)),
            pl.BlockSpec((2, CMP_HIDDEN, HEAD_DIM), lambda b: (0, 0, 0)),
            pl.BlockSpec((1, HEAD_DIM), lambda b: (0, 0)),
        ],
        out_specs=(pl.BlockSpec((1, N_KV_HEADS, n_rows, LANES), lambda b: (b, 0, 0, 0)),
                   pl.BlockSpec((1, N_KV_HEADS, n_rows, LANES), lambda b: (b, 0, 0, 0))),
        out_shape=(jax.ShapeDtypeStruct((B, N_KV_HEADS, n_rows, LANES), _bf16),
                   jax.ShapeDtypeStruct((B, N_KV_HEADS, n_rows, LANES), _bf16)),
        compiler_params=pltpu.CompilerParams(
            dimension_semantics=("parallel",), vmem_limit_bytes=VMEM_LIMIT),
        name="compress",
    )(kc_g, vc_g, pe_rows, w1x, w2, gk0)


def _interleave(streams):
    live = list(streams)
    while live:
        for g in list(live):
            try:
                next(g)
            except StopIteration:
                live.remove(g)


def _delayed(stream, units):
    for _ in range(units):
        yield
    yield from stream


def _attn_kernel(qn_ref, qr_ref, kcmp_ref, vcmpt_ref, ksl_ref, vslt_ref, kwn_ref, vwnt_ref, gate_ref,
                 ovt_ref, mask_ref, gout_ref, o_ref):
    refs = (qn_ref, qr_ref, kcmp_ref, vcmpt_ref, ksl_ref, vslt_ref, kwn_ref, vwnt_ref, gate_ref,
            ovt_ref, mask_ref, gout_ref, o_ref)
    S = ksl_ref.shape[2]

    def step_body(c):
        tiles = [_attn_tile(sub, c, *refs) for sub in range(ATTN_SUBTILES)]
        _interleave([_delayed(stream, sub * TILE_SKEW) for sub, (streams, _) in enumerate(tiles)
                     for stream in streams])
        for _, finish in tiles:
            finish()

    for c in range(S // (ATTN_SUBTILES * ATTN_Q_TILE)):
        pl.when(pl.program_id(1) == c)(functools.partial(step_body, c))


def _attn_tile(sub, c, qn_ref, qr_ref, kcmp_ref, vcmpt_ref, ksl_ref, vslt_ref, kwn_ref, vwnt_ref, gate_ref,
               ovt_ref, mask_ref, gout_ref, o_ref):
    tq = ATTN_Q_TILE
    sub_cols = slice(sub * tq, (sub + 1) * tq)
    rows = GQA_GROUP * tq
    S = ksl_ref.shape[2]
    n_cmp_pad = kcmp_ref.shape[2]
    n_sel = S // SEL_BLOCK
    win_tiles = WINDOW // tq
    unit_tiles = ATTN_UNIT // tq
    i = c * ATTN_SUBTILES + sub
    q0 = i * tq

    qpos = q0 + lax.broadcasted_iota(jnp.int32, (1, rows), 1) % tq
    blk = lax.broadcasted_iota(jnp.int32, (n_sel, tq), 0)
    tpos = q0 + lax.broadcasted_iota(jnp.int32, (n_sel, tq), 1)
    gates_t = gate_ref[0, sub_cols, :].T
    zero_rows = jnp.zeros((HEAD_DIM, rows), _bf16)

    def stacked_heads_t(ref, h):
        return jnp.concatenate([ref[0, GQA_GROUP * h + g, :, sub_cols] for g in range(GQA_GROUP)], axis=1)

    def attend(out, h, k_ref, vt_ref, q_t, first_tile, tile_masks):
        n_tiles = len(tile_masks)
        scores, mx = [], None
        for u0 in range(0, n_tiles, unit_tiles):
            nt = min(unit_tiles, n_tiles - u0)
            keys = slice((first_tile + u0) * tq, (first_tile + u0 + nt) * tq)
            sc = _dot(k_ref[0, h, keys, :], q_t)
            if any(tile_masks[u0 + t] is not None for t in range(nt)):
                sc = jnp.concatenate(
                    [sc[t * tq:(t + 1) * tq] if tile_masks[u0 + t] is None
                     else sc[t * tq:(t + 1) * tq] + tile_masks[u0 + t]() for t in range(nt)], axis=0)
            scores.append(sc)
            cm = jnp.max(sc, axis=0, keepdims=True)
            mx = cm if mx is None else jnp.maximum(mx, cm)
            yield
        probs = []
        for sc in scores:
            probs.append(jnp.exp2(sc - mx).astype(_bf16))
            yield
        vt = jnp.concatenate([vt_ref[0, h, first_tile + t] for t in range(n_tiles)], axis=1)
        acc = _dot(vt, jnp.concatenate(probs, axis=0))
        out[h] = acc[0:HEAD_DIM] / acc[HEAD_DIM:HEAD_DIM + 1]
        yield

    def window_stream(out, h, c):
        first_tile = max(i - win_tiles, 0)
        q_t = jnp.concatenate([stacked_heads_t(qr_ref, h), zero_rows], axis=0)
        masks = [None] * (i - first_tile) + [lambda: mask_ref[MASK_DIAG]]
        if i >= win_tiles:
            masks[0] = lambda: mask_ref[MASK_LOW]
        yield from attend(out, h, kwn_ref, vwnt_ref, q_t, first_tile, masks)

    def compressed_and_selected_stream(o_cmps, o_sels, h, c):
        qn = stacked_heads_t(qn_ref, h)
        qr = stacked_heads_t(qr_ref, h)

        s = _dot(kcmp_ref[0, h], jnp.concatenate([qn, zero_rows], axis=0))
        n_idx = lax.broadcasted_iota(jnp.int32, (n_cmp_pad, 1), 0)
        valid_c = n_idx * CMP_STRIDE + (CMP_BLOCK - 1) <= qpos
        s = jnp.where(valid_c, s, NEG_BIG)
        m = jnp.max(s, axis=0, keepdims=True)
        e = jnp.where(valid_c, jnp.exp2(s - m), 0.0)
        p_c = e / jnp.maximum(jnp.sum(e, axis=0, keepdims=True), 1e-20)
        o_cmps[h] = _dot(vcmpt_ref[0, h], p_c.astype(_bf16))[0:HEAD_DIM]
        yield

        eligible = blk * SEL_BLOCK <= tpos
        if (i + 1) * tq <= SEL_TOPN * SEL_BLOCK:
            chosen = eligible
        else:
            p_sum = p_c[:, 0:tq] + p_c[:, tq:2 * tq] + p_c[:, 2 * tq:3 * tq] + p_c[:, 3 * tq:4 * tq]
            p_hi = p_sum.astype(_bf16)
            p_lo = (p_sum - p_hi.astype(_f32)).astype(_bf16)
            imp = _dot(ovt_ref[...], p_hi) + _dot(ovt_ref[...], p_lo)
            cur = tpos // SEL_BLOCK
            forced = (blk == 0) | (blk == cur) | (blk == cur - 1)
            score = jnp.where(forced, FORCE_SCORE, jnp.where(eligible, imp, -jnp.inf))
            rank = jnp.zeros((n_sel, tq), _f32)
            for jp in range(n_sel):
                other = score[jp:jp + 1, :]
                tie = jnp.where(blk > jp, 1.0, 0.0)
                rank = rank + jnp.where(other > score, 1.0, jnp.where(other == score, tie, 0.0))
            chosen = (rank < float(SEL_TOPN)) & eligible
        sel_bias = jnp.where(chosen, 0.0, NEG_BIG).astype(_bf16)
        yield

        q_t = jnp.concatenate([qr, jnp.concatenate([sel_bias] * GQA_GROUP, axis=1),
                               jnp.zeros((LANES - HEAD_DIM - n_sel, rows), _bf16)], axis=0)
        masks = [None] * i + [lambda: mask_ref[MASK_DIAG]]
        yield from attend(o_sels, h, ksl_ref, vslt_ref, q_t, 0, masks)

    o_wins, o_cmps, o_sels = {}, {}, {}
    streams = [
        window_stream(o_wins, 0, c),
        compressed_and_selected_stream(o_cmps, o_sels, 0, c),
        _delayed(window_stream(o_wins, 1, c), STREAM_SKEW),
        _delayed(compressed_and_selected_stream(o_cmps, o_sels, 1, c), STREAM_SKEW),
    ]

    def finish():
        slabs = []
        for h in range(N_KV_HEADS):
            for g in range(GQA_GROUP):
                hq = GQA_GROUP * h + g
                cols = slice(g * tq, (g + 1) * tq)
                slabs.append(gates_t[hq:hq + 1] * o_cmps[h][:, cols]
                             + gates_t[N_ATTN_HEADS + hq:N_ATTN_HEADS + hq + 1] * o_sels[h][:, cols]
                             + gates_t[2 * N_ATTN_HEADS + hq:2 * N_ATTN_HEADS + hq + 1] * o_wins[h][:, cols])
        ot = jnp.concatenate(slabs, axis=0)
        ms = jnp.mean(ot * ot, axis=0, keepdims=True)
        o_ref[0, sub_cols, :] = (ot * lax.rsqrt(ms + EPS) * gout_ref[...]).T.astype(_bf16)

    return streams, finish


def _attn_call(qn_t, qr_t, kcmp, vcmp_t, ksl, vsl_t, kwn, vwn_t, gates, ovt, masks, gout_attn):
    B, _, _, S = qn_t.shape
    tq = ATTN_Q_TILE
    step = tq * ATTN_SUBTILES
    assert tq == LANES and S % step == 0 and WINDOW % tq == 0
    assert ATTN_UNIT % tq == 0 and kcmp.shape[2] == LANES
    n_cmp_pad = kcmp.shape[2]
    n_sel = S // SEL_BLOCK
    rows = GQA_GROUP * tq
    per_b4 = lambda b, i: (b, 0, 0, 0)
    const2 = lambda b, i: (0, 0)
    vt_spec = pl.BlockSpec((1, N_KV_HEADS, S // LANES, LANES, LANES), lambda b, i: (b, 0, 0, 0, 0))
    return pl.pallas_call(
        _attn_kernel,
        grid=(B, S // step),
        in_specs=[
            pl.BlockSpec((1, N_ATTN_HEADS, HEAD_DIM, step), lambda b, i: (b, 0, 0, i)),
            pl.BlockSpec((1, N_ATTN_HEADS, HEAD_DIM, step), lambda b, i: (b, 0, 0, i)),
            pl.BlockSpec((1, N_KV_HEADS, n_cmp_pad, LANES), per_b4),
            pl.BlockSpec((1, N_KV_HEADS, n_cmp_pad, LANES), per_b4),
            pl.BlockSpec((1, N_KV_HEADS, S, LANES), per_b4),
            vt_spec,
            pl.BlockSpec((1, N_KV_HEADS, S, LANES), per_b4),
            vt_spec,
            pl.BlockSpec((1, step, LANES), lambda b, i: (b, i, 0)),
            pl.BlockSpec((n_sel, n_cmp_pad), const2),
            pl.BlockSpec((2, tq, rows), lambda b, i: (0, 0, 0)),
            pl.BlockSpec((ATTN_WIDTH, tq), const2),
        ],
        out_specs=pl.BlockSpec((1, step, ATTN_WIDTH), lambda b, i: (b, i, 0)),
        out_shape=jax.ShapeDtypeStruct((B, S, ATTN_WIDTH), _bf16),
        compiler_params=pltpu.CompilerParams(
            dimension_semantics=("parallel", "arbitrary"), vmem_limit_bytes=VMEM_LIMIT),
        name="attn",
    )(qn_t, qr_t, kcmp, vcmp_t, ksl, vsl_t, kwn, vwn_t, gates, ovt, masks, gout_attn)


def _ffn_kernel(x_ref, oa_ref, om_ref, wo_ref, gffn_ref, w1_ref, w2_ref, out_ref, act_scr):
    x2 = (x_ref[...] + _dot(oa_ref[...], wo_ref[0:ATTN_WIDTH, :])
          + _dot(om_ref[...], wo_ref[ATTN_WIDTH:ATTN_WIDTH + MIX_WIDTH, :]))
    ms = jnp.mean(x2 * x2, axis=-1, keepdims=True)
    h = (x2 * lax.rsqrt(ms + EPS) * gffn_ref[...]).astype(_bf16)
    for c in range(D_FF // FF_CHUNK):
        cols = slice(c * FF_CHUNK, (c + 1) * FF_CHUNK)
        a = jnp.maximum(_dot(h, w1_ref[:, cols]), 0.0)
        act_scr[:, cols] = (a * a).astype(_bf16)
    out_ref[...] = x2 + _dot(act_scr[...], w2_ref[...])


def _ffn_call(x2d, oa, om, wo, gffn, w1, w2):
    T = x2d.shape[0]
    tm = FFN_TILE
    tok = lambda i: (i, 0)
    const = lambda i: (0, 0)
    return pl.pallas_call(
        _ffn_kernel,
        grid=(T // tm,),
        in_specs=[
            pl.BlockSpec((tm, D_MODEL), tok),
            pl.BlockSpec((tm, ATTN_WIDTH), tok),
            pl.BlockSpec((tm, MIX_WIDTH), tok),
            pl.BlockSpec((D_MODEL, D_MODEL), const),
            pl.BlockSpec((1, D_MODEL), const),
            pl.BlockSpec((D_MODEL, D_FF), const),
            pl.BlockSpec((D_FF, D_MODEL), const),
        ],
        out_specs=pl.BlockSpec((tm, D_MODEL), tok),
        out_shape=jax.ShapeDtypeStruct((T, D_MODEL), _f32),
        scratch_shapes=[pltpu.VMEM((tm, D_FF), _bf16)],
        compiler_params=pltpu.CompilerParams(
            dimension_semantics=("parallel",), vmem_limit_bytes=VMEM_LIMIT),
        name="ffn",
    )(x2d, oa, om, wo, gffn, w1, w2)


def _constants(S):
    half = HEAD_DIM // 2
    inv = ROPE_THETA ** (-jnp.arange(half, dtype=_f32) / half)
    ang = jnp.arange(S, dtype=_f32)[:, None] * inv[None, :]
    cos = jnp.cos(ang)
    sin = jnp.sin(ang)
    cos2 = jnp.concatenate([cos, cos, cos, cos], axis=1)
    sin2 = jnp.concatenate([-sin, sin, -sin, sin], axis=1)
    mlane = np.arange(MXU_DIM)
    bd = (mlane[:, None] // HEAD_DIM == mlane[None, :] // HEAD_DIM).astype(np.float32) / HEAD_DIM
    n_cmp = (S - CMP_BLOCK) // CMP_STRIDE + 1
    n_sel = S // SEL_BLOCK
    n_cmp_pad = S // CMP_STRIDE
    cmp_start = np.arange(n_cmp)[:, None] * CMP_STRIDE
    sel_start = np.arange(n_sel)[None, :] * SEL_BLOCK
    overlap = np.clip(np.minimum(cmp_start + CMP_BLOCK, sel_start + SEL_BLOCK)
                      - np.maximum(cmp_start, sel_start), 0, None).astype(np.float32) / CMP_BLOCK
    ovt = np.zeros((n_sel, n_cmp_pad), np.float32)
    ovt[:, :n_cmp] = overlap.T
    qq = (np.arange(GQA_GROUP * ATTN_Q_TILE) % ATTN_Q_TILE)[None, :]
    kk = np.arange(ATTN_Q_TILE)[:, None]
    masks = np.zeros((2, ATTN_Q_TILE, GQA_GROUP * ATTN_Q_TILE), np.float32)
    masks[MASK_DIAG] = np.where(kk <= qq, 0.0, NEG_BIG)
    masks[MASK_LOW] = np.where(kk > qq, 0.0, NEG_BIG)
    return cos2, sin2, jnp.asarray(bd, _bf16), jnp.asarray(ovt, _bf16), jnp.asarray(masks, _f32)


def _pack_w_in(w):
    gate0 = ATTN_WIDTH + 6 * KV_WIDTH
    n_g = N_GATES * N_ATTN_HEADS
    gates = w[:, gate0:gate0 + n_g].reshape(D_MODEL, N_ATTN_HEADS, N_GATES)
    gates = jnp.transpose(gates, (0, 2, 1)).reshape(D_MODEL, n_g)
    pad = jnp.zeros((D_MODEL, _C_KV + _KV_GROUP - gate0 - n_g), w.dtype)
    return jnp.concatenate([w[:, :gate0], gates, pad, w[:, gate0 + n_g:]], axis=1).astype(_bf16)


def _expand_cmp_w1(w1):
    w = w1.reshape(2, CMP_STRIDE, HEAD_DIM, CMP_HIDDEN)
    z = jnp.zeros_like(w)
    h0 = jnp.concatenate([w, z], axis=-1)
    h1 = jnp.concatenate([z, w], axis=-1)
    both = jnp.stack([h0, h1], axis=2)
    return both.reshape(2, CMP_STRIDE * KV_WIDTH, N_KV_HEADS * CMP_HIDDEN).astype(_bf16)


def _expand_pe(pe):
    p = pe.reshape(2, CMP_STRIDE, 1, HEAD_DIM)
    return jnp.broadcast_to(p, (2, CMP_STRIDE, N_KV_HEADS, HEAD_DIM)).reshape(2, CMP_STRIDE * KV_WIDTH)


def _forward(x, g_mix_norm, w_in, g_q, g_k, cmp_pe, cmp_w1, cmp_w2, g_sgu, sp_w, sp_b, g_out, w_out,
             g_ffn_norm, w_ff1, w_ff2):
    B, S, _ = x.shape
    l = 0
    scale = float(HEAD_DIM ** -0.5 * np.log2(np.e))
    cos2, sin2, bd, ovt, masks = _constants(S)

    w_in_p = _pack_w_in(w_in[l])
    gq = (jnp.tile(g_q[l], N_ATTN_HEADS) * scale)[None, :]
    gksl = jnp.tile(g_k[l, 1], N_KV_HEADS)[None, :]
    gkwn = jnp.tile(g_k[l, 2], N_KV_HEADS)[None, :]
    gsgu = g_sgu[l].reshape(1, MIX_WIDTH)
    spw = sp_w[l].reshape(N_MIX_GROUPS // 2, 2, CHUNK, CHUNK)
    spw = jnp.transpose(spw, (0, 2, 1, 3)).reshape(N_MIX_GROUPS // 2, CHUNK, 2 * CHUNK).astype(_bf16)
    spb = jnp.repeat(sp_b[l].T, HEAD_DIM, axis=1)
    gout = g_out[l]

    qn, qr, kc, vc, ksl, vsl, kwn, vwn, gates, omix = _proj_call(
        x, g_mix_norm[l][None, :], w_in_p, gq, gksl, gkwn, cos2, sin2, bd, gsgu, spw, spb,
        gout[None, ATTN_WIDTH:])

    pe_rows = jnp.stack([_expand_pe(cmp_pe[l, 0]), _expand_pe(cmp_pe[l, 1])])
    w1x = jnp.stack([_expand_cmp_w1(cmp_w1[l, 0]), _expand_cmp_w1(cmp_w1[l, 1])])
    kcmp, vcmp = _compress_call(kc, vc, pe_rows, w1x, cmp_w2[l].astype(_bf16), g_k[l, 0][None, :])

    oattn = _attn_call(qn, qr, kcmp, vcmp, ksl, vsl, kwn, vwn, gates, ovt, masks,
                       jnp.broadcast_to(gout[:ATTN_WIDTH, None], (ATTN_WIDTH, ATTN_Q_TILE)))

    out = _ffn_call(x.reshape(B * S, D_MODEL), oattn.reshape(B * S, ATTN_WIDTH),
                    omix.reshape(B * S, MIX_WIDTH), w_out[l].astype(_bf16), g_ffn_norm[l][None, :],
                    w_ff1[l].astype(_bf16), w_ff2[l].astype(_bf16))
    return out.reshape(B, S, D_MODEL), (qn, qr, kc, vc, ksl, vsl, kwn, vwn, gates, omix, kcmp, vcmp, oattn)


def kernel(x, g_mix_norm, w_in, g_q, g_k, cmp_pe, cmp_w1, cmp_w2, g_sgu, sp_w, sp_b, g_out, w_out,
           g_ffn_norm, w_ff1, w_ff2):
    return _forward(x, g_mix_norm, w_in, g_q, g_k, cmp_pe, cmp_w1, cmp_w2, g_sgu, sp_w, sp_b, g_out, w_out,
                    g_ffn_norm, w_ff1, w_ff2)[0]
```

```python
import functools

import numpy as np
import jax
import jax.numpy as jnp
from jax import lax
from jax.experimental import pallas as pl
from jax.experimental.pallas import tpu as pltpu

D_MODEL = 1024
HEAD_DIM = 64
N_ATTN_HEADS = 8
N_MIX_GROUPS = 8
GQA_GROUP = 4
N_KV_HEADS = 2
ATTN_WIDTH = 512
MIX_WIDTH = 512
KV_WIDTH = 128
N_GATES = 3
CMP_BLOCK = 32
CMP_STRIDE = 16
CMP_HIDDEN = 256
SEL_BLOCK = 64
SEL_TOPN = 16
WINDOW = 512
CHUNK = 128
D_FF = 4 * D_MODEL
ROPE_THETA = 10000.0
EPS = 1e-6
FORCE_SCORE = 1e9

LANES = 128
NEG_BIG = -1e30
PROJ_TILE = 256
PROJ_UNIT = 512
ATTN_Q_TILE = 128
ATTN_SUBTILES = 2
TILE_SKEW = 2
SEL_CHUNK = 512
ATTN_UNIT = 512
STREAM_SKEW = 1
MASK_FULL, MASK_DIAG, MASK_LOW, MASK_NONE = 0, 1, 2, 3
FFN_TILE = 256
FF_CHUNK = 1024
VMEM_LIMIT = 56 * 1024 * 1024

_C_Q = 0
_C_KV = 512
_KV_KC, _KV_VC, _KV_KSL, _KV_VSL, _KV_KWN, _KV_VWN, _KV_GATE = (128 * n for n in range(7))
_KV_GROUP = 1024
_C_ZU = 1536
_C_ZV = 2048
_IN_COLS_PACKED = 2560
MXU_DIM = 256

_bf16 = jnp.bfloat16
_f32 = jnp.float32


def _dot(a, b):
    return jnp.dot(a, b, preferred_element_type=_f32)


def _group_mean(sq, bd_ref):
    cw = min(sq.shape[1], MXU_DIM)
    bd = bd_ref[0:cw, 0:cw]
    parts = [_dot(sq[:, c * cw:(c + 1) * cw].astype(_bf16), bd) for c in range(sq.shape[1] // cw)]
    return parts[0] if len(parts) == 1 else jnp.concatenate(parts, axis=1)


def _swap_halves(v):
    n = v.shape[1]
    lane = lax.broadcasted_iota(jnp.int32, v.shape, 1)
    return jnp.where((lane % HEAD_DIM) < HEAD_DIM // 2, pltpu.roll(v, n - HEAD_DIM // 2, 1),
                     pltpu.roll(v, HEAD_DIM // 2, 1))


def _proj_kernel(tiles_per_seq, x_ref, gmix_ref, w_ref, gq_ref, gksl_ref, gkwn_ref, cos_ref, sin_ref, bd_ref,
                 gsgu_ref, spw_ref, spb_ref, gomix_ref,
                 qn_ref, qr_ref, kc_ref, vc_ref, ksl_ref, vsl_ref, kwn_ref, vwn_ref, gate_ref, omix_ref,
                 h_even, h_odd, z_even, z_odd, regroup_scr):
    t = pl.program_id(0)

    @pl.when(t == 0)
    def _():
        h_odd[...] = jnp.zeros(h_odd.shape, _bf16)
        z_odd[...] = jnp.zeros(z_odd.shape, _f32)

    pl.when(t % 2 == 0)(functools.partial(
        _proj_step, tiles_per_seq, x_ref, gmix_ref, w_ref, gq_ref, gksl_ref, gkwn_ref, cos_ref, sin_ref, bd_ref,
        gsgu_ref, spw_ref, spb_ref, gomix_ref, qn_ref, qr_ref, kc_ref, vc_ref, ksl_ref, vsl_ref, kwn_ref,
        vwn_ref, gate_ref, omix_ref, h_even, h_odd, z_even, z_odd, regroup_scr))
    pl.when(t % 2 == 1)(functools.partial(
        _proj_step, tiles_per_seq, x_ref, gmix_ref, w_ref, gq_ref, gksl_ref, gkwn_ref, cos_ref, sin_ref, bd_ref,
        gsgu_ref, spw_ref, spb_ref, gomix_ref, qn_ref, qr_ref, kc_ref, vc_ref, ksl_ref, vsl_ref, kwn_ref,
        vwn_ref, gate_ref, omix_ref, h_odd, h_even, z_odd, z_even, regroup_scr))


def _proj_step(tiles_per_seq, x_ref, gmix_ref, w_ref, gq_ref, gksl_ref, gkwn_ref, cos_ref, sin_ref, bd_ref,
               gsgu_ref, spw_ref, spb_ref, gomix_ref,
               qn_ref, qr_ref, kc_ref, vc_ref, ksl_ref, vsl_ref, kwn_ref, vwn_ref, gate_ref, omix_ref,
               h_next, h_prev, z_next, z_prev, regroup_scr):
    tm = x_ref.shape[0]
    i = jnp.maximum(pl.program_id(0) - 2, 0) % tiles_per_seq
    cos = cos_ref[...]
    sin = sin_ref[...]

    def project_stream():
        h = h_prev[...]
        for lo in range(0, _IN_COLS_PACKED, PROJ_UNIT):
            z_next[:, lo:lo + PROJ_UNIT] = _dot(h, w_ref[:, lo:lo + PROJ_UNIT])
            yield

    def prenorm_stream():
        x = x_ref[...]
        ms = jnp.mean(x * x, axis=-1, keepdims=True)
        h_next[...] = (x * lax.rsqrt(ms + EPS) * gmix_ref[...]).astype(_bf16)
        yield

    def query_stream():
        zq = z_prev[:, _C_Q:_C_Q + ATTN_WIDTH]
        qn = zq * lax.rsqrt(_group_mean(zq * zq, bd_ref) + EPS) * gq_ref[...]
        yield
        cos4 = jnp.concatenate([cos] * 4, axis=1)
        sin4 = jnp.concatenate([sin] * 4, axis=1)
        qr = qn * cos4 + _swap_halves(qn) * sin4
        yield
        qn_t, qr_t = qn.T, qr.T
        for hq in range(N_ATTN_HEADS):
            sl = slice(hq * HEAD_DIM, (hq + 1) * HEAD_DIM)
            qn_ref[0, hq] = qn_t[sl].astype(_bf16)
            qr_ref[0, hq] = qr_t[sl].astype(_bf16)

    def kv_stream():
        def segment(off):
            return z_prev[:, _C_KV + off:_C_KV + off + KV_WIDTH]

        for n, (off, out_ref) in enumerate(((_KV_KC, kc_ref), (_KV_VC, vc_ref))):
            regroup_scr[n] = segment(off)
            for tok in range(CMP_STRIDE):
                every_16th = regroup_scr[n, pl.ds(tok, tm // CMP_STRIDE, stride=CMP_STRIDE), :]
                out_ref[0, :, tok * KV_WIDTH:(tok + 1) * KV_WIDTH] = every_16th.astype(_bf16)
        gate_ref[0] = jax.nn.sigmoid(segment(_KV_GATE))

        ones_block = jnp.where(lax.broadcasted_iota(jnp.int32, (HEAD_DIM, LANES), 0) == 0, 1.0, 0.0)
        for off, out_ref in ((_KV_VSL, vsl_ref), (_KV_VWN, vwn_ref)):
            zv2 = segment(off)
            for tt in range(tm // LANES):
                zt = zv2[tt * LANES:(tt + 1) * LANES, :].T
                for hh in range(N_KV_HEADS):
                    out_ref[0, hh, tt] = jnp.concatenate(
                        [zt[hh * HEAD_DIM:(hh + 1) * HEAD_DIM], ones_block], axis=0).astype(_bf16)
            yield

        lane = lax.broadcasted_iota(jnp.int32, (tm, LANES), 1)
        pos = i * tm + lax.broadcasted_iota(jnp.int32, (tm, LANES), 0)
        blk_onehot = jnp.where(lane - HEAD_DIM == pos // SEL_BLOCK, 1.0, 0.0)
        for off, g_ref, out_ref, extra in ((_KV_KSL, gksl_ref, ksl_ref, blk_onehot),
                                           (_KV_KWN, gkwn_ref, kwn_ref, jnp.zeros((tm, LANES), _f32))):
            zk = segment(off)
            kn = zk * lax.rsqrt(_group_mean(zk * zk, bd_ref) + EPS) * g_ref[...]
            kr = kn * cos + _swap_halves(kn) * sin
            out_ref[0, 0] = jnp.where(lane < HEAD_DIM, kr, extra).astype(_bf16)
            out_ref[0, 1] = jnp.where(lane < HEAD_DIM, pltpu.roll(kr, HEAD_DIM, 1), extra).astype(_bf16)
            yield

    row = lax.broadcasted_iota(jnp.int32, (CHUNK, 2 * CHUNK), 0)
    colw = lax.broadcasted_iota(jnp.int32, (CHUNK, 2 * CHUNK), 1) % CHUNK
    causal_w = colw <= row
    lane_c = lax.broadcasted_iota(jnp.int32, (CHUNK, LANES), 1)

    def gmlp_stream(c):
        rows = slice(c * CHUNK, (c + 1) * CHUNK)
        zu = jax.nn.gelu(z_prev[rows, _C_ZU:_C_ZU + MIX_WIDTH])
        yield
        zv = jax.nn.gelu(z_prev[rows, _C_ZV:_C_ZV + MIX_WIDTH])
        yield
        vn = zv * lax.rsqrt(_group_mean(zv * zv, bd_ref) + EPS) * gsgu_ref[...]
        yield
        sv_parts = []
        for p in range(N_MIX_GROUPS // 2):
            vp = vn[:, p * LANES:(p + 1) * LANES]
            rhs = jnp.concatenate([jnp.where(lane_c < HEAD_DIM, vp, 0.0),
                                   jnp.where(lane_c < HEAD_DIM, 0.0, vp)], axis=0).astype(_bf16)
            w_pair = jnp.where(causal_w, spw_ref[p], jnp.zeros((), _bf16))
            sv_parts.append(_dot(w_pair, rhs))
        yield
        sv = jnp.concatenate(sv_parts, axis=1) + spb_ref[...]
        om = zu * sv
        oms = jnp.mean(om * om, axis=-1, keepdims=True)
        omix_ref[0, rows] = (om * lax.rsqrt(oms + EPS) * gomix_ref[...]).astype(_bf16)

    _interleave([project_stream(), query_stream(), kv_stream()] + [gmlp_stream(c) for c in range(tm // CHUNK)]
                + [prenorm_stream()])


def _proj_call(x, gmix, w_in_p, gq, gksl, gkwn, cos, sin, bd, gsgu, spw, spb, gomix):
    B, S, _ = x.shape
    tm = PROJ_TILE
    nt = S // tm
    n_tiles = B * nt
    const2 = lambda t: (0, 0)
    const3 = lambda t: (0, 0, 0)

    def done(t):
        return jnp.maximum(t - 2, 0)

    tok3 = lambda t: (done(t) // nt, done(t) % nt, 0)
    head4 = lambda t: (done(t) // nt, 0, done(t) % nt, 0)
    out_shape = (
        jax.ShapeDtypeStruct((B, N_ATTN_HEADS, HEAD_DIM, S), _bf16),
        jax.ShapeDtypeStruct((B, N_ATTN_HEADS, HEAD_DIM, S), _bf16),
        jax.ShapeDtypeStruct((B, S // CMP_STRIDE, CMP_STRIDE * KV_WIDTH), _bf16),
        jax.ShapeDtypeStruct((B, S // CMP_STRIDE, CMP_STRIDE * KV_WIDTH), _bf16),
        jax.ShapeDtypeStruct((B, N_KV_HEADS, S, LANES), _bf16),
        jax.ShapeDtypeStruct((B, N_KV_HEADS, S // LANES, LANES, LANES), _bf16),
        jax.ShapeDtypeStruct((B, N_KV_HEADS, S, LANES), _bf16),
        jax.ShapeDtypeStruct((B, N_KV_HEADS, S // LANES, LANES, LANES), _bf16),
        jax.ShapeDtypeStruct((B, S, LANES), _f32),
        jax.ShapeDtypeStruct((B, S, MIX_WIDTH), _bf16),
    )
    q_spec = pl.BlockSpec((1, N_ATTN_HEADS, HEAD_DIM, tm), lambda t: (done(t) // nt, 0, 0, done(t) % nt))
    k_spec = pl.BlockSpec((1, N_KV_HEADS, tm, LANES), head4)
    t_spec = pl.BlockSpec((1, tm, LANES), tok3)
    g_spec = pl.BlockSpec((1, tm // CMP_STRIDE, CMP_STRIDE * KV_WIDTH), tok3)
    vt_spec = pl.BlockSpec((1, N_KV_HEADS, tm // LANES, LANES, LANES),
                           lambda t: (done(t) // nt, 0, done(t) % nt, 0, 0))
    return pl.pallas_call(
        functools.partial(_proj_kernel, nt),
        grid=(n_tiles + 2,),
        in_specs=[
            pl.BlockSpec((tm, D_MODEL), lambda t: (jnp.minimum(t, n_tiles - 1), 0)),
            pl.BlockSpec((1, D_MODEL), const2),
            pl.BlockSpec((D_MODEL, _IN_COLS_PACKED), const2),
            pl.BlockSpec((1, ATTN_WIDTH), const2),
            pl.BlockSpec((1, KV_WIDTH), const2),
            pl.BlockSpec((1, KV_WIDTH), const2),
            pl.BlockSpec((tm, LANES), lambda t: (done(t) % nt, 0)),
            pl.BlockSpec((tm, LANES), lambda t: (done(t) % nt, 0)),
            pl.BlockSpec((MXU_DIM, MXU_DIM), const2),
            pl.BlockSpec((1, MIX_WIDTH), const2),
            pl.BlockSpec((N_MIX_GROUPS // 2, CHUNK, 2 * CHUNK), const3),
            pl.BlockSpec((CHUNK, MIX_WIDTH), const2),
            pl.BlockSpec((1, MIX_WIDTH), const2),
        ],
        out_specs=(q_spec, q_spec, g_spec, g_spec, k_spec, vt_spec, k_spec, vt_spec, t_spec,
                   pl.BlockSpec((1, tm, MIX_WIDTH), tok3)),
        out_shape=out_shape,
        scratch_shapes=[pltpu.VMEM((tm, D_MODEL), _bf16), pltpu.VMEM((tm, D_MODEL), _bf16),
                        pltpu.VMEM((tm, _IN_COLS_PACKED), _f32), pltpu.VMEM((tm, _IN_COLS_PACKED), _f32),
                        pltpu.VMEM((2, tm, KV_WIDTH), _f32)],
        compiler_params=pltpu.CompilerParams(
            dimension_semantics=("arbitrary",), vmem_limit_bytes=VMEM_LIMIT),
        name="proj",
    )(x.reshape(B * S, D_MODEL), gmix, w_in_p, gq, gksl, gkwn, cos, sin, bd, gsgu, spw, spb, gomix)


def _compress_kernel(kc_ref, vc_ref, pe_ref, w1_ref, w2_ref, gk_ref, kcmp_ref, vcmp_ref):
    n_rows = kc_ref.shape[1]
    zero_half = jnp.zeros((n_rows, HEAD_DIM), _f32)
    for t, (src_ref, out_ref) in enumerate(((kc_ref, kcmp_ref), (vc_ref, vcmp_ref))):
        g = src_ref[0].astype(_f32)
        first = _dot((g + pe_ref[t, 0:1]).astype(_bf16), w1_ref[t, 0])
        second = _dot((g + pe_ref[t, 1:2]).astype(_bf16), w1_ref[t, 1])
        hid = jax.nn.gelu(first + pltpu.roll(second, n_rows - 1, 0)).astype(_bf16)
        heads = [_dot(hid[:, hh * CMP_HIDDEN:(hh + 1) * CMP_HIDDEN], w2_ref[t]) for hh in range(N_KV_HEADS)]
        if t == 0:
            heads = [c * lax.rsqrt(jnp.mean(c * c, axis=-1, keepdims=True) + EPS) * gk_ref[...] for c in heads]
            for hh in range(N_KV_HEADS):
                out_ref[0, hh] = jnp.concatenate([heads[hh], zero_half], axis=1).astype(_bf16)
        else:
            for hh in range(N_KV_HEADS):
                vt = jnp.concatenate([heads[hh], zero_half], axis=1).T
                out_ref[0, hh] = vt.astype(_bf16)


def _compress_call(kc_g, vc_g, pe_rows, w1x, w2, gk0):
    B, n_rows, width = kc_g.shape
    return pl.pallas_call(
        _compress_kernel,
        grid=(B,),
        in_specs=[
            pl.BlockSpec((1, n_rows, width), lambda b: (b, 0, 0)),
            pl.BlockSpec((1, n_rows, width), lambda b: (b, 0, 0)),
            pl.BlockSpec((2, 2, width), lambda b: (0, 0, 0)),
            pl.BlockSpec((2, 2, width, N_KV_HEADS * CMP_HIDDEN), lambda b: (0, 0, 0, 0)),
            pl.BlockSpec((2, CMP_HIDDEN, HEAD_DIM), lambda b: (0, 0, 0)),
            pl.BlockSpec((1, HEAD_DIM), lambda b: (0, 0)),
        ],
        out_specs=(pl.BlockSpec((1, N_KV_HEADS, n_rows, LANES), lambda b: (b, 0, 0, 0)),
                   pl.BlockSpec((1, N_KV_HEADS, n_rows, LANES), lambda b: (b, 0, 0, 0))),
        out_shape=(jax.ShapeDtypeStruct((B, N_KV_HEADS, n_rows, LANES), _bf16),
                   jax.ShapeDtypeStruct((B, N_KV_HEADS, n_rows, LANES), _bf16)),
        compiler_params=pltpu.CompilerParams(
            dimension_semantics=("parallel",), vmem_limit_bytes=VMEM_LIMIT),
        name="compress",
    )(kc_g, vc_g, pe_rows, w1x, w2, gk0)


def _interleave(streams):
    live = list(streams)
    while live:
        for g in list(live):
            try:
                next(g)
            except StopIteration:
                live.remove(g)


def _delayed(stream, units):
    for _ in range(units):
        yield
    yield from stream


def _attn_kernel(qn_ref, qr_ref, kcmp_ref, vcmpt_ref, ksl_ref, vslt_ref, kwn_ref, vwnt_ref, gate_ref,
                 ovt_ref, mask_ref, gout_ref, o_ref):
    refs = (qn_ref, qr_ref, kcmp_ref, vcmpt_ref, ksl_ref, vslt_ref, kwn_ref, vwnt_ref, gate_ref,
            ovt_ref, mask_ref, gout_ref, o_ref)
    S = ksl_ref.shape[2]

    def step_body(c):
        tiles = [_attn_tile(sub, c, *refs) for sub in range(ATTN_SUBTILES)]
        _interleave([_delayed(stream, sub * TILE_SKEW) for sub, (streams, _) in enumerate(tiles)
                     for stream in streams])
        for _, finish in tiles:
            finish()

    cls = (pl.program_id(1) * ATTN_SUBTILES * ATTN_Q_TILE) // SEL_CHUNK
    for c in range(S // SEL_CHUNK):
        pl.when(cls == c)(functools.partial(step_body, c))


def _attn_tile(sub, c, qn_ref, qr_ref, kcmp_ref, vcmpt_ref, ksl_ref, vslt_ref, kwn_ref, vwnt_ref, gate_ref,
               ovt_ref, mask_ref, gout_ref, o_ref):
    tq = ATTN_Q_TILE
    sub_cols = slice(sub * tq, (sub + 1) * tq)
    rows = GQA_GROUP * tq
    S = ksl_ref.shape[2]
    n_cmp_pad = kcmp_ref.shape[2]
    n_sel = S // SEL_BLOCK
    win_tiles = WINDOW // tq
    unit_tiles = ATTN_UNIT // tq
    i = pl.program_id(1) * ATTN_SUBTILES + sub
    q0 = i * tq

    qpos = q0 + lax.broadcasted_iota(jnp.int32, (1, rows), 1) % tq
    blk = lax.broadcasted_iota(jnp.int32, (n_sel, tq), 0)
    tpos = q0 + lax.broadcasted_iota(jnp.int32, (n_sel, tq), 1)
    gates_t = gate_ref[0, sub_cols, :].T
    zero_rows = jnp.zeros((HEAD_DIM, rows), _bf16)

    def stacked_heads_t(ref, h):
        return jnp.concatenate([ref[0, GQA_GROUP * h + g, :, sub_cols] for g in range(GQA_GROUP)], axis=1)

    def tile_mask(key_tile, low_edge=False):
        kind = jnp.where(key_tile < i, MASK_FULL, jnp.where(key_tile == i, MASK_DIAG, MASK_NONE))
        if low_edge:
            kind = jnp.where(i >= win_tiles, MASK_LOW, kind)
        return mask_ref[kind]

    def attend(out, h, k_ref, vt_ref, q_t, first_tile, tile_masks):
        n_tiles = len(tile_masks)
        scores, mx = [], None
        for u0 in range(0, n_tiles, unit_tiles):
            nt = min(unit_tiles, n_tiles - u0)
            if isinstance(first_tile, int):
                keys = slice((first_tile + u0) * tq, (first_tile + u0 + nt) * tq)
            else:
                keys = pl.ds(pl.multiple_of((first_tile + u0) * tq, tq), nt * tq)
            sc = _dot(k_ref[0, h, keys, :], q_t)
            if any(tile_masks[u0 + t] is not None for t in range(nt)):
                sc = jnp.concatenate(
                    [sc[t * tq:(t + 1) * tq] if tile_masks[u0 + t] is None
                     else sc[t * tq:(t + 1) * tq] + tile_masks[u0 + t]() for t in range(nt)], axis=0)
            scores.append(sc)
            cm = jnp.max(sc, axis=0, keepdims=True)
            mx = cm if mx is None else jnp.maximum(mx, cm)
            yield
        probs = []
        for sc in scores:
            probs.append(jnp.exp2(sc - mx).astype(_bf16))
            yield
        vt = jnp.concatenate([vt_ref[0, h, first_tile + t] for t in range(n_tiles)], axis=1)
        acc = _dot(vt, jnp.concatenate(probs, axis=0))
        out[h] = acc[0:HEAD_DIM] / acc[HEAD_DIM:HEAD_DIM + 1]
        yield

    def window_stream(out, h, c):
        first_tile = jnp.maximum(i - win_tiles, 0)
        q_t = jnp.concatenate([stacked_heads_t(qr_ref, h), zero_rows], axis=0)
        if c * SEL_CHUNK >= WINDOW:
            masks = ([lambda: mask_ref[MASK_LOW]] + [None] * (win_tiles - 1) + [lambda: mask_ref[MASK_DIAG]])
        else:
            masks = [functools.partial(tile_mask, first_tile + t, low_edge=(t == 0))
                     for t in range(win_tiles + 1)]
        yield from attend(out, h, kwn_ref, vwnt_ref, q_t, first_tile, masks)

    def compressed_and_selected_stream(o_cmps, o_sels, h, c):
        qn = stacked_heads_t(qn_ref, h)
        qr = stacked_heads_t(qr_ref, h)

        s = _dot(kcmp_ref[0, h], jnp.concatenate([qn, zero_rows], axis=0))
        n_idx = lax.broadcasted_iota(jnp.int32, (n_cmp_pad, 1), 0)
        valid_c = n_idx * CMP_STRIDE + (CMP_BLOCK - 1) <= qpos
        s = jnp.where(valid_c, s, NEG_BIG)
        m = jnp.max(s, axis=0, keepdims=True)
        e = jnp.where(valid_c, jnp.exp2(s - m), 0.0)
        p_c = e / jnp.maximum(jnp.sum(e, axis=0, keepdims=True), 1e-20)
        o_cmps[h] = _dot(vcmpt_ref[0, h], p_c.astype(_bf16))[0:HEAD_DIM]
        yield

        eligible = blk * SEL_BLOCK <= tpos
        if SEL_CHUNK * (c + 1) <= SEL_TOPN * SEL_BLOCK:
            chosen = eligible
        else:
            p_sum = p_c[:, 0:tq] + p_c[:, tq:2 * tq] + p_c[:, 2 * tq:3 * tq] + p_c[:, 3 * tq:4 * tq]
            p_hi = p_sum.astype(_bf16)
            p_lo = (p_sum - p_hi.astype(_f32)).astype(_bf16)
            imp = _dot(ovt_ref[...], p_hi) + _dot(ovt_ref[...], p_lo)
            cur = tpos // SEL_BLOCK
            forced = (blk == 0) | (blk == cur) | (blk == cur - 1)
            score = jnp.where(forced, FORCE_SCORE, jnp.where(eligible, imp, -jnp.inf))
            rank = jnp.zeros((n_sel, tq), _f32)
            for jp in range(n_sel):
                other = score[jp:jp + 1, :]
                tie = jnp.where(blk > jp, 1.0, 0.0)
                rank = rank + jnp.where(other > score, 1.0, jnp.where(other == score, tie, 0.0))
            chosen = (rank < float(SEL_TOPN)) & eligible
        sel_bias = jnp.where(chosen, 0.0, NEG_BIG).astype(_bf16)
        yield

        q_t = jnp.concatenate([qr, jnp.concatenate([sel_bias] * GQA_GROUP, axis=1),
                               jnp.zeros((LANES - HEAD_DIM - n_sel, rows), _bf16)], axis=0)
        n_tiles = (SEL_CHUNK // tq) * (c + 1)
        first_edge = n_tiles - SEL_CHUNK // tq
        masks = [None if t < first_edge else functools.partial(tile_mask, t) for t in range(n_tiles)]
        yield from attend(o_sels, h, ksl_ref, vslt_ref, q_t, 0, masks)

    o_wins, o_cmps, o_sels = {}, {}, {}
    streams = [
        window_stream(o_wins, 0, c),
        compressed_and_selected_stream(o_cmps, o_sels, 0, c),
        _delayed(window_stream(o_wins, 1, c), STREAM_SKEW),
        _delayed(compressed_and_selected_stream(o_cmps, o_sels, 1, c), STREAM_SKEW),
    ]

    def finish():
        slabs = []
        for h in range(N_KV_HEADS):
            for g in range(GQA_GROUP):
                hq = GQA_GROUP * h + g
                cols = slice(g * tq, (g + 1) * tq)
                slabs.append(gates_t[hq:hq + 1] * o_cmps[h][:, cols]
                             + gates_t[N_ATTN_HEADS + hq:N_ATTN_HEADS + hq + 1] * o_sels[h][:, cols]
                             + gates_t[2 * N_ATTN_HEADS + hq:2 * N_ATTN_HEADS + hq + 1] * o_wins[h][:, cols])
        ot = jnp.concatenate(slabs, axis=0)
        ms = jnp.mean(ot * ot, axis=0, keepdims=True)
        o_ref[0, sub_cols, :] = (ot * lax.rsqrt(ms + EPS) * gout_ref[...]).T.astype(_bf16)

    return streams, finish


def _attn_call(qn_t, qr_t, kcmp, vcmp_t, ksl, vsl_t, kwn, vwn_t, gates, ovt, masks, gout_attn):
    B, _, _, S = qn_t.shape
    tq = ATTN_Q_TILE
    assert tq == LANES and S % SEL_CHUNK == 0 and SEL_CHUNK % tq == 0 and WINDOW % tq == 0
    assert ATTN_UNIT % tq == 0 and kcmp.shape[2] == LANES
    step = tq * ATTN_SUBTILES
    assert SEL_CHUNK % step == 0
    n_cmp_pad = kcmp.shape[2]
    n_sel = S // SEL_BLOCK
    rows = GQA_GROUP * tq
    per_b4 = lambda b, i: (b, 0, 0, 0)
    const2 = lambda b, i: (0, 0)
    vt_spec = pl.BlockSpec((1, N_KV_HEADS, S // LANES, LANES, LANES), lambda b, i: (b, 0, 0, 0, 0))
    return pl.pallas_call(
        _attn_kernel,
        grid=(B, S // step),
        in_specs=[
            pl.BlockSpec((1, N_ATTN_HEADS, HEAD_DIM, step), lambda b, i: (b, 0, 0, i)),
            pl.BlockSpec((1, N_ATTN_HEADS, HEAD_DIM, step), lambda b, i: (b, 0, 0, i)),
            pl.BlockSpec((1, N_KV_HEADS, n_cmp_pad, LANES), per_b4),
            pl.BlockSpec((1, N_KV_HEADS, n_cmp_pad, LANES), per_b4),
            pl.BlockSpec((1, N_KV_HEADS, S, LANES), per_b4),
            vt_spec,
            pl.BlockSpec((1, N_KV_HEADS, S, LANES), per_b4),
            vt_spec,
            pl.BlockSpec((1, step, LANES), lambda b, i: (b, i, 0)),
            pl.BlockSpec((n_sel, n_cmp_pad), const2),
            pl.BlockSpec((4, tq, rows), lambda b, i: (0, 0, 0)),
            pl.BlockSpec((ATTN_WIDTH, tq), const2),
        ],
        out_specs=pl.BlockSpec((1, step, ATTN_WIDTH), lambda b, i: (b, i, 0)),
        out_shape=jax.ShapeDtypeStruct((B, S, ATTN_WIDTH), _bf16),
        compiler_params=pltpu.CompilerParams(
            dimension_semantics=("parallel", "arbitrary"), vmem_limit_bytes=VMEM_LIMIT),
        name="attn",
    )(qn_t, qr_t, kcmp, vcmp_t, ksl, vsl_t, kwn, vwn_t, gates, ovt, masks, gout_attn)


def _ffn_kernel(x_ref, oa_ref, om_ref, wo_ref, gffn_ref, w1_ref, w2_ref, out_ref, act_scr):
    x2 = (x_ref[...] + _dot(oa_ref[...], wo_ref[0:ATTN_WIDTH, :])
          + _dot(om_ref[...], wo_ref[ATTN_WIDTH:ATTN_WIDTH + MIX_WIDTH, :]))
    ms = jnp.mean(x2 * x2, axis=-1, keepdims=True)
    h = (x2 * lax.rsqrt(ms + EPS) * gffn_ref[...]).astype(_bf16)
    for c in range(D_FF // FF_CHUNK):
        cols = slice(c * FF_CHUNK, (c + 1) * FF_CHUNK)
        a = jnp.maximum(_dot(h, w1_ref[:, cols]), 0.0)
        act_scr[:, cols] = (a * a).astype(_bf16)
    out_ref[...] = x2 + _dot(act_scr[...], w2_ref[...])


def _ffn_call(x2d, oa, om, wo, gffn, w1, w2):
    T = x2d.shape[0]
    tm = FFN_TILE
    tok = lambda i: (i, 0)
    const = lambda i: (0, 0)
    return pl.pallas_call(
        _ffn_kernel,
        grid=(T // tm,),
        in_specs=[
            pl.BlockSpec((tm, D_MODEL), tok),
            pl.BlockSpec((tm, ATTN_WIDTH), tok),
            pl.BlockSpec((tm, MIX_WIDTH), tok),
            pl.BlockSpec((D_MODEL, D_MODEL), const),
            pl.BlockSpec((1, D_MODEL), const),
            pl.BlockSpec((D_MODEL, D_FF), const),
            pl.BlockSpec((D_FF, D_MODEL), const),
        ],
        out_specs=pl.BlockSpec((tm, D_MODEL), tok),
        out_shape=jax.ShapeDtypeStruct((T, D_MODEL), _f32),
        scratch_shapes=[pltpu.VMEM((tm, D_FF), _bf16)],
        compiler_params=pltpu.CompilerParams(
            dimension_semantics=("parallel",), vmem_limit_bytes=VMEM_LIMIT),
        name="ffn",
    )(x2d, oa, om, wo, gffn, w1, w2)


def _constants(S):
    half = HEAD_DIM // 2
    inv = ROPE_THETA ** (-jnp.arange(half, dtype=_f32) / half)
    ang = jnp.arange(S, dtype=_f32)[:, None] * inv[None, :]
    cos = jnp.cos(ang)
    sin = jnp.sin(ang)
    cos2 = jnp.concatenate([cos, cos, cos, cos], axis=1)
    sin2 = jnp.concatenate([-sin, sin, -sin, sin], axis=1)
    mlane = np.arange(MXU_DIM)
    bd = (mlane[:, None] // HEAD_DIM == mlane[None, :] // HEAD_DIM).astype(np.float32) / HEAD_DIM
    n_cmp = (S - CMP_BLOCK) // CMP_STRIDE + 1
    n_sel = S // SEL_BLOCK
    n_cmp_pad = S // CMP_STRIDE
    cmp_start = np.arange(n_cmp)[:, None] * CMP_STRIDE
    sel_start = np.arange(n_sel)[None, :] * SEL_BLOCK
    overlap = np.clip(np.minimum(cmp_start + CMP_BLOCK, sel_start + SEL_BLOCK)
                      - np.maximum(cmp_start, sel_start), 0, None).astype(np.float32) / CMP_BLOCK
    ovt = np.zeros((n_sel, n_cmp_pad), np.float32)
    ovt[:, :n_cmp] = overlap.T
    qq = (np.arange(GQA_GROUP * ATTN_Q_TILE) % ATTN_Q_TILE)[None, :]
    kk = np.arange(ATTN_Q_TILE)[:, None]
    masks = np.zeros((4, ATTN_Q_TILE, GQA_GROUP * ATTN_Q_TILE), np.float32)
    masks[MASK_DIAG] = np.where(kk <= qq, 0.0, NEG_BIG)
    masks[MASK_LOW] = np.where(kk > qq, 0.0, NEG_BIG)
    masks[MASK_NONE] = NEG_BIG
    return cos2, sin2, jnp.asarray(bd, _bf16), jnp.asarray(ovt, _bf16), jnp.asarray(masks, _f32)


def _pack_w_in(w):
    w = w.astype(_bf16)
    gate0 = ATTN_WIDTH + 6 * KV_WIDTH
    n_g = N_GATES * N_ATTN_HEADS
    gates = w[:, gate0:gate0 + n_g].reshape(D_MODEL, N_ATTN_HEADS, N_GATES)
    gates = jnp.transpose(gates, (0, 2, 1)).reshape(D_MODEL, n_g)
    pad = jnp.zeros((D_MODEL, _C_KV + _KV_GROUP - gate0 - n_g), w.dtype)
    return jnp.concatenate([w[:, :gate0], gates, pad, w[:, gate0 + n_g:]], axis=1)


def _expand_cmp_w1(w1):
    w = w1.reshape(2, CMP_STRIDE, HEAD_DIM, CMP_HIDDEN)
    z = jnp.zeros_like(w)
    h0 = jnp.concatenate([w, z], axis=-1)
    h1 = jnp.concatenate([z, w], axis=-1)
    both = jnp.stack([h0, h1], axis=2)
    return both.reshape(2, CMP_STRIDE * KV_WIDTH, N_KV_HEADS * CMP_HIDDEN).astype(_bf16)


def _expand_pe(pe):
    p = pe.reshape(2, CMP_STRIDE, 1, HEAD_DIM)
    return jnp.broadcast_to(p, (2, CMP_STRIDE, N_KV_HEADS, HEAD_DIM)).reshape(2, CMP_STRIDE * KV_WIDTH)


def _forward(x, g_mix_norm, w_in, g_q, g_k, cmp_pe, cmp_w1, cmp_w2, g_sgu, sp_w, sp_b, g_out, w_out,
             g_ffn_norm, w_ff1, w_ff2):
    B, S, _ = x.shape
    l = 0
    scale = float(HEAD_DIM ** -0.5 * np.log2(np.e))
    cos2, sin2, bd, ovt, masks = _constants(S)

    w_in_p = _pack_w_in(w_in[l])
    gq = (jnp.tile(g_q[l], N_ATTN_HEADS) * scale)[None, :]
    gksl = jnp.tile(g_k[l, 1], N_KV_HEADS)[None, :]
    gkwn = jnp.tile(g_k[l, 2], N_KV_HEADS)[None, :]
    gsgu = g_sgu[l].reshape(1, MIX_WIDTH)
    spw = sp_w[l].reshape(N_MIX_GROUPS // 2, 2, CHUNK, CHUNK)
    spw = jnp.transpose(spw, (0, 2, 1, 3)).reshape(N_MIX_GROUPS // 2, CHUNK, 2 * CHUNK).astype(_bf16)
    spb = jnp.repeat(sp_b[l].T, HEAD_DIM, axis=1)
    gout = g_out[l]

    qn, qr, kc, vc, ksl, vsl, kwn, vwn, gates, omix = _proj_call(
        x, g_mix_norm[l][None, :], w_in_p, gq, gksl, gkwn, cos2, sin2, bd, gsgu, spw, spb,
        gout[None, ATTN_WIDTH:])

    pe_rows = jnp.stack([_expand_pe(cmp_pe[l, 0]), _expand_pe(cmp_pe[l, 1])])
    w1x = jnp.stack([_expand_cmp_w1(cmp_w1[l, 0]), _expand_cmp_w1(cmp_w1[l, 1])])
    kcmp, vcmp = _compress_call(kc, vc, pe_rows, w1x, cmp_w2[l].astype(_bf16), g_k[l, 0][None, :])

    oattn = _attn_call(qn, qr, kcmp, vcmp, ksl, vsl, kwn, vwn, gates, ovt, masks,
                       jnp.broadcast_to(gout[:ATTN_WIDTH, None], (ATTN_WIDTH, ATTN_Q_TILE)))

    out = _ffn_call(x.reshape(B * S, D_MODEL), oattn.reshape(B * S, ATTN_WIDTH),
                    omix.reshape(B * S, MIX_WIDTH), w_out[l].astype(_bf16), g_ffn_norm[l][None, :],
                    w_ff1[l].astype(_bf16), w_ff2[l].astype(_bf16))
    return out.reshape(B, S, D_MODEL), (qn, qr, kc, vc, ksl, vsl, kwn, vwn, gates, omix, kcmp, vcmp, oattn)


def kernel(x, g_mix_norm, w_in, g_q, g_k, cmp_pe, cmp_w1, cmp_w2, g_sgu, sp_w, sp_b, g_out, w_out,
           g_ffn_norm, w_ff1, w_ff2):
    return _forward(x, g_mix_norm, w_in, g_q, g_k, cmp_pe, cmp_w1, cmp_w2, g_sgu, sp_w, sp_b, g_out, w_out,
                    g_ffn_norm, w_ff1, w_ff2)[0]
```

```python
import functools

import numpy as np
import jax
import jax.numpy as jnp
from jax import lax
from jax.experimental import pallas as pl
from jax.experimental.pallas import tpu as pltpu

D_MODEL = 1024
HEAD_DIM = 64
N_ATTN_HEADS = 8
N_MIX_GROUPS = 8
GQA_GROUP = 4
N_KV_HEADS = 2
ATTN_WIDTH = 512
MIX_WIDTH = 512
KV_WIDTH = 128
N_GATES = 3
CMP_BLOCK = 32
CMP_STRIDE = 16
CMP_HIDDEN = 256
SEL_BLOCK = 64
SEL_TOPN = 16
WINDOW = 512
CHUNK = 128
D_FF = 4 * D_MODEL
ROPE_THETA = 10000.0
EPS = 1e-6
FORCE_SCORE = 1e9

LANES = 128
NEG_BIG = -1e30
PROJ_TILE = 256
PROJ_UNIT = 512
ATTN_Q_TILE = 128
ATTN_SUBTILES = 2
TILE_SKEW = 2
SEL_CHUNK = 512
ATTN_UNIT = 512
STREAM_SKEW = 1
MASK_FULL, MASK_DIAG, MASK_LOW, MASK_NONE = 0, 1, 2, 3
FFN_TILE = 512
FF_CHUNK = 1024
VMEM_LIMIT = 56 * 1024 * 1024

_C_Q = 0
_C_KV = 512
_KV_KC, _KV_VC, _KV_KSL, _KV_VSL, _KV_KWN, _KV_VWN, _KV_GATE = (128 * n for n in range(7))
_KV_GROUP = 1024
_C_ZU = 1536
_C_ZV = 2048
_IN_COLS_PACKED = 2560
MXU_DIM = 256

_bf16 = jnp.bfloat16
_f32 = jnp.float32


def _dot(a, b):
    return jnp.dot(a, b, preferred_element_type=_f32)


def _group_mean(sq, bd_ref):
    cw = min(sq.shape[1], MXU_DIM)
    bd = bd_ref[0:cw, 0:cw]
    parts = [_dot(sq[:, c * cw:(c + 1) * cw].astype(_bf16), bd) for c in range(sq.shape[1] // cw)]
    return parts[0] if len(parts) == 1 else jnp.concatenate(parts, axis=1)


def _swap_halves(v):
    n = v.shape[1]
    lane = lax.broadcasted_iota(jnp.int32, v.shape, 1)
    return jnp.where((lane % HEAD_DIM) < HEAD_DIM // 2, pltpu.roll(v, n - HEAD_DIM // 2, 1),
                     pltpu.roll(v, HEAD_DIM // 2, 1))


def _proj_kernel(tiles_per_seq, x_ref, gmix_ref, w_ref, gq_ref, gksl_ref, gkwn_ref, cos_ref, sin_ref, bd_ref,
                 gsgu_ref, spw_ref, spb_ref, gomix_ref,
                 qn_ref, qr_ref, kc_ref, vc_ref, ksl_ref, vsl_ref, kwn_ref, vwn_ref, gate_ref, omix_ref,
                 h_even, h_odd, z_even, z_odd, regroup_scr):
    t = pl.program_id(0)

    @pl.when(t == 0)
    def _():
        h_odd[...] = jnp.zeros(h_odd.shape, _bf16)
        z_odd[...] = jnp.zeros(z_odd.shape, _f32)

    pl.when(t % 2 == 0)(functools.partial(
        _proj_step, tiles_per_seq, x_ref, gmix_ref, w_ref, gq_ref, gksl_ref, gkwn_ref, cos_ref, sin_ref, bd_ref,
        gsgu_ref, spw_ref, spb_ref, gomix_ref, qn_ref, qr_ref, kc_ref, vc_ref, ksl_ref, vsl_ref, kwn_ref,
        vwn_ref, gate_ref, omix_ref, h_even, h_odd, z_even, z_odd, regroup_scr))
    pl.when(t % 2 == 1)(functools.partial(
        _proj_step, tiles_per_seq, x_ref, gmix_ref, w_ref, gq_ref, gksl_ref, gkwn_ref, cos_ref, sin_ref, bd_ref,
        gsgu_ref, spw_ref, spb_ref, gomix_ref, qn_ref, qr_ref, kc_ref, vc_ref, ksl_ref, vsl_ref, kwn_ref,
        vwn_ref, gate_ref, omix_ref, h_odd, h_even, z_odd, z_even, regroup_scr))


def _proj_step(tiles_per_seq, x_ref, gmix_ref, w_ref, gq_ref, gksl_ref, gkwn_ref, cos_ref, sin_ref, bd_ref,
               gsgu_ref, spw_ref, spb_ref, gomix_ref,
               qn_ref, qr_ref, kc_ref, vc_ref, ksl_ref, vsl_ref, kwn_ref, vwn_ref, gate_ref, omix_ref,
               h_next, h_prev, z_next, z_prev, regroup_scr):
    tm = x_ref.shape[0]
    i = jnp.maximum(pl.program_id(0) - 2, 0) % tiles_per_seq
    cos = cos_ref[...]
    sin = sin_ref[...]

    def project_stream():
        h = h_prev[...]
        for lo in range(0, _IN_COLS_PACKED, PROJ_UNIT):
            z_next[:, lo:lo + PROJ_UNIT] = _dot(h, w_ref[:, lo:lo + PROJ_UNIT])
            yield

    def prenorm_stream():
        x = x_ref[...]
        ms = jnp.mean(x * x, axis=-1, keepdims=True)
        h_next[...] = (x * lax.rsqrt(ms + EPS) * gmix_ref[...]).astype(_bf16)
        yield

    def query_stream():
        zq = z_prev[:, _C_Q:_C_Q + ATTN_WIDTH]
        qn = zq * lax.rsqrt(_group_mean(zq * zq, bd_ref) + EPS) * gq_ref[...]
        yield
        cos4 = jnp.concatenate([cos] * 4, axis=1)
        sin4 = jnp.concatenate([sin] * 4, axis=1)
        qr = qn * cos4 + _swap_halves(qn) * sin4
        yield
        qn_t, qr_t = qn.T, qr.T
        for hq in range(N_ATTN_HEADS):
            sl = slice(hq * HEAD_DIM, (hq + 1) * HEAD_DIM)
            qn_ref[0, hq] = qn_t[sl].astype(_bf16)
            qr_ref[0, hq] = qr_t[sl].astype(_bf16)

    def kv_stream():
        def segment(off):
            return z_prev[:, _C_KV + off:_C_KV + off + KV_WIDTH]

        for n, (off, out_ref) in enumerate(((_KV_KC, kc_ref), (_KV_VC, vc_ref))):
            regroup_scr[n] = segment(off)
            for tok in range(CMP_STRIDE):
                every_16th = regroup_scr[n, pl.ds(tok, tm // CMP_STRIDE, stride=CMP_STRIDE), :]
                out_ref[0, :, tok * KV_WIDTH:(tok + 1) * KV_WIDTH] = every_16th.astype(_bf16)
        gate_ref[0] = jax.nn.sigmoid(segment(_KV_GATE))

        ones_block = jnp.where(lax.broadcasted_iota(jnp.int32, (HEAD_DIM, LANES), 0) == 0, 1.0, 0.0)
        for off, out_ref in ((_KV_VSL, vsl_ref), (_KV_VWN, vwn_ref)):
            zv2 = segment(off)
            for tt in range(tm // LANES):
                zt = zv2[tt * LANES:(tt + 1) * LANES, :].T
                for hh in range(N_KV_HEADS):
                    out_ref[0, hh, tt] = jnp.concatenate(
                        [zt[hh * HEAD_DIM:(hh + 1) * HEAD_DIM], ones_block], axis=0).astype(_bf16)
            yield

        lane = lax.broadcasted_iota(jnp.int32, (tm, LANES), 1)
        pos = i * tm + lax.broadcasted_iota(jnp.int32, (tm, LANES), 0)
        blk_onehot = jnp.where(lane - HEAD_DIM == pos // SEL_BLOCK, 1.0, 0.0)
        for off, g_ref, out_ref, extra in ((_KV_KSL, gksl_ref, ksl_ref, blk_onehot),
                                           (_KV_KWN, gkwn_ref, kwn_ref, jnp.zeros((tm, LANES), _f32))):
            zk = segment(off)
            kn = zk * lax.rsqrt(_group_mean(zk * zk, bd_ref) + EPS) * g_ref[...]
            kr = kn * cos + _swap_halves(kn) * sin
            out_ref[0, 0] = jnp.where(lane < HEAD_DIM, kr, extra).astype(_bf16)
            out_ref[0, 1] = jnp.where(lane < HEAD_DIM, pltpu.roll(kr, HEAD_DIM, 1), extra).astype(_bf16)
            yield

    row = lax.broadcasted_iota(jnp.int32, (CHUNK, 2 * CHUNK), 0)
    colw = lax.broadcasted_iota(jnp.int32, (CHUNK, 2 * CHUNK), 1) % CHUNK
    causal_w = colw <= row
    lane_c = lax.broadcasted_iota(jnp.int32, (CHUNK, LANES), 1)

    def gmlp_stream(c):
        rows = slice(c * CHUNK, (c + 1) * CHUNK)
        zu = jax.nn.gelu(z_prev[rows, _C_ZU:_C_ZU + MIX_WIDTH])
        yield
        zv = jax.nn.gelu(z_prev[rows, _C_ZV:_C_ZV + MIX_WIDTH])
        yield
        vn = zv * lax.rsqrt(_group_mean(zv * zv, bd_ref) + EPS) * gsgu_ref[...]
        yield
        sv_parts = []
        for p in range(N_MIX_GROUPS // 2):
            vp = vn[:, p * LANES:(p + 1) * LANES]
            rhs = jnp.concatenate([jnp.where(lane_c < HEAD_DIM, vp, 0.0),
                                   jnp.where(lane_c < HEAD_DIM, 0.0, vp)], axis=0).astype(_bf16)
            w_pair = jnp.where(causal_w, spw_ref[p], jnp.zeros((), _bf16))
            sv_parts.append(_dot(w_pair, rhs))
        yield
        sv = jnp.concatenate(sv_parts, axis=1) + spb_ref[...]
        om = zu * sv
        oms = jnp.mean(om * om, axis=-1, keepdims=True)
        omix_ref[0, rows] = (om * lax.rsqrt(oms + EPS) * gomix_ref[...]).astype(_bf16)

    _interleave([project_stream(), query_stream(), kv_stream()] + [gmlp_stream(c) for c in range(tm // CHUNK)]
                + [prenorm_stream()])


def _proj_call(x, gmix, w_in_p, gq, gksl, gkwn, cos, sin, bd, gsgu, spw, spb, gomix):
    B, S, _ = x.shape
    tm = PROJ_TILE
    nt = S // tm
    n_tiles = B * nt
    const2 = lambda t: (0, 0)
    const3 = lambda t: (0, 0, 0)

    def done(t):
        return jnp.maximum(t - 2, 0)

    tok3 = lambda t: (done(t) // nt, done(t) % nt, 0)
    head4 = lambda t: (done(t) // nt, 0, done(t) % nt, 0)
    out_shape = (
        jax.ShapeDtypeStruct((B, N_ATTN_HEADS, HEAD_DIM, S), _bf16),
        jax.ShapeDtypeStruct((B, N_ATTN_HEADS, HEAD_DIM, S), _bf16),
        jax.ShapeDtypeStruct((B, S // CMP_STRIDE, CMP_STRIDE * KV_WIDTH), _bf16),
        jax.ShapeDtypeStruct((B, S // CMP_STRIDE, CMP_STRIDE * KV_WIDTH), _bf16),
        jax.ShapeDtypeStruct((B, N_KV_HEADS, S, LANES), _bf16),
        jax.ShapeDtypeStruct((B, N_KV_HEADS, S // LANES, LANES, LANES), _bf16),
        jax.ShapeDtypeStruct((B, N_KV_HEADS, S, LANES), _bf16),
        jax.ShapeDtypeStruct((B, N_KV_HEADS, S // LANES, LANES, LANES), _bf16),
        jax.ShapeDtypeStruct((B, S, LANES), _f32),
        jax.ShapeDtypeStruct((B, S, MIX_WIDTH), _bf16),
    )
    q_spec = pl.BlockSpec((1, N_ATTN_HEADS, HEAD_DIM, tm), lambda t: (done(t) // nt, 0, 0, done(t) % nt))
    k_spec = pl.BlockSpec((1, N_KV_HEADS, tm, LANES), head4)
    t_spec = pl.BlockSpec((1, tm, LANES), tok3)
    g_spec = pl.BlockSpec((1, tm // CMP_STRIDE, CMP_STRIDE * KV_WIDTH), tok3)
    vt_spec = pl.BlockSpec((1, N_KV_HEADS, tm // LANES, LANES, LANES),
                           lambda t: (done(t) // nt, 0, done(t) % nt, 0, 0))
    return pl.pallas_call(
        functools.partial(_proj_kernel, nt),
        grid=(n_tiles + 2,),
        in_specs=[
            pl.BlockSpec((tm, D_MODEL), lambda t: (jnp.minimum(t, n_tiles - 1), 0)),
            pl.BlockSpec((1, D_MODEL), const2),
            pl.BlockSpec((D_MODEL, _IN_COLS_PACKED), const2),
            pl.BlockSpec((1, ATTN_WIDTH), const2),
            pl.BlockSpec((1, KV_WIDTH), const2),
            pl.BlockSpec((1, KV_WIDTH), const2),
            pl.BlockSpec((tm, LANES), lambda t: (done(t) % nt, 0)),
            pl.BlockSpec((tm, LANES), lambda t: (done(t) % nt, 0)),
            pl.BlockSpec((MXU_DIM, MXU_DIM), const2),
            pl.BlockSpec((1, MIX_WIDTH), const2),
            pl.BlockSpec((N_MIX_GROUPS // 2, CHUNK, 2 * CHUNK), const3),
            pl.BlockSpec((CHUNK, MIX_WIDTH), const2),
            pl.BlockSpec((1, MIX_WIDTH), const2),
        ],
        out_specs=(q_spec, q_spec, g_spec, g_spec, k_spec, vt_spec, k_spec, vt_spec, t_spec,
                   pl.BlockSpec((1, tm, MIX_WIDTH), tok3)),
        out_shape=out_shape,
        scratch_shapes=[pltpu.VMEM((tm, D_MODEL), _bf16), pltpu.VMEM((tm, D_MODEL), _bf16),
                        pltpu.VMEM((tm, _IN_COLS_PACKED), _f32), pltpu.VMEM((tm, _IN_COLS_PACKED), _f32),
                        pltpu.VMEM((2, tm, KV_WIDTH), _f32)],
        compiler_params=pltpu.CompilerParams(
            dimension_semantics=("arbitrary",), vmem_limit_bytes=VMEM_LIMIT),
        name="proj",
    )(x.reshape(B * S, D_MODEL), gmix, w_in_p, gq, gksl, gkwn, cos, sin, bd, gsgu, spw, spb, gomix)


def _compress_kernel(kc_ref, vc_ref, pe_ref, w1_ref, w2_ref, gk_ref, kcmp_ref, vcmp_ref):
    n_rows = kc_ref.shape[1]
    zero_half = jnp.zeros((n_rows, HEAD_DIM), _f32)
    for t, (src_ref, out_ref) in enumerate(((kc_ref, kcmp_ref), (vc_ref, vcmp_ref))):
        g = src_ref[0].astype(_f32)
        first = _dot((g + pe_ref[t, 0:1]).astype(_bf16), w1_ref[t, 0])
        second = _dot((g + pe_ref[t, 1:2]).astype(_bf16), w1_ref[t, 1])
        hid = jax.nn.gelu(first + pltpu.roll(second, n_rows - 1, 0)).astype(_bf16)
        heads = [_dot(hid[:, hh * CMP_HIDDEN:(hh + 1) * CMP_HIDDEN], w2_ref[t]) for hh in range(N_KV_HEADS)]
        if t == 0:
            heads = [c * lax.rsqrt(jnp.mean(c * c, axis=-1, keepdims=True) + EPS) * gk_ref[...] for c in heads]
            for hh in range(N_KV_HEADS):
                out_ref[0, hh] = jnp.concatenate([heads[hh], zero_half], axis=1).astype(_bf16)
        else:
            for hh in range(N_KV_HEADS):
                vt = jnp.concatenate([heads[hh], zero_half], axis=1).T
                out_ref[0, hh] = vt.astype(_bf16)


def _compress_call(kc_g, vc_g, pe_rows, w1x, w2, gk0):
    B, n_rows, width = kc_g.shape
    return pl.pallas_call(
        _compress_kernel,
        grid=(B,),
        in_specs=[
            pl.BlockSpec((1, n_rows, width), lambda b: (b, 0, 0)),
            pl.BlockSpec((1, n_rows, width), lambda b: (b, 0, 0)),
            pl.BlockSpec((2, 2, width), lambda b: (0, 0, 0)),
            pl.BlockSpec((2, 2, width, N_KV_HEADS * CMP_HIDDEN), lambda b: (0, 0, 0, 0)),
            pl.BlockSpec((2, CMP_HIDDEN, HEAD_DIM), lambda b: (0, 0, 0)),
            pl.BlockSpec((1, HEAD_DIM), lambda b: (0, 0)),
        ],
        out_specs=(pl.BlockSpec((1, N_KV_HEADS, n_rows, LANES), lambda b: (b, 0, 0, 0)),
                   pl.BlockSpec((1, N_KV_HEADS, n_rows, LANES), lambda b: (b, 0, 0, 0))),
        out_shape=(jax.ShapeDtypeStruct((B, N_KV_HEADS, n_rows, LANES), _bf16),
                   jax.ShapeDtypeStruct((B, N_KV_HEADS, n_rows, LANES), _bf16)),
        compiler_params=pltpu.CompilerParams(
            dimension_semantics=("parallel",), vmem_limit_bytes=VMEM_LIMIT),
        name="compress",
    )(kc_g, vc_g, pe_rows, w1x, w2, gk0)


def _interleave(streams):
    live = list(streams)
    while live:
        for g in list(live):
            try:
                next(g)
            except StopIteration:
                live.remove(g)


def _delayed(stream, units):
    for _ in range(units):
        yield
    yield from stream


def _attn_kernel(qn_ref, qr_ref, kcmp_ref, vcmpt_ref, ksl_ref, vslt_ref, kwn_ref, vwnt_ref, gate_ref,
                 ovt_ref, mask_ref, gout_ref, o_ref):
    refs = (qn_ref, qr_ref, kcmp_ref, vcmpt_ref, ksl_ref, vslt_ref, kwn_ref, vwnt_ref, gate_ref,
            ovt_ref, mask_ref, gout_ref, o_ref)
    S = ksl_ref.shape[2]

    def step_body(c):
        tiles = [_attn_tile(sub, c, *refs) for sub in range(ATTN_SUBTILES)]
        _interleave([_delayed(stream, sub * TILE_SKEW) for sub, (streams, _) in enumerate(tiles)
                     for stream in streams])
        for _, finish in tiles:
            finish()

    cls = (pl.program_id(1) * ATTN_SUBTILES * ATTN_Q_TILE) // SEL_CHUNK
    for c in range(S // SEL_CHUNK):
        pl.when(cls == c)(functools.partial(step_body, c))


def _attn_tile(sub, c, qn_ref, qr_ref, kcmp_ref, vcmpt_ref, ksl_ref, vslt_ref, kwn_ref, vwnt_ref, gate_ref,
               ovt_ref, mask_ref, gout_ref, o_ref):
    tq = ATTN_Q_TILE
    sub_cols = slice(sub * tq, (sub + 1) * tq)
    rows = GQA_GROUP * tq
    S = ksl_ref.shape[2]
    n_cmp_pad = kcmp_ref.shape[2]
    n_sel = S // SEL_BLOCK
    win_tiles = WINDOW // tq
    unit_tiles = ATTN_UNIT // tq
    i = pl.program_id(1) * ATTN_SUBTILES + sub
    q0 = i * tq

    qpos = q0 + lax.broadcasted_iota(jnp.int32, (1, rows), 1) % tq
    blk = lax.broadcasted_iota(jnp.int32, (n_sel, tq), 0)
    tpos = q0 + lax.broadcasted_iota(jnp.int32, (n_sel, tq), 1)
    gates_t = gate_ref[0, sub_cols, :].T
    zero_rows = jnp.zeros((HEAD_DIM, rows), _bf16)

    def stacked_heads_t(ref, h):
        return jnp.concatenate([ref[0, GQA_GROUP * h + g, :, sub_cols] for g in range(GQA_GROUP)], axis=1)

    def tile_mask(key_tile, low_edge=False):
        kind = jnp.where(key_tile < i, MASK_FULL, jnp.where(key_tile == i, MASK_DIAG, MASK_NONE))
        if low_edge:
            kind = jnp.where(i >= win_tiles, MASK_LOW, kind)
        return mask_ref[kind]

    def attend(out, h, k_ref, vt_ref, q_t, first_tile, tile_masks):
        n_tiles = len(tile_masks)
        scores, mx = [], None
        for u0 in range(0, n_tiles, unit_tiles):
            nt = min(unit_tiles, n_tiles - u0)
            if isinstance(first_tile, int):
                keys = slice((first_tile + u0) * tq, (first_tile + u0 + nt) * tq)
            else:
                keys = pl.ds(pl.multiple_of((first_tile + u0) * tq, tq), nt * tq)
            sc = _dot(k_ref[0, h, keys, :], q_t)
            if any(tile_masks[u0 + t] is not None for t in range(nt)):
                sc = jnp.concatenate(
                    [sc[t * tq:(t + 1) * tq] if tile_masks[u0 + t] is None
                     else sc[t * tq:(t + 1) * tq] + tile_masks[u0 + t]() for t in range(nt)], axis=0)
            scores.append(sc)
            cm = jnp.max(sc, axis=0, keepdims=True)
            mx = cm if mx is None else jnp.maximum(mx, cm)
            yield
        probs = []
        for sc in scores:
            probs.append(jnp.exp2(sc - mx).astype(_bf16))
            yield
        vt = jnp.concatenate([vt_ref[0, h, first_tile + t] for t in range(n_tiles)], axis=1)
        acc = _dot(vt, jnp.concatenate(probs, axis=0))
        out[h] = acc[0:HEAD_DIM] / acc[HEAD_DIM:HEAD_DIM + 1]
        yield

    def window_stream(out, h, c):
        first_tile = jnp.maximum(i - win_tiles, 0)
        q_t = jnp.concatenate([stacked_heads_t(qr_ref, h), zero_rows], axis=0)
        if c * SEL_CHUNK >= WINDOW:
            masks = ([lambda: mask_ref[MASK_LOW]] + [None] * (win_tiles - 1) + [lambda: mask_ref[MASK_DIAG]])
        else:
            masks = [functools.partial(tile_mask, first_tile + t, low_edge=(t == 0))
                     for t in range(win_tiles + 1)]
        yield from attend(out, h, kwn_ref, vwnt_ref, q_t, first_tile, masks)

    def compressed_and_selected_stream(o_cmps, o_sels, h, c):
        qn = stacked_heads_t(qn_ref, h)
        qr = stacked_heads_t(qr_ref, h)

        s = _dot(kcmp_ref[0, h], jnp.concatenate([qn, zero_rows], axis=0))
        n_idx = lax.broadcasted_iota(jnp.int32, (n_cmp_pad, 1), 0)
        valid_c = n_idx * CMP_STRIDE + (CMP_BLOCK - 1) <= qpos
        s = jnp.where(valid_c, s, NEG_BIG)
        m = jnp.max(s, axis=0, keepdims=True)
        e = jnp.where(valid_c, jnp.exp2(s - m), 0.0)
        p_c = e / jnp.maximum(jnp.sum(e, axis=0, keepdims=True), 1e-20)
        o_cmps[h] = _dot(vcmpt_ref[0, h], p_c.astype(_bf16))[0:HEAD_DIM]
        yield

        eligible = blk * SEL_BLOCK <= tpos
        if SEL_CHUNK * (c + 1) <= SEL_TOPN * SEL_BLOCK:
            chosen = eligible
        else:
            p_sum = p_c[:, 0:tq] + p_c[:, tq:2 * tq] + p_c[:, 2 * tq:3 * tq] + p_c[:, 3 * tq:4 * tq]
            p_hi = p_sum.astype(_bf16)
            p_lo = (p_sum - p_hi.astype(_f32)).astype(_bf16)
            imp = _dot(ovt_ref[...], p_hi) + _dot(ovt_ref[...], p_lo)
            cur = tpos // SEL_BLOCK
            forced = (blk == 0) | (blk == cur) | (blk == cur - 1)
            score = jnp.where(forced, FORCE_SCORE, jnp.where(eligible, imp, -jnp.inf))
            rank = jnp.zeros((n_sel, tq), _f32)
            for jp in range(n_sel):
                other = score[jp:jp + 1, :]
                tie = jnp.where(blk > jp, 1.0, 0.0)
                rank = rank + jnp.where(other > score, 1.0, jnp.where(other == score, tie, 0.0))
            chosen = (rank < float(SEL_TOPN)) & eligible
        sel_bias = jnp.where(chosen, 0.0, NEG_BIG).astype(_bf16)
        yield

        q_t = jnp.concatenate([qr, jnp.concatenate([sel_bias] * GQA_GROUP, axis=1),
                               jnp.zeros((LANES - HEAD_DIM - n_sel, rows), _bf16)], axis=0)
        n_tiles = (SEL_CHUNK // tq) * (c + 1)
        first_edge = n_tiles - SEL_CHUNK // tq
        masks = [None if t < first_edge else functools.partial(tile_mask, t) for t in range(n_tiles)]
        yield from attend(o_sels, h, ksl_ref, vslt_ref, q_t, 0, masks)

    o_wins, o_cmps, o_sels = {}, {}, {}
    streams = [
        window_stream(o_wins, 0, c),
        compressed_and_selected_stream(o_cmps, o_sels, 0, c),
        _delayed(window_stream(o_wins, 1, c), STREAM_SKEW),
        _delayed(compressed_and_selected_stream(o_cmps, o_sels, 1, c), STREAM_SKEW),
    ]

    def finish():
        slabs = []
        for h in range(N_KV_HEADS):
            for g in range(GQA_GROUP):
                hq = GQA_GROUP * h + g
                cols = slice(g * tq, (g + 1) * tq)
                slabs.append(gates_t[hq:hq + 1] * o_cmps[h][:, cols]
                             + gates_t[N_ATTN_HEADS + hq:N_ATTN_HEADS + hq + 1] * o_sels[h][:, cols]
                             + gates_t[2 * N_ATTN_HEADS + hq:2 * N_ATTN_HEADS + hq + 1] * o_wins[h][:, cols])
        ot = jnp.concatenate(slabs, axis=0)
        ms = jnp.mean(ot * ot, axis=0, keepdims=True)
        o_ref[0, sub_cols, :] = (ot * lax.rsqrt(ms + EPS) * gout_ref[...]).T.astype(_bf16)

    return streams, finish


def _attn_call(qn_t, qr_t, kcmp, vcmp_t, ksl, vsl_t, kwn, vwn_t, gates, ovt, masks, gout_attn):
    B, _, _, S = qn_t.shape
    tq = ATTN_Q_TILE
    assert tq == LANES and S % SEL_CHUNK == 0 and SEL_CHUNK % tq == 0 and WINDOW % tq == 0
    assert ATTN_UNIT % tq == 0 and kcmp.shape[2] == LANES
    step = tq * ATTN_SUBTILES
    assert SEL_CHUNK % step == 0
    n_cmp_pad = kcmp.shape[2]
    n_sel = S // SEL_BLOCK
    rows = GQA_GROUP * tq
    per_b4 = lambda b, i: (b, 0, 0, 0)
    const2 = lambda b, i: (0, 0)
    vt_spec = pl.BlockSpec((1, N_KV_HEADS, S // LANES, LANES, LANES), lambda b, i: (b, 0, 0, 0, 0))
    return pl.pallas_call(
        _attn_kernel,
        grid=(B, S // step),
        in_specs=[
            pl.BlockSpec((1, N_ATTN_HEADS, HEAD_DIM, step), lambda b, i: (b, 0, 0, i)),
            pl.BlockSpec((1, N_ATTN_HEADS, HEAD_DIM, step), lambda b, i: (b, 0, 0, i)),
            pl.BlockSpec((1, N_KV_HEADS, n_cmp_pad, LANES), per_b4),
            pl.BlockSpec((1, N_KV_HEADS, n_cmp_pad, LANES), per_b4),
            pl.BlockSpec((1, N_KV_HEADS, S, LANES), per_b4),
            vt_spec,
            pl.BlockSpec((1, N_KV_HEADS, S, LANES), per_b4),
            vt_spec,
            pl.BlockSpec((1, step, LANES), lambda b, i: (b, i, 0)),
            pl.BlockSpec((n_sel, n_cmp_pad), const2),
            pl.BlockSpec((4, tq, rows), lambda b, i: (0, 0, 0)),
            pl.BlockSpec((ATTN_WIDTH, tq), const2),
        ],
        out_specs=pl.BlockSpec((1, step, ATTN_WIDTH), lambda b, i: (b, i, 0)),
        out_shape=jax.ShapeDtypeStruct((B, S, ATTN_WIDTH), _bf16),
        compiler_params=pltpu.CompilerParams(
            dimension_semantics=("parallel", "arbitrary"), vmem_limit_bytes=VMEM_LIMIT),
        name="attn",
    )(qn_t, qr_t, kcmp, vcmp_t, ksl, vsl_t, kwn, vwn_t, gates, ovt, masks, gout_attn)


def _ffn_kernel(x_ref, oa_ref, om_ref, wo_ref, gffn_ref, w1_ref, w2_ref, out_ref, act_scr):
    x2 = (x_ref[...] + _dot(oa_ref[...], wo_ref[0:ATTN_WIDTH, :])
          + _dot(om_ref[...], wo_ref[ATTN_WIDTH:ATTN_WIDTH + MIX_WIDTH, :]))
    ms = jnp.mean(x2 * x2, axis=-1, keepdims=True)
    h = (x2 * lax.rsqrt(ms + EPS) * gffn_ref[...]).astype(_bf16)
    for c in range(D_FF // FF_CHUNK):
        cols = slice(c * FF_CHUNK, (c + 1) * FF_CHUNK)
        a = jnp.maximum(_dot(h, w1_ref[:, cols]), 0.0)
        act_scr[:, cols] = (a * a).astype(_bf16)
    out_ref[...] = x2 + _dot(act_scr[...], w2_ref[...])


def _ffn_call(x2d, oa, om, wo, gffn, w1, w2):
    T = x2d.shape[0]
    tm = FFN_TILE
    tok = lambda i: (i, 0)
    const = lambda i: (0, 0)
    return pl.pallas_call(
        _ffn_kernel,
        grid=(T // tm,),
        in_specs=[
            pl.BlockSpec((tm, D_MODEL), tok),
            pl.BlockSpec((tm, ATTN_WIDTH), tok),
            pl.BlockSpec((tm, MIX_WIDTH), tok),
            pl.BlockSpec((D_MODEL, D_MODEL), const),
            pl.BlockSpec((1, D_MODEL), const),
            pl.BlockSpec((D_MODEL, D_FF), const),
            pl.BlockSpec((D_FF, D_MODEL), const),
        ],
        out_specs=pl.BlockSpec((tm, D_MODEL), tok),
        out_shape=jax.ShapeDtypeStruct((T, D_MODEL), _f32),
        scratch_shapes=[pltpu.VMEM((tm, D_FF), _bf16)],
        compiler_params=pltpu.CompilerParams(
            dimension_semantics=("parallel",), vmem_limit_bytes=VMEM_LIMIT),
        name="ffn",
    )(x2d, oa, om, wo, gffn, w1, w2)


def _constants(S):
    half = HEAD_DIM // 2
    inv = ROPE_THETA ** (-jnp.arange(half, dtype=_f32) / half)
    ang = jnp.arange(S, dtype=_f32)[:, None] * inv[None, :]
    cos = jnp.cos(ang)
    sin = jnp.sin(ang)
    cos2 = jnp.concatenate([cos, cos, cos, cos], axis=1)
    sin2 = jnp.concatenate([-sin, sin, -sin, sin], axis=1)
    mlane = np.arange(MXU_DIM)
    bd = (mlane[:, None] // HEAD_DIM == mlane[None, :] // HEAD_DIM).astype(np.float32) / HEAD_DIM
    n_cmp = (S - CMP_BLOCK) // CMP_STRIDE + 1
    n_sel = S // SEL_BLOCK
    n_cmp_pad = S // CMP_STRIDE
    cmp_start = np.arange(n_cmp)[:, None] * CMP_STRIDE
    sel_start = np.arange(n_sel)[None, :] * SEL_BLOCK
    overlap = np.clip(np.minimum(cmp_start + CMP_BLOCK, sel_start + SEL_BLOCK)
                      - np.maximum(cmp_start, sel_start), 0, None).astype(np.float32) / CMP_BLOCK
    ovt = np.zeros((n_sel, n_cmp_pad), np.float32)
    ovt[:, :n_cmp] = overlap.T
    qq = (np.arange(GQA_GROUP * ATTN_Q_TILE) % ATTN_Q_TILE)[None, :]
    kk = np.arange(ATTN_Q_TILE)[:, None]
    masks = np.zeros((4, ATTN_Q_TILE, GQA_GROUP * ATTN_Q_TILE), np.float32)
    masks[MASK_DIAG] = np.where(kk <= qq, 0.0, NEG_BIG)
    masks[MASK_LOW] = np.where(kk > qq, 0.0, NEG_BIG)
    masks[MASK_NONE] = NEG_BIG
    return cos2, sin2, jnp.asarray(bd, _bf16), jnp.asarray(ovt, _bf16), jnp.asarray(masks, _f32)


def _pack_w_in(w):
    w = w.astype(_bf16)
    gate0 = ATTN_WIDTH + 6 * KV_WIDTH
    n_g = N_GATES * N_ATTN_HEADS
    gates = w[:, gate0:gate0 + n_g].reshape(D_MODEL, N_ATTN_HEADS, N_GATES)
    gates = jnp.transpose(gates, (0, 2, 1)).reshape(D_MODEL, n_g)
    pad = jnp.zeros((D_MODEL, _C_KV + _KV_GROUP - gate0 - n_g), w.dtype)
    return jnp.concatenate([w[:, :gate0], gates, pad, w[:, gate0 + n_g:]], axis=1)


def _expand_cmp_w1(w1):
    w = w1.reshape(2, CMP_STRIDE, HEAD_DIM, CMP_HIDDEN)
    z = jnp.zeros_like(w)
    h0 = jnp.concatenate([w, z], axis=-1)
    h1 = jnp.concatenate([z, w], axis=-1)
    both = jnp.stack([h0, h1], axis=2)
    return both.reshape(2, CMP_STRIDE * KV_WIDTH, N_KV_HEADS * CMP_HIDDEN).astype(_bf16)


def _expand_pe(pe):
    p = pe.reshape(2, CMP_STRIDE, 1, HEAD_DIM)
    return jnp.broadcast_to(p, (2, CMP_STRIDE, N_KV_HEADS, HEAD_DIM)).reshape(2, CMP_STRIDE * KV_WIDTH)


def _forward(x, g_mix_norm, w_in, g_q, g_k, cmp_pe, cmp_w1, cmp_w2, g_sgu, sp_w, sp_b, g_out, w_out,
             g_ffn_norm, w_ff1, w_ff2):
    B, S, _ = x.shape
    l = 0
    scale = float(HEAD_DIM ** -0.5 * np.log2(np.e))
    cos2, sin2, bd, ovt, masks = _constants(S)

    w_in_p = _pack_w_in(w_in[l])
    gq = (jnp.tile(g_q[l], N_ATTN_HEADS) * scale)[None, :]
    gksl = jnp.tile(g_k[l, 1], N_KV_HEADS)[None, :]
    gkwn = jnp.tile(g_k[l, 2], N_KV_HEADS)[None, :]
    gsgu = g_sgu[l].reshape(1, MIX_WIDTH)
    spw = sp_w[l].reshape(N_MIX_GROUPS // 2, 2, CHUNK, CHUNK)
    spw = jnp.transpose(spw, (0, 2, 1, 3)).reshape(N_MIX_GROUPS // 2, CHUNK, 2 * CHUNK).astype(_bf16)
    spb = jnp.repeat(sp_b[l].T, HEAD_DIM, axis=1)
    gout = g_out[l]

    qn, qr, kc, vc, ksl, vsl, kwn, vwn, gates, omix = _proj_call(
        x, g_mix_norm[l][None, :], w_in_p, gq, gksl, gkwn, cos2, sin2, bd, gsgu, spw, spb,
        gout[None, ATTN_WIDTH:])

    pe_rows = jnp.stack([_expand_pe(cmp_pe[l, 0]), _expand_pe(cmp_pe[l, 1])])
    w1x = jnp.stack([_expand_cmp_w1(cmp_w1[l, 0]), _expand_cmp_w1(cmp_w1[l, 1])])
    kcmp, vcmp = _compress_call(kc, vc, pe_rows, w1x, cmp_w2[l].astype(_bf16), g_k[l, 0][None, :])

    oattn = _attn_call(qn, qr, kcmp, vcmp, ksl, vsl, kwn, vwn, gates, ovt, masks,
                       jnp.broadcast_to(gout[:ATTN_WIDTH, None], (ATTN_WIDTH, ATTN_Q_TILE)))

    out = _ffn_call(x.reshape(B * S, D_MODEL), oattn.reshape(B * S, ATTN_WIDTH),
                    omix.reshape(B * S, MIX_WIDTH), w_out[l].astype(_bf16), g_ffn_norm[l][None, :],
                    w_ff1[l].astype(_bf16), w_ff2[l].astype(_bf16))
    return out.reshape(B, S, D_MODEL), (qn, qr, kc, vc, ksl, vsl, kwn, vwn, gates, omix, kcmp, vcmp, oattn)


def kernel(x, g_mix_norm, w_in, g_q, g_k, cmp_pe, cmp_w1, cmp_w2, g_sgu, sp_w, sp_b, g_out, w_out,
           g_ffn_norm, w_ff1, w_ff2):
    return _forward(x, g_mix_norm, w_in, g_q, g_k, cmp_pe, cmp_w1, cmp_w2, g_sgu, sp_w, sp_b, g_out, w_out,
                    g_ffn_norm, w_ff1, w_ff2)[0]
```

```python
import functools

import numpy as np
import jax
import jax.numpy as jnp
from jax import lax
from jax.experimental import pallas as pl
from jax.experimental.pallas import tpu as pltpu

D_MODEL = 1024
HEAD_DIM = 64
N_ATTN_HEADS = 8
N_MIX_GROUPS = 8
GQA_GROUP = 4
N_KV_HEADS = 2
ATTN_WIDTH = 512
MIX_WIDTH = 512
KV_WIDTH = 128
N_GATES = 3
CMP_BLOCK = 32
CMP_STRIDE = 16
CMP_HIDDEN = 256
SEL_BLOCK = 64
SEL_TOPN = 16
WINDOW = 512
CHUNK = 128
D_FF = 4 * D_MODEL
ROPE_THETA = 10000.0
EPS = 1e-6
FORCE_SCORE = 1e9

LANES = 128
NEG_BIG = -1e30
PROJ_TILE = 256
PROJ_UNIT = 512
ATTN_Q_TILE = 128
ATTN_SUBTILES = 2
TILE_SKEW = 2
SEL_CHUNK = 512
ATTN_UNIT = 512
STREAM_SKEW = 1
MASK_FULL, MASK_DIAG, MASK_LOW, MASK_NONE = 0, 1, 2, 3
FFN_TILE = 512
FF_CHUNK = 1024
VMEM_LIMIT = 56 * 1024 * 1024

_C_Q = 0
_C_KV = 512
_KV_KC, _KV_VC, _KV_KSL, _KV_VSL, _KV_KWN, _KV_VWN, _KV_GATE = (128 * n for n in range(7))
_KV_GROUP = 1024
_C_ZU = 1536
_C_ZV = 2048
_IN_COLS_PACKED = 2560
MXU_DIM = 256

_bf16 = jnp.bfloat16
_f32 = jnp.float32


def _dot(a, b):
    return jnp.dot(a, b, preferred_element_type=_f32)


def _group_mean(sq, bd_ref):
    cw = min(sq.shape[1], MXU_DIM)
    bd = bd_ref[0:cw, 0:cw]
    parts = [_dot(sq[:, c * cw:(c + 1) * cw].astype(_bf16), bd) for c in range(sq.shape[1] // cw)]
    return parts[0] if len(parts) == 1 else jnp.concatenate(parts, axis=1)


def _swap_halves(v):
    n = v.shape[1]
    lane = lax.broadcasted_iota(jnp.int32, v.shape, 1)
    return jnp.where((lane % HEAD_DIM) < HEAD_DIM // 2, pltpu.roll(v, n - HEAD_DIM // 2, 1),
                     pltpu.roll(v, HEAD_DIM // 2, 1))


def _proj_kernel(tiles_per_seq, x_ref, gmix_ref, w_ref, gq_ref, gksl_ref, gkwn_ref, cos_ref, sin_ref, bd_ref,
                 gsgu_ref, spw_ref, spb_ref, gomix_ref,
                 qn_ref, qr_ref, kc_ref, vc_ref, ksl_ref, vsl_ref, kwn_ref, vwn_ref, gate_ref, omix_ref,
                 h_even, h_odd, z_even, z_odd, regroup_scr):
    t = pl.program_id(0)

    @pl.when(t == 0)
    def _():
        h_odd[...] = jnp.zeros(h_odd.shape, _bf16)
        z_odd[...] = jnp.zeros(z_odd.shape, _f32)

    pl.when(t % 2 == 0)(functools.partial(
        _proj_step, tiles_per_seq, x_ref, gmix_ref, w_ref, gq_ref, gksl_ref, gkwn_ref, cos_ref, sin_ref, bd_ref,
        gsgu_ref, spw_ref, spb_ref, gomix_ref, qn_ref, qr_ref, kc_ref, vc_ref, ksl_ref, vsl_ref, kwn_ref,
        vwn_ref, gate_ref, omix_ref, h_even, h_odd, z_even, z_odd, regroup_scr))
    pl.when(t % 2 == 1)(functools.partial(
        _proj_step, tiles_per_seq, x_ref, gmix_ref, w_ref, gq_ref, gksl_ref, gkwn_ref, cos_ref, sin_ref, bd_ref,
        gsgu_ref, spw_ref, spb_ref, gomix_ref, qn_ref, qr_ref, kc_ref, vc_ref, ksl_ref, vsl_ref, kwn_ref,
        vwn_ref, gate_ref, omix_ref, h_odd, h_even, z_odd, z_even, regroup_scr))


def _proj_step(tiles_per_seq, x_ref, gmix_ref, w_ref, gq_ref, gksl_ref, gkwn_ref, cos_ref, sin_ref, bd_ref,
               gsgu_ref, spw_ref, spb_ref, gomix_ref,
               qn_ref, qr_ref, kc_ref, vc_ref, ksl_ref, vsl_ref, kwn_ref, vwn_ref, gate_ref, omix_ref,
               h_next, h_prev, z_next, z_prev, regroup_scr):
    tm = x_ref.shape[0]
    i = jnp.maximum(pl.program_id(0) - 2, 0) % tiles_per_seq
    cos = cos_ref[...]
    sin = sin_ref[...]

    def project_stream():
        h = h_prev[...]
        for lo in range(0, _IN_COLS_PACKED, PROJ_UNIT):
            z_next[:, lo:lo + PROJ_UNIT] = _dot(h, w_ref[:, lo:lo + PROJ_UNIT])
            yield

    def prenorm_stream():
        x = x_ref[...]
        ms = jnp.mean(x * x, axis=-1, keepdims=True)
        h_next[...] = (x * lax.rsqrt(ms + EPS) * gmix_ref[...]).astype(_bf16)
        yield

    def query_stream():
        zq = z_prev[:, _C_Q:_C_Q + ATTN_WIDTH]
        qn = zq * lax.rsqrt(_group_mean(zq * zq, bd_ref) + EPS) * gq_ref[...]
        yield
        cos4 = jnp.concatenate([cos] * 4, axis=1)
        sin4 = jnp.concatenate([sin] * 4, axis=1)
        qr = qn * cos4 + _swap_halves(qn) * sin4
        yield
        qn_t, qr_t = qn.T, qr.T
        for hq in range(N_ATTN_HEADS):
            sl = slice(hq * HEAD_DIM, (hq + 1) * HEAD_DIM)
            qn_ref[0, hq] = qn_t[sl].astype(_bf16)
            qr_ref[0, hq] = qr_t[sl].astype(_bf16)

    def kv_stream():
        def segment(off):
            return z_prev[:, _C_KV + off:_C_KV + off + KV_WIDTH]

        for n, (off, out_ref) in enumerate(((_KV_KC, kc_ref), (_KV_VC, vc_ref))):
            regroup_scr[n] = segment(off)
            for tok in range(CMP_STRIDE):
                every_16th = regroup_scr[n, pl.ds(tok, tm // CMP_STRIDE, stride=CMP_STRIDE), :]
                out_ref[0, :, tok * KV_WIDTH:(tok + 1) * KV_WIDTH] = every_16th.astype(_bf16)
        gate_ref[0] = jax.nn.sigmoid(segment(_KV_GATE))

        ones_block = jnp.where(lax.broadcasted_iota(jnp.int32, (HEAD_DIM, LANES), 0) == 0, 1.0, 0.0)
        for off, out_ref in ((_KV_VSL, vsl_ref), (_KV_VWN, vwn_ref)):
            zv2 = segment(off)
            for tt in range(tm // LANES):
                zt = zv2[tt * LANES:(tt + 1) * LANES, :].T
                for hh in range(N_KV_HEADS):
                    out_ref[0, hh, tt] = jnp.concatenate(
                        [zt[hh * HEAD_DIM:(hh + 1) * HEAD_DIM], ones_block], axis=0).astype(_bf16)
            yield

        lane = lax.broadcasted_iota(jnp.int32, (tm, LANES), 1)
        pos = i * tm + lax.broadcasted_iota(jnp.int32, (tm, LANES), 0)
        blk_onehot = jnp.where(lane - HEAD_DIM == pos // SEL_BLOCK, 1.0, 0.0)
        for off, g_ref, out_ref, extra in ((_KV_KSL, gksl_ref, ksl_ref, blk_onehot),
                                           (_KV_KWN, gkwn_ref, kwn_ref, jnp.zeros((tm, LANES), _f32))):
            zk = segment(off)
            kn = zk * lax.rsqrt(_group_mean(zk * zk, bd_ref) + EPS) * g_ref[...]
            kr = kn * cos + _swap_halves(kn) * sin
            out_ref[0, 0] = jnp.where(lane < HEAD_DIM, kr, extra).astype(_bf16)
            out_ref[0, 1] = jnp.where(lane < HEAD_DIM, pltpu.roll(kr, HEAD_DIM, 1), extra).astype(_bf16)
            yield

    row = lax.broadcasted_iota(jnp.int32, (CHUNK, 2 * CHUNK), 0)
    colw = lax.broadcasted_iota(jnp.int32, (CHUNK, 2 * CHUNK), 1) % CHUNK
    causal_w = colw <= row
    lane_c = lax.broadcasted_iota(jnp.int32, (CHUNK, LANES), 1)

    def gmlp_stream(c):
        rows = slice(c * CHUNK, (c + 1) * CHUNK)
        zu = jax.nn.gelu(z_prev[rows, _C_ZU:_C_ZU + MIX_WIDTH])
        yield
        zv = jax.nn.gelu(z_prev[rows, _C_ZV:_C_ZV + MIX_WIDTH])
        yield
        vn = zv * lax.rsqrt(_group_mean(zv * zv, bd_ref) + EPS) * gsgu_ref[...]
        yield
        sv_parts = []
        for p in range(N_MIX_GROUPS // 2):
            vp = vn[:, p * LANES:(p + 1) * LANES]
            rhs = jnp.concatenate([jnp.where(lane_c < HEAD_DIM, vp, 0.0),
                                   jnp.where(lane_c < HEAD_DIM, 0.0, vp)], axis=0).astype(_bf16)
            w_pair = jnp.where(causal_w, spw_ref[p], jnp.zeros((), _bf16))
            sv_parts.append(_dot(w_pair, rhs))
        yield
        sv = jnp.concatenate(sv_parts, axis=1) + spb_ref[...]
        om = zu * sv
        oms = jnp.mean(om * om, axis=-1, keepdims=True)
        omix_ref[0, rows] = (om * lax.rsqrt(oms + EPS) * gomix_ref[...]).astype(_bf16)

    _interleave([project_stream(), query_stream(), kv_stream()] + [gmlp_stream(c) for c in range(tm // CHUNK)]
                + [prenorm_stream()])


def _proj_call(x, gmix, w_in_p, gq, gksl, gkwn, cos, sin, bd, gsgu, spw, spb, gomix):
    B, S, _ = x.shape
    tm = PROJ_TILE
    nt = S // tm
    n_tiles = B * nt
    const2 = lambda t: (0, 0)
    const3 = lambda t: (0, 0, 0)

    def done(t):
        return jnp.maximum(t - 2, 0)

    tok3 = lambda t: (done(t) // nt, done(t) % nt, 0)
    head4 = lambda t: (done(t) // nt, 0, done(t) % nt, 0)
    out_shape = (
        jax.ShapeDtypeStruct((B, N_ATTN_HEADS, HEAD_DIM, S), _bf16),
        jax.ShapeDtypeStruct((B, N_ATTN_HEADS, HEAD_DIM, S), _bf16),
        jax.ShapeDtypeStruct((B, S // CMP_STRIDE, CMP_STRIDE * KV_WIDTH), _bf16),
        jax.ShapeDtypeStruct((B, S // CMP_STRIDE, CMP_STRIDE * KV_WIDTH), _bf16),
        jax.ShapeDtypeStruct((B, N_KV_HEADS, S, LANES), _bf16),
        jax.ShapeDtypeStruct((B, N_KV_HEADS, S // LANES, LANES, LANES), _bf16),
        jax.ShapeDtypeStruct((B, N_KV_HEADS, S, LANES), _bf16),
        jax.ShapeDtypeStruct((B, N_KV_HEADS, S // LANES, LANES, LANES), _bf16),
        jax.ShapeDtypeStruct((B, S, LANES), _f32),
        jax.ShapeDtypeStruct((B, S, MIX_WIDTH), _bf16),
    )
    q_spec = pl.BlockSpec((1, N_ATTN_HEADS, HEAD_DIM, tm), lambda t: (done(t) // nt, 0, 0, done(t) % nt))
    k_spec = pl.BlockSpec((1, N_KV_HEADS, tm, LANES), head4)
    t_spec = pl.BlockSpec((1, tm, LANES), tok3)
    g_spec = pl.BlockSpec((1, tm // CMP_STRIDE, CMP_STRIDE * KV_WIDTH), tok3)
    vt_spec = pl.BlockSpec((1, N_KV_HEADS, tm // LANES, LANES, LANES),
                           lambda t: (done(t) // nt, 0, done(t) % nt, 0, 0))
    return pl.pallas_call(
        functools.partial(_proj_kernel, nt),
        grid=(n_tiles + 2,),
        in_specs=[
            pl.BlockSpec((tm, D_MODEL), lambda t: (jnp.minimum(t, n_tiles - 1), 0)),
            pl.BlockSpec((1, D_MODEL), const2),
            pl.BlockSpec((D_MODEL, _IN_COLS_PACKED), const2),
            pl.BlockSpec((1, ATTN_WIDTH), const2),
            pl.BlockSpec((1, KV_WIDTH), const2),
            pl.BlockSpec((1, KV_WIDTH), const2),
            pl.BlockSpec((tm, LANES), lambda t: (done(t) % nt, 0)),
            pl.BlockSpec((tm, LANES), lambda t: (done(t) % nt, 0)),
            pl.BlockSpec((MXU_DIM, MXU_DIM), const2),
            pl.BlockSpec((1, MIX_WIDTH), const2),
            pl.BlockSpec((N_MIX_GROUPS // 2, CHUNK, 2 * CHUNK), const3),
            pl.BlockSpec((CHUNK, MIX_WIDTH), const2),
            pl.BlockSpec((1, MIX_WIDTH), const2),
        ],
        out_specs=(q_spec, q_spec, g_spec, g_spec, k_spec, vt_spec, k_spec, vt_spec, t_spec,
                   pl.BlockSpec((1, tm, MIX_WIDTH), tok3)),
        out_shape=out_shape,
        scratch_shapes=[pltpu.VMEM((tm, D_MODEL), _bf16), pltpu.VMEM((tm, D_MODEL), _bf16),
                        pltpu.VMEM((tm, _IN_COLS_PACKED), _f32), pltpu.VMEM((tm, _IN_COLS_PACKED), _f32),
                        pltpu.VMEM((2, tm, KV_WIDTH), _f32)],
        compiler_params=pltpu.CompilerParams(
            dimension_semantics=("arbitrary",), vmem_limit_bytes=VMEM_LIMIT),
        name="proj",
    )(x.reshape(B * S, D_MODEL), gmix, w_in_p, gq, gksl, gkwn, cos, sin, bd, gsgu, spw, spb, gomix)


def _compress_kernel(kc_ref, vc_ref, pe_ref, w1_ref, w2_ref, gk_ref, kcmp_ref, vcmp_ref):
    n_rows = kc_ref.shape[1]
    zero_half = jnp.zeros((n_rows, HEAD_DIM), _f32)
    for t, (src_ref, out_ref) in enumerate(((kc_ref, kcmp_ref), (vc_ref, vcmp_ref))):
        g = src_ref[0].astype(_f32)
        first = _dot((g + pe_ref[t, 0:1]).astype(_bf16), w1_ref[t, 0])
        second = _dot((g + pe_ref[t, 1:2]).astype(_bf16), w1_ref[t, 1])
        hid = jax.nn.gelu(first + pltpu.roll(second, n_rows - 1, 0)).astype(_bf16)
        heads = [_dot(hid[:, hh * CMP_HIDDEN:(hh + 1) * CMP_HIDDEN], w2_ref[t]) for hh in range(N_KV_HEADS)]
        if t == 0:
            heads = [c * lax.rsqrt(jnp.mean(c * c, axis=-1, keepdims=True) + EPS) * gk_ref[...] for c in heads]
            for hh in range(N_KV_HEADS):
                out_ref[0, hh] = jnp.concatenate([heads[hh], zero_half], axis=1).astype(_bf16)
        else:
            for hh in range(N_KV_HEADS):
                vt = jnp.concatenate([heads[hh], zero_half], axis=1).T
                out_ref[0, hh] = vt.astype(_bf16)


def _compress_call(kc_g, vc_g, pe_rows, w1x, w2, gk0):
    B, n_rows, width = kc_g.shape
    return pl.pallas_call(
        _compress_kernel,
        grid=(B,),
        in_specs=[
            pl.BlockSpec((1, n_rows, width), lambda b: (b, 0, 0)),
            pl.BlockSpec((1, n_rows, width), lambda b: (b, 0, 0)),
            pl.BlockSpec((2, 2, width), lambda b: (0, 0, 0)),
            pl.BlockSpec((2, 2, width, N_KV_HEADS * CMP_HIDDEN), lambda b: (0, 0, 0, 0)),
            pl.BlockSpec((2, CMP_HIDDEN, HEAD_DIM), lambda b: (0, 0, 0)),
            pl.BlockSpec((1, HEAD_DIM), lambda b: (0, 0)),
        ],
        out_specs=(pl.BlockSpec((1, N_KV_HEADS, n_rows, LANES), lambda b: (b, 0, 0, 0)),
                   pl.BlockSpec((1, N_KV_HEADS, n_rows, LANES), lambda b: (b, 0, 0, 0))),
        out_shape=(jax.ShapeDtypeStruct((B, N_KV_HEADS, n_rows, LANES), _bf16),
                   jax.ShapeDtypeStruct((B, N_KV_HEADS, n_rows, LANES), _bf16)),
        compiler_params=pltpu.CompilerParams(
            dimension_semantics=("parallel",), vmem_limit_bytes=VMEM_LIMIT),
        name="compress",
    )(kc_g, vc_g, pe_rows, w1x, w2, gk0)


def _interleave(streams):
    live = list(streams)
    while live:
        for g in list(live):
            try:
                next(g)
            except StopIteration:
                live.remove(g)


def _delayed(stream, units):
    for _ in range(units):
        yield
    yield from stream


def _attn_kernel(qn_ref, qr_ref, kcmp_ref, vcmpt_ref, ksl_ref, vslt_ref, kwn_ref, vwnt_ref, gate_ref,
                 ovt_ref, mask_ref, gout_ref, o_ref):
    refs = (qn_ref, qr_ref, kcmp_ref, vcmpt_ref, ksl_ref, vslt_ref, kwn_ref, vwnt_ref, gate_ref,
            ovt_ref, mask_ref, gout_ref, o_ref)
    S = ksl_ref.shape[2]

    def step_body(c):
        tiles = [_attn_tile(sub, c, *refs) for sub in range(ATTN_SUBTILES)]
        _interleave([_delayed(stream, sub * TILE_SKEW) for sub, (streams, _) in enumerate(tiles)
                     for stream in streams])
        for _, finish in tiles:
            finish()

    cls = (pl.program_id(1) * ATTN_SUBTILES * ATTN_Q_TILE) // SEL_CHUNK
    for c in range(S // SEL_CHUNK):
        pl.when(cls == c)(functools.partial(step_body, c))


def _attn_tile(sub, c, qn_ref, qr_ref, kcmp_ref, vcmpt_ref, ksl_ref, vslt_ref, kwn_ref, vwnt_ref, gate_ref,
               ovt_ref, mask_ref, gout_ref, o_ref):
    tq = ATTN_Q_TILE
    sub_cols = slice(sub * tq, (sub + 1) * tq)
    rows = GQA_GROUP * tq
    S = ksl_ref.shape[2]
    n_cmp_pad = kcmp_ref.shape[2]
    n_sel = S // SEL_BLOCK
    win_tiles = WINDOW // tq
    unit_tiles = ATTN_UNIT // tq
    i = pl.program_id(1) * ATTN_SUBTILES + sub
    q0 = i * tq

    qpos = q0 + lax.broadcasted_iota(jnp.int32, (1, rows), 1) % tq
    blk = lax.broadcasted_iota(jnp.int32, (n_sel, tq), 0)
    tpos = q0 + lax.broadcasted_iota(jnp.int32, (n_sel, tq), 1)
    gates_t = gate_ref[0, sub_cols, :].T
    zero_rows = jnp.zeros((HEAD_DIM, rows), _bf16)

    def stacked_heads_t(ref, h):
        return jnp.concatenate([ref[0, GQA_GROUP * h + g, :, sub_cols] for g in range(GQA_GROUP)], axis=1)

    def tile_mask(key_tile, low_edge=False):
        kind = jnp.where(key_tile < i, MASK_FULL, jnp.where(key_tile == i, MASK_DIAG, MASK_NONE))
        if low_edge:
            kind = jnp.where(i >= win_tiles, MASK_LOW, kind)
        return mask_ref[kind]

    def attend(out, h, k_ref, vt_ref, q_t, first_tile, tile_masks):
        n_tiles = len(tile_masks)
        scores, mx = [], None
        for u0 in range(0, n_tiles, unit_tiles):
            nt = min(unit_tiles, n_tiles - u0)
            if isinstance(first_tile, int):
                keys = slice((first_tile + u0) * tq, (first_tile + u0 + nt) * tq)
            else:
                keys = pl.ds(pl.multiple_of((first_tile + u0) * tq, tq), nt * tq)
            sc = _dot(k_ref[0, h, keys, :], q_t)
            if any(tile_masks[u0 + t] is not None for t in range(nt)):
                sc = jnp.concatenate(
                    [sc[t * tq:(t + 1) * tq] if tile_masks[u0 + t] is None
                     else sc[t * tq:(t + 1) * tq] + tile_masks[u0 + t]() for t in range(nt)], axis=0)
            scores.append(sc)
            cm = jnp.max(sc, axis=0, keepdims=True)
            mx = cm if mx is None else jnp.maximum(mx, cm)
            yield
        probs = []
        for sc in scores:
            probs.append(jnp.exp2(sc - mx).astype(_bf16))
            yield
        vt = jnp.concatenate([vt_ref[0, h, first_tile + t] for t in range(n_tiles)], axis=1)
        acc = _dot(vt, jnp.concatenate(probs, axis=0))
        out[h] = acc[0:HEAD_DIM] / acc[HEAD_DIM:HEAD_DIM + 1]
        yield

    def window_stream(out, h, c):
        first_tile = jnp.maximum(i - win_tiles, 0)
        q_t = jnp.concatenate([stacked_heads_t(qr_ref, h), zero_rows], axis=0)
        if c * SEL_CHUNK >= WINDOW:
            masks = ([lambda: mask_ref[MASK_LOW]] + [None] * (win_tiles - 1) + [lambda: mask_ref[MASK_DIAG]])
        else:
            masks = [functools.partial(tile_mask, first_tile + t, low_edge=(t == 0))
                     for t in range(win_tiles + 1)]
        yield from attend(out, h, kwn_ref, vwnt_ref, q_t, first_tile, masks)

    def compressed_and_selected_stream(o_cmps, o_sels, h, c):
        qn = stacked_heads_t(qn_ref, h)
        qr = stacked_heads_t(qr_ref, h)

        s = _dot(kcmp_ref[0, h], jnp.concatenate([qn, zero_rows], axis=0))
        n_idx = lax.broadcasted_iota(jnp.int32, (n_cmp_pad, 1), 0)
        valid_c = n_idx * CMP_STRIDE + (CMP_BLOCK - 1) <= qpos
        s = jnp.where(valid_c, s, NEG_BIG)
        m = jnp.max(s, axis=0, keepdims=True)
        e = jnp.where(valid_c, jnp.exp2(s - m), 0.0)
        p_c = e / jnp.maximum(jnp.sum(e, axis=0, keepdims=True), 1e-20)
        o_cmps[h] = _dot(vcmpt_ref[0, h], p_c.astype(_bf16))[0:HEAD_DIM]
        yield

        eligible = blk * SEL_BLOCK <= tpos
        if SEL_CHUNK * (c + 1) <= SEL_TOPN * SEL_BLOCK:
            chosen = eligible
        else:
            p_sum = p_c[:, 0:tq] + p_c[:, tq:2 * tq] + p_c[:, 2 * tq:3 * tq] + p_c[:, 3 * tq:4 * tq]
            p_hi = p_sum.astype(_bf16)
            p_lo = (p_sum - p_hi.astype(_f32)).astype(_bf16)
            imp = _dot(ovt_ref[...], p_hi) + _dot(ovt_ref[...], p_lo)
            cur = tpos // SEL_BLOCK
            forced = (blk == 0) | (blk == cur) | (blk == cur - 1)
            score = jnp.where(forced, FORCE_SCORE, jnp.where(eligible, imp, -jnp.inf))
            rank = jnp.zeros((n_sel, tq), _f32)
            for jp in range(n_sel):
                other = score[jp:jp + 1, :]
                tie = jnp.where(blk > jp, 1.0, 0.0)
                rank = rank + jnp.where(other > score, 1.0, jnp.where(other == score, tie, 0.0))
            chosen = (rank < float(SEL_TOPN)) & eligible
        sel_bias = jnp.where(chosen, 0.0, NEG_BIG).astype(_bf16)
        yield

        q_t = jnp.concatenate([qr, jnp.concatenate([sel_bias] * GQA_GROUP, axis=1),
                               jnp.zeros((LANES - HEAD_DIM - n_sel, rows), _bf16)], axis=0)
        n_tiles = (SEL_CHUNK // tq) * (c + 1)
        first_edge = n_tiles - SEL_CHUNK // tq
        masks = [None if t < first_edge else functools.partial(tile_mask, t) for t in range(n_tiles)]
        yield from attend(o_sels, h, ksl_ref, vslt_ref, q_t, 0, masks)

    o_wins, o_cmps, o_sels = {}, {}, {}
    streams = [
        window_stream(o_wins, 0, c),
        compressed_and_selected_stream(o_cmps, o_sels, 0, c),
        _delayed(window_stream(o_wins, 1, c), STREAM_SKEW),
        _delayed(compressed_and_selected_stream(o_cmps, o_sels, 1, c), STREAM_SKEW),
    ]

    def finish():
        slabs = []
        for h in range(N_KV_HEADS):
            for g in range(GQA_GROUP):
                hq = GQA_GROUP * h + g
                cols = slice(g * tq, (g + 1) * tq)
                slabs.append(gates_t[hq:hq + 1] * o_cmps[h][:, cols]
                             + gates_t[N_ATTN_HEADS + hq:N_ATTN_HEADS + hq + 1] * o_sels[h][:, cols]
                             + gates_t[2 * N_ATTN_HEADS + hq:2 * N_ATTN_HEADS + hq + 1] * o_wins[h][:, cols])
        ot = jnp.concatenate(slabs, axis=0)
        ms = jnp.mean(ot * ot, axis=0, keepdims=True)
        o_ref[0, sub_cols, :] = (ot * lax.rsqrt(ms + EPS) * gout_ref[...]).T.astype(_bf16)

    return streams, finish


def _attn_call(qn_t, qr_t, kcmp, vcmp_t, ksl, vsl_t, kwn, vwn_t, gates, ovt, masks, gout_attn):
    B, _, _, S = qn_t.shape
    tq = ATTN_Q_TILE
    assert tq == LANES and S % SEL_CHUNK == 0 and SEL_CHUNK % tq == 0 and WINDOW % tq == 0
    assert ATTN_UNIT % tq == 0 and kcmp.shape[2] == LANES
    step = tq * ATTN_SUBTILES
    assert SEL_CHUNK % step == 0
    n_cmp_pad = kcmp.shape[2]
    n_sel = S // SEL_BLOCK
    rows = GQA_GROUP * tq
    per_b4 = lambda b, i: (b, 0, 0, 0)
    const2 = lambda b, i: (0, 0)
    vt_spec = pl.BlockSpec((1, N_KV_HEADS, S // LANES, LANES, LANES), lambda b, i: (b, 0, 0, 0, 0))
    return pl.pallas_call(
        _attn_kernel,
        grid=(B, S // step),
        in_specs=[
            pl.BlockSpec((1, N_ATTN_HEADS, HEAD_DIM, step), lambda b, i: (b, 0, 0, i)),
            pl.BlockSpec((1, N_ATTN_HEADS, HEAD_DIM, step), lambda b, i: (b, 0, 0, i)),
            pl.BlockSpec((1, N_KV_HEADS, n_cmp_pad, LANES), per_b4),
            pl.BlockSpec((1, N_KV_HEADS, n_cmp_pad, LANES), per_b4),
            pl.BlockSpec((1, N_KV_HEADS, S, LANES), per_b4),
            vt_spec,
            pl.BlockSpec((1, N_KV_HEADS, S, LANES), per_b4),
            vt_spec,
            pl.BlockSpec((1, step, LANES), lambda b, i: (b, i, 0)),
            pl.BlockSpec((n_sel, n_cmp_pad), const2),
            pl.BlockSpec((4, tq, rows), lambda b, i: (0, 0, 0)),
            pl.BlockSpec((ATTN_WIDTH, tq), const2),
        ],
        out_specs=pl.BlockSpec((1, step, ATTN_WIDTH), lambda b, i: (b, i, 0)),
        out_shape=jax.ShapeDtypeStruct((B, S, ATTN_WIDTH), _bf16),
        compiler_params=pltpu.CompilerParams(
            dimension_semantics=("parallel", "arbitrary"), vmem_limit_bytes=VMEM_LIMIT),
        name="attn",
    )(qn_t, qr_t, kcmp, vcmp_t, ksl, vsl_t, kwn, vwn_t, gates, ovt, masks, gout_attn)


def _ffn_kernel(x_ref, oa_ref, om_ref, wo_ref, gffn_ref, w1_ref, w2_ref, out_ref, act_scr):
    x2 = (x_ref[...] + _dot(oa_ref[...], wo_ref[0:ATTN_WIDTH, :])
          + _dot(om_ref[...], wo_ref[ATTN_WIDTH:ATTN_WIDTH + MIX_WIDTH, :]))
    ms = jnp.mean(x2 * x2, axis=-1, keepdims=True)
    h = (x2 * lax.rsqrt(ms + EPS) * gffn_ref[...]).astype(_bf16)
    for c in range(D_FF // FF_CHUNK):
        cols = slice(c * FF_CHUNK, (c + 1) * FF_CHUNK)
        a = jnp.maximum(_dot(h, w1_ref[:, cols]), 0.0)
        act_scr[:, cols] = (a * a).astype(_bf16)
    out_ref[...] = x2 + _dot(act_scr[...], w2_ref[...])


def _ffn_call(x2d, oa, om, wo, gffn, w1, w2):
    T = x2d.shape[0]
    tm = FFN_TILE
    tok = lambda i: (i, 0)
    const = lambda i: (0, 0)
    return pl.pallas_call(
        _ffn_kernel,
        grid=(T // tm,),
        in_specs=[
            pl.BlockSpec((tm, D_MODEL), tok),
            pl.BlockSpec((tm, ATTN_WIDTH), tok),
            pl.BlockSpec((tm, MIX_WIDTH), tok),
            pl.BlockSpec((D_MODEL, D_MODEL), const),
            pl.BlockSpec((1, D_MODEL), const),
            pl.BlockSpec((D_MODEL, D_FF), const),
            pl.BlockSpec((D_FF, D_MODEL), const),
        ],
        out_specs=pl.BlockSpec((tm, D_MODEL), tok),
        out_shape=jax.ShapeDtypeStruct((T, D_MODEL), _f32),
        scratch_shapes=[pltpu.VMEM((tm, D_FF), _bf16)],
        compiler_params=pltpu.CompilerParams(
            dimension_semantics=("parallel",), vmem_limit_bytes=VMEM_LIMIT),
        name="ffn",
    )(x2d, oa, om, wo, gffn, w1, w2)


def _constants(S):
    half = HEAD_DIM // 2
    inv = ROPE_THETA ** (-jnp.arange(half, dtype=_f32) / half)
    ang = jnp.arange(S, dtype=_f32)[:, None] * inv[None, :]
    cos = jnp.cos(ang)
    sin = jnp.sin(ang)
    cos2 = jnp.concatenate([cos, cos, cos, cos], axis=1)
    sin2 = jnp.concatenate([-sin, sin, -sin, sin], axis=1)
    mlane = np.arange(MXU_DIM)
    bd = (mlane[:, None] // HEAD_DIM == mlane[None, :] // HEAD_DIM).astype(np.float32) / HEAD_DIM
    n_cmp = (S - CMP_BLOCK) // CMP_STRIDE + 1
    n_sel = S // SEL_BLOCK
    n_cmp_pad = S // CMP_STRIDE
    cmp_start = np.arange(n_cmp)[:, None] * CMP_STRIDE
    sel_start = np.arange(n_sel)[None, :] * SEL_BLOCK
    overlap = np.clip(np.minimum(cmp_start + CMP_BLOCK, sel_start + SEL_BLOCK)
                      - np.maximum(cmp_start, sel_start), 0, None).astype(np.float32) / CMP_BLOCK
    ovt = np.zeros((n_sel, n_cmp_pad), np.float32)
    ovt[:, :n_cmp] = overlap.T
    qq = (np.arange(GQA_GROUP * ATTN_Q_TILE) % ATTN_Q_TILE)[None, :]
    kk = np.arange(ATTN_Q_TILE)[:, None]
    masks = np.zeros((4, ATTN_Q_TILE, GQA_GROUP * ATTN_Q_TILE), np.float32)
    masks[MASK_DIAG] = np.where(kk <= qq, 0.0, NEG_BIG)
    masks[MASK_LOW] = np.where(kk > qq, 0.0, NEG_BIG)
    masks[MASK_NONE] = NEG_BIG
    return cos2, sin2, jnp.asarray(bd, _bf16), jnp.asarray(ovt, _bf16), jnp.asarray(masks, _f32)


def _pack_w_in(w):
    w = w.astype(_bf16)
    gate0 = ATTN_WIDTH + 6 * KV_WIDTH
    n_g = N_GATES * N_ATTN_HEADS
    gates = w[:, gate0:gate0 + n_g].reshape(D_MODEL, N_ATTN_HEADS, N_GATES)
    gates = jnp.transpose(gates, (0, 2, 1)).reshape(D_MODEL, n_g)
    pad = jnp.zeros((D_MODEL, _C_KV + _KV_GROUP - gate0 - n_g), w.dtype)
    return jnp.concatenate([w[:, :gate0], gates, pad, w[:, gate0 + n_g:]], axis=1)


def _expand_cmp_w1(w1):
    w = w1.reshape(2, CMP_STRIDE, HEAD_DIM, CMP_HIDDEN)
    z = jnp.zeros_like(w)
    h0 = jnp.concatenate([w, z], axis=-1)
    h1 = jnp.concatenate([z, w], axis=-1)
    both = jnp.stack([h0, h1], axis=2)
    return both.reshape(2, CMP_STRIDE * KV_WIDTH, N_KV_HEADS * CMP_HIDDEN).astype(_bf16)


def _expand_pe(pe):
    p = pe.reshape(2, CMP_STRIDE, 1, HEAD_DIM)
    return jnp.broadcast_to(p, (2, CMP_STRIDE, N_KV_HEADS, HEAD_DIM)).reshape(2, CMP_STRIDE * KV_WIDTH)


def kernel(x, g_mix_norm, w_in, g_q, g_k, cmp_pe, cmp_w1, cmp_w2, g_sgu, sp_w, sp_b, g_out, w_out,
           g_ffn_norm, w_ff1, w_ff2):
    B, S, _ = x.shape
    l = 0
    scale = float(HEAD_DIM ** -0.5 * np.log2(np.e))
    cos2, sin2, bd, ovt, masks = _constants(S)

    w_in_p = _pack_w_in(w_in[l])
    gq = (jnp.tile(g_q[l], N_ATTN_HEADS) * scale)[None, :]
    gksl = jnp.tile(g_k[l, 1], N_KV_HEADS)[None, :]
    gkwn = jnp.tile(g_k[l, 2], N_KV_HEADS)[None, :]
    gsgu = g_sgu[l].reshape(1, MIX_WIDTH)
    spw = sp_w[l].reshape(N_MIX_GROUPS // 2, 2, CHUNK, CHUNK)
    spw = jnp.transpose(spw, (0, 2, 1, 3)).reshape(N_MIX_GROUPS // 2, CHUNK, 2 * CHUNK).astype(_bf16)
    spb = jnp.repeat(sp_b[l].T, HEAD_DIM, axis=1)
    gout = g_out[l]

    qn, qr, kc, vc, ksl, vsl, kwn, vwn, gates, omix = _proj_call(
        x, g_mix_norm[l][None, :], w_in_p, gq, gksl, gkwn, cos2, sin2, bd, gsgu, spw, spb,
        gout[None, ATTN_WIDTH:])

    pe_rows = jnp.stack([_expand_pe(cmp_pe[l, 0]), _expand_pe(cmp_pe[l, 1])])
    w1x = jnp.stack([_expand_cmp_w1(cmp_w1[l, 0]), _expand_cmp_w1(cmp_w1[l, 1])])
    kcmp, vcmp = _compress_call(kc, vc, pe_rows, w1x, cmp_w2[l].astype(_bf16), g_k[l, 0][None, :])

    oattn = _attn_call(qn, qr, kcmp, vcmp, ksl, vsl, kwn, vwn, gates, ovt, masks,
                       jnp.broadcast_to(gout[:ATTN_WIDTH, None], (ATTN_WIDTH, ATTN_Q_TILE)))

    out = _ffn_call(x.reshape(B * S, D_MODEL), oattn.reshape(B * S, ATTN_WIDTH),
                    omix.reshape(B * S, MIX_WIDTH), w_out[l].astype(_bf16), g_ffn_norm[l][None, :],
                    w_ff1[l].astype(_bf16), w_ff2[l].astype(_bf16))
    return out.reshape(B, S, D_MODEL)
```

```python
import functools

import numpy as np
import jax
import jax.numpy as jnp
from jax import lax
from jax.experimental import pallas as pl
from jax.experimental.pallas import tpu as pltpu

D_MODEL = 1024
HEAD_DIM = 64
N_ATTN_HEADS = 8
N_MIX_GROUPS = 8
GQA_GROUP = 4
N_KV_HEADS = 2
ATTN_WIDTH = 512
MIX_WIDTH = 512
KV_WIDTH = 128
N_GATES = 3
CMP_BLOCK = 32
CMP_STRIDE = 16
CMP_HIDDEN = 256
SEL_BLOCK = 64
SEL_TOPN = 16
WINDOW = 512
CHUNK = 128
D_FF = 4 * D_MODEL
ROPE_THETA = 10000.0
EPS = 1e-6
FORCE_SCORE = 1e9

LANES = 128
NEG_BIG = -1e30
PROJ_TILE = 256
PROJ_UNIT = 512
ATTN_Q_TILE = 128
ATTN_SUBTILES = 2
TILE_SKEW = 2
SEL_CHUNK = 512
ATTN_UNIT = 512
STREAM_SKEW = 1
MASK_FULL, MASK_DIAG, MASK_LOW, MASK_NONE = 0, 1, 2, 3
FFN_TILE = 1024
FF_CHUNK = 1024
VMEM_LIMIT = 56 * 1024 * 1024

_C_Q = 0
_C_KV = 512
_KV_KC, _KV_VC, _KV_KSL, _KV_VSL, _KV_KWN, _KV_VWN, _KV_GATE = (128 * n for n in range(7))
_KV_GROUP = 1024
_C_ZU = 1536
_C_ZV = 2048
_IN_COLS_PACKED = 2560
MXU_DIM = 256

_bf16 = jnp.bfloat16
_f32 = jnp.float32


def _dot(a, b):
    return jnp.dot(a, b, preferred_element_type=_f32)


def _group_mean(sq, bd_ref):
    cw = min(sq.shape[1], MXU_DIM)
    bd = bd_ref[0:cw, 0:cw]
    parts = [_dot(sq[:, c * cw:(c + 1) * cw].astype(_bf16), bd) for c in range(sq.shape[1] // cw)]
    return parts[0] if len(parts) == 1 else jnp.concatenate(parts, axis=1)


def _swap_halves(v):
    n = v.shape[1]
    lane = lax.broadcasted_iota(jnp.int32, v.shape, 1)
    return jnp.where((lane % HEAD_DIM) < HEAD_DIM // 2, pltpu.roll(v, n - HEAD_DIM // 2, 1),
                     pltpu.roll(v, HEAD_DIM // 2, 1))


def _proj_kernel(tiles_per_seq, x_ref, gmix_ref, w_ref, gq_ref, gksl_ref, gkwn_ref, cos_ref, sin_ref, bd_ref,
                 gsgu_ref, spw_ref, spb_ref, gomix_ref,
                 qn_ref, qr_ref, kc_ref, vc_ref, ksl_ref, vsl_ref, kwn_ref, vwn_ref, gate_ref, omix_ref,
                 h_even, h_odd, z_even, z_odd, regroup_scr):
    t = pl.program_id(0)

    @pl.when(t == 0)
    def _():
        h_odd[...] = jnp.zeros(h_odd.shape, _bf16)
        z_odd[...] = jnp.zeros(z_odd.shape, _f32)

    pl.when(t % 2 == 0)(functools.partial(
        _proj_step, tiles_per_seq, x_ref, gmix_ref, w_ref, gq_ref, gksl_ref, gkwn_ref, cos_ref, sin_ref, bd_ref,
        gsgu_ref, spw_ref, spb_ref, gomix_ref, qn_ref, qr_ref, kc_ref, vc_ref, ksl_ref, vsl_ref, kwn_ref,
        vwn_ref, gate_ref, omix_ref, h_even, h_odd, z_even, z_odd, regroup_scr))
    pl.when(t % 2 == 1)(functools.partial(
        _proj_step, tiles_per_seq, x_ref, gmix_ref, w_ref, gq_ref, gksl_ref, gkwn_ref, cos_ref, sin_ref, bd_ref,
        gsgu_ref, spw_ref, spb_ref, gomix_ref, qn_ref, qr_ref, kc_ref, vc_ref, ksl_ref, vsl_ref, kwn_ref,
        vwn_ref, gate_ref, omix_ref, h_odd, h_even, z_odd, z_even, regroup_scr))


def _proj_step(tiles_per_seq, x_ref, gmix_ref, w_ref, gq_ref, gksl_ref, gkwn_ref, cos_ref, sin_ref, bd_ref,
               gsgu_ref, spw_ref, spb_ref, gomix_ref,
               qn_ref, qr_ref, kc_ref, vc_ref, ksl_ref, vsl_ref, kwn_ref, vwn_ref, gate_ref, omix_ref,
               h_next, h_prev, z_next, z_prev, regroup_scr):
    tm = x_ref.shape[0]
    i = jnp.maximum(pl.program_id(0) - 2, 0) % tiles_per_seq
    cos = cos_ref[...]
    sin = sin_ref[...]

    def project_stream():
        h = h_prev[...]
        for lo in range(0, _IN_COLS_PACKED, PROJ_UNIT):
            z_next[:, lo:lo + PROJ_UNIT] = _dot(h, w_ref[:, lo:lo + PROJ_UNIT])
            yield

    def prenorm_stream():
        x = x_ref[...]
        ms = jnp.mean(x * x, axis=-1, keepdims=True)
        h_next[...] = (x * lax.rsqrt(ms + EPS) * gmix_ref[...]).astype(_bf16)
        yield

    def query_stream():
        zq = z_prev[:, _C_Q:_C_Q + ATTN_WIDTH]
        qn = zq * lax.rsqrt(_group_mean(zq * zq, bd_ref) + EPS) * gq_ref[...]
        yield
        cos4 = jnp.concatenate([cos] * 4, axis=1)
        sin4 = jnp.concatenate([sin] * 4, axis=1)
        qr = qn * cos4 + _swap_halves(qn) * sin4
        yield
        qn_t, qr_t = qn.T, qr.T
        for hq in range(N_ATTN_HEADS):
            sl = slice(hq * HEAD_DIM, (hq + 1) * HEAD_DIM)
            qn_ref[0, hq] = qn_t[sl].astype(_bf16)
            qr_ref[0, hq] = qr_t[sl].astype(_bf16)

    def kv_stream():
        def segment(off):
            return z_prev[:, _C_KV + off:_C_KV + off + KV_WIDTH]

        for n, (off, out_ref) in enumerate(((_KV_KC, kc_ref), (_KV_VC, vc_ref))):
            regroup_scr[n] = segment(off)
            for tok in range(CMP_STRIDE):
                every_16th = regroup_scr[n, pl.ds(tok, tm // CMP_STRIDE, stride=CMP_STRIDE), :]
                out_ref[0, :, tok * KV_WIDTH:(tok + 1) * KV_WIDTH] = every_16th.astype(_bf16)
        gate_ref[0] = jax.nn.sigmoid(segment(_KV_GATE))

        ones_block = jnp.where(lax.broadcasted_iota(jnp.int32, (HEAD_DIM, LANES), 0) == 0, 1.0, 0.0)
        for off, out_ref in ((_KV_VSL, vsl_ref), (_KV_VWN, vwn_ref)):
            zv2 = segment(off)
            for tt in range(tm // LANES):
                zt = zv2[tt * LANES:(tt + 1) * LANES, :].T
                for hh in range(N_KV_HEADS):
                    out_ref[0, hh, tt] = jnp.concatenate(
                        [zt[hh * HEAD_DIM:(hh + 1) * HEAD_DIM], ones_block], axis=0).astype(_bf16)
            yield

        lane = lax.broadcasted_iota(jnp.int32, (tm, LANES), 1)
        pos = i * tm + lax.broadcasted_iota(jnp.int32, (tm, LANES), 0)
        blk_onehot = jnp.where(lane - HEAD_DIM == pos // SEL_BLOCK, 1.0, 0.0)
        for off, g_ref, out_ref, extra in ((_KV_KSL, gksl_ref, ksl_ref, blk_onehot),
                                           (_KV_KWN, gkwn_ref, kwn_ref, jnp.zeros((tm, LANES), _f32))):
            zk = segment(off)
            kn = zk * lax.rsqrt(_group_mean(zk * zk, bd_ref) + EPS) * g_ref[...]
            kr = kn * cos + _swap_halves(kn) * sin
            out_ref[0, 0] = jnp.where(lane < HEAD_DIM, kr, extra).astype(_bf16)
            out_ref[0, 1] = jnp.where(lane < HEAD_DIM, pltpu.roll(kr, HEAD_DIM, 1), extra).astype(_bf16)
            yield

    row = lax.broadcasted_iota(jnp.int32, (CHUNK, 2 * CHUNK), 0)
    colw = lax.broadcasted_iota(jnp.int32, (CHUNK, 2 * CHUNK), 1) % CHUNK
    causal_w = colw <= row
    lane_c = lax.broadcasted_iota(jnp.int32, (CHUNK, LANES), 1)

    def gmlp_stream(c):
        rows = slice(c * CHUNK, (c + 1) * CHUNK)
        zu = jax.nn.gelu(z_prev[rows, _C_ZU:_C_ZU + MIX_WIDTH])
        yield
        zv = jax.nn.gelu(z_prev[rows, _C_ZV:_C_ZV + MIX_WIDTH])
        yield
        vn = zv * lax.rsqrt(_group_mean(zv * zv, bd_ref) + EPS) * gsgu_ref[...]
        yield
        sv_parts = []
        for p in range(N_MIX_GROUPS // 2):
            vp = vn[:, p * LANES:(p + 1) * LANES]
            rhs = jnp.concatenate([jnp.where(lane_c < HEAD_DIM, vp, 0.0),
                                   jnp.where(lane_c < HEAD_DIM, 0.0, vp)], axis=0).astype(_bf16)
            w_pair = jnp.where(causal_w, spw_ref[p], jnp.zeros((), _bf16))
            sv_parts.append(_dot(w_pair, rhs))
        yield
        sv = jnp.concatenate(sv_parts, axis=1) + spb_ref[...]
        om = zu * sv
        oms = jnp.mean(om * om, axis=-1, keepdims=True)
        omix_ref[0, rows] = (om * lax.rsqrt(oms + EPS) * gomix_ref[...]).astype(_bf16)

    _interleave([project_stream(), query_stream(), kv_stream()] + [gmlp_stream(c) for c in range(tm // CHUNK)]
                + [prenorm_stream()])


def _proj_call(x, gmix, w_in_p, gq, gksl, gkwn, cos, sin, bd, gsgu, spw, spb, gomix):
    B, S, _ = x.shape
    tm = PROJ_TILE
    nt = S // tm
    n_tiles = B * nt
    const2 = lambda t: (0, 0)
    const3 = lambda t: (0, 0, 0)

    def done(t):
        return jnp.maximum(t - 2, 0)

    tok3 = lambda t: (done(t) // nt, done(t) % nt, 0)
    head4 = lambda t: (done(t) // nt, 0, done(t) % nt, 0)
    out_shape = (
        jax.ShapeDtypeStruct((B, N_ATTN_HEADS, HEAD_DIM, S), _bf16),
        jax.ShapeDtypeStruct((B, N_ATTN_HEADS, HEAD_DIM, S), _bf16),
        jax.ShapeDtypeStruct((B, S // CMP_STRIDE, CMP_STRIDE * KV_WIDTH), _bf16),
        jax.ShapeDtypeStruct((B, S // CMP_STRIDE, CMP_STRIDE * KV_WIDTH), _bf16),
        jax.ShapeDtypeStruct((B, N_KV_HEADS, S, LANES), _bf16),
        jax.ShapeDtypeStruct((B, N_KV_HEADS, S // LANES, LANES, LANES), _bf16),
        jax.ShapeDtypeStruct((B, N_KV_HEADS, S, LANES), _bf16),
        jax.ShapeDtypeStruct((B, N_KV_HEADS, S // LANES, LANES, LANES), _bf16),
        jax.ShapeDtypeStruct((B, S, LANES), _f32),
        jax.ShapeDtypeStruct((B, S, MIX_WIDTH), _bf16),
    )
    q_spec = pl.BlockSpec((1, N_ATTN_HEADS, HEAD_DIM, tm), lambda t: (done(t) // nt, 0, 0, done(t) % nt))
    k_spec = pl.BlockSpec((1, N_KV_HEADS, tm, LANES), head4)
    t_spec = pl.BlockSpec((1, tm, LANES), tok3)
    g_spec = pl.BlockSpec((1, tm // CMP_STRIDE, CMP_STRIDE * KV_WIDTH), tok3)
    vt_spec = pl.BlockSpec((1, N_KV_HEADS, tm // LANES, LANES, LANES),
                           lambda t: (done(t) // nt, 0, done(t) % nt, 0, 0))
    return pl.pallas_call(
        functools.partial(_proj_kernel, nt),
        grid=(n_tiles + 2,),
        in_specs=[
            pl.BlockSpec((tm, D_MODEL), lambda t: (jnp.minimum(t, n_tiles - 1), 0)),
            pl.BlockSpec((1, D_MODEL), const2),
            pl.BlockSpec((D_MODEL, _IN_COLS_PACKED), const2),
            pl.BlockSpec((1, ATTN_WIDTH), const2),
            pl.BlockSpec((1, KV_WIDTH), const2),
            pl.BlockSpec((1, KV_WIDTH), const2),
            pl.BlockSpec((tm, LANES), lambda t: (done(t) % nt, 0)),
            pl.BlockSpec((tm, LANES), lambda t: (done(t) % nt, 0)),
            pl.BlockSpec((MXU_DIM, MXU_DIM), const2),
            pl.BlockSpec((1, MIX_WIDTH), const2),
            pl.BlockSpec((N_MIX_GROUPS // 2, CHUNK, 2 * CHUNK), const3),
            pl.BlockSpec((CHUNK, MIX_WIDTH), const2),
            pl.BlockSpec((1, MIX_WIDTH), const2),
        ],
        out_specs=(q_spec, q_spec, g_spec, g_spec, k_spec, vt_spec, k_spec, vt_spec, t_spec,
                   pl.BlockSpec((1, tm, MIX_WIDTH), tok3)),
        out_shape=out_shape,
        scratch_shapes=[pltpu.VMEM((tm, D_MODEL), _bf16), pltpu.VMEM((tm, D_MODEL), _bf16),
                        pltpu.VMEM((tm, _IN_COLS_PACKED), _f32), pltpu.VMEM((tm, _IN_COLS_PACKED), _f32),
                        pltpu.VMEM((2, tm, KV_WIDTH), _f32)],
        compiler_params=pltpu.CompilerParams(
            dimension_semantics=("arbitrary",), vmem_limit_bytes=VMEM_LIMIT),
        name="proj",
    )(x.reshape(B * S, D_MODEL), gmix, w_in_p, gq, gksl, gkwn, cos, sin, bd, gsgu, spw, spb, gomix)


def _compress_kernel(kc_ref, vc_ref, pe_ref, w1_ref, w2_ref, gk_ref, kcmp_ref, vcmp_ref):
    n_rows = kc_ref.shape[1]
    zero_half = jnp.zeros((n_rows, HEAD_DIM), _f32)
    for t, (src_ref, out_ref) in enumerate(((kc_ref, kcmp_ref), (vc_ref, vcmp_ref))):
        g = src_ref[0].astype(_f32)
        first = _dot((g + pe_ref[t, 0:1]).astype(_bf16), w1_ref[t, 0])
        second = _dot((g + pe_ref[t, 1:2]).astype(_bf16), w1_ref[t, 1])
        hid = jax.nn.gelu(first + pltpu.roll(second, n_rows - 1, 0)).astype(_bf16)
        heads = [_dot(hid[:, hh * CMP_HIDDEN:(hh + 1) * CMP_HIDDEN], w2_ref[t]) for hh in range(N_KV_HEADS)]
        if t == 0:
            heads = [c * lax.rsqrt(jnp.mean(c * c, axis=-1, keepdims=True) + EPS) * gk_ref[...] for c in heads]
            for hh in range(N_KV_HEADS):
                out_ref[0, hh] = jnp.concatenate([heads[hh], zero_half], axis=1).astype(_bf16)
        else:
            for hh in range(N_KV_HEADS):
                vt = jnp.concatenate([heads[hh], zero_half], axis=1).T
                out_ref[0, hh] = vt.astype(_bf16)


def _compress_call(kc_g, vc_g, pe_rows, w1x, w2, gk0):
    B, n_rows, width = kc_g.shape
    return pl.pallas_call(
        _compress_kernel,
        grid=(B,),
        in_specs=[
            pl.BlockSpec((1, n_rows, width), lambda b: (b, 0, 0)),
            pl.BlockSpec((1, n_rows, width), lambda b: (b, 0, 0)),
            pl.BlockSpec((2, 2, width), lambda b: (0, 0, 0)),
            pl.BlockSpec((2, 2, width, N_KV_HEADS * CMP_HIDDEN), lambda b: (0, 0, 0, 0)),
            pl.BlockSpec((2, CMP_HIDDEN, HEAD_DIM), lambda b: (0, 0, 0)),
            pl.BlockSpec((1, HEAD_DIM), lambda b: (0, 0)),
        ],
        out_specs=(pl.BlockSpec((1, N_KV_HEADS, n_rows, LANES), lambda b: (b, 0, 0, 0)),
                   pl.BlockSpec((1, N_KV_HEADS, n_rows, LANES), lambda b: (b, 0, 0, 0))),
        out_shape=(jax.ShapeDtypeStruct((B, N_KV_HEADS, n_rows, LANES), _bf16),
                   jax.ShapeDtypeStruct((B, N_KV_HEADS, n_rows, LANES), _bf16)),
        compiler_params=pltpu.CompilerParams(
            dimension_semantics=("parallel",), vmem_limit_bytes=VMEM_LIMIT),
        name="compress",
    )(kc_g, vc_g, pe_rows, w1x, w2, gk0)


def _interleave(streams):
    live = list(streams)
    while live:
        for g in list(live):
            try:
                next(g)
            except StopIteration:
                live.remove(g)


def _delayed(stream, units):
    for _ in range(units):
        yield
    yield from stream


def _attn_kernel(qn_ref, qr_ref, kcmp_ref, vcmpt_ref, ksl_ref, vslt_ref, kwn_ref, vwnt_ref, gate_ref,
                 ovt_ref, mask_ref, gout_ref, o_ref):
    refs = (qn_ref, qr_ref, kcmp_ref, vcmpt_ref, ksl_ref, vslt_ref, kwn_ref, vwnt_ref, gate_ref,
            ovt_ref, mask_ref, gout_ref, o_ref)
    S = ksl_ref.shape[2]

    def step_body(c):
        tiles = [_attn_tile(sub, c, *refs) for sub in range(ATTN_SUBTILES)]
        _interleave([_delayed(stream, sub * TILE_SKEW) for sub, (streams, _) in enumerate(tiles)
                     for stream in streams])
        for _, finish in tiles:
            finish()

    cls = (pl.program_id(1) * ATTN_SUBTILES * ATTN_Q_TILE) // SEL_CHUNK
    for c in range(S // SEL_CHUNK):
        pl.when(cls == c)(functools.partial(step_body, c))


def _attn_tile(sub, c, qn_ref, qr_ref, kcmp_ref, vcmpt_ref, ksl_ref, vslt_ref, kwn_ref, vwnt_ref, gate_ref,
               ovt_ref, mask_ref, gout_ref, o_ref):
    tq = ATTN_Q_TILE
    sub_cols = slice(sub * tq, (sub + 1) * tq)
    rows = GQA_GROUP * tq
    S = ksl_ref.shape[2]
    n_cmp_pad = kcmp_ref.shape[2]
    n_sel = S // SEL_BLOCK
    win_tiles = WINDOW // tq
    unit_tiles = ATTN_UNIT // tq
    i = pl.program_id(1) * ATTN_SUBTILES + sub
    q0 = i * tq

    qpos = q0 + lax.broadcasted_iota(jnp.int32, (1, rows), 1) % tq
    blk = lax.broadcasted_iota(jnp.int32, (n_sel, tq), 0)
    tpos = q0 + lax.broadcasted_iota(jnp.int32, (n_sel, tq), 1)
    gates_t = gate_ref[0, sub_cols, :].T
    zero_rows = jnp.zeros((HEAD_DIM, rows), _bf16)

    def stacked_heads_t(ref, h):
        return jnp.concatenate([ref[0, GQA_GROUP * h + g, :, sub_cols] for g in range(GQA_GROUP)], axis=1)

    def tile_mask(key_tile, low_edge=False):
        kind = jnp.where(key_tile < i, MASK_FULL, jnp.where(key_tile == i, MASK_DIAG, MASK_NONE))
        if low_edge:
            kind = jnp.where(i >= win_tiles, MASK_LOW, kind)
        return mask_ref[kind]

    def attend(out, h, k_ref, vt_ref, q_t, first_tile, tile_masks):
        n_tiles = len(tile_masks)
        scores, mx = [], None
        for u0 in range(0, n_tiles, unit_tiles):
            nt = min(unit_tiles, n_tiles - u0)
            if isinstance(first_tile, int):
                keys = slice((first_tile + u0) * tq, (first_tile + u0 + nt) * tq)
            else:
                keys = pl.ds(pl.multiple_of((first_tile + u0) * tq, tq), nt * tq)
            sc = _dot(k_ref[0, h, keys, :], q_t)
            if any(tile_masks[u0 + t] is not None for t in range(nt)):
                sc = jnp.concatenate(
                    [sc[t * tq:(t + 1) * tq] if tile_masks[u0 + t] is None
                     else sc[t * tq:(t + 1) * tq] + tile_masks[u0 + t]() for t in range(nt)], axis=0)
            scores.append(sc)
            cm = jnp.max(sc, axis=0, keepdims=True)
            mx = cm if mx is None else jnp.maximum(mx, cm)
            yield
        probs = []
        for sc in scores:
            probs.append(jnp.exp2(sc - mx).astype(_bf16))
            yield
        vt = jnp.concatenate([vt_ref[0, h, first_tile + t] for t in range(n_tiles)], axis=1)
        acc = _dot(vt, jnp.concatenate(probs, axis=0))
        out[h] = acc[0:HEAD_DIM] / acc[HEAD_DIM:HEAD_DIM + 1]
        yield

    def window_stream(out, h, c):
        first_tile = jnp.maximum(i - win_tiles, 0)
        q_t = jnp.concatenate([stacked_heads_t(qr_ref, h), zero_rows], axis=0)
        if c * SEL_CHUNK >= WINDOW:
            masks = ([lambda: mask_ref[MASK_LOW]] + [None] * (win_tiles - 1) + [lambda: mask_ref[MASK_DIAG]])
        else:
            masks = [functools.partial(tile_mask, first_tile + t, low_edge=(t == 0))
                     for t in range(win_tiles + 1)]
        yield from attend(out, h, kwn_ref, vwnt_ref, q_t, first_tile, masks)

    def compressed_and_selected_stream(o_cmps, o_sels, h, c):
        qn = stacked_heads_t(qn_ref, h)
        qr = stacked_heads_t(qr_ref, h)

        s = _dot(kcmp_ref[0, h], jnp.concatenate([qn, zero_rows], axis=0))
        n_idx = lax.broadcasted_iota(jnp.int32, (n_cmp_pad, 1), 0)
        valid_c = n_idx * CMP_STRIDE + (CMP_BLOCK - 1) <= qpos
        s = jnp.where(valid_c, s, NEG_BIG)
        m = jnp.max(s, axis=0, keepdims=True)
        e = jnp.where(valid_c, jnp.exp2(s - m), 0.0)
        p_c = e / jnp.maximum(jnp.sum(e, axis=0, keepdims=True), 1e-20)
        o_cmps[h] = _dot(vcmpt_ref[0, h], p_c.astype(_bf16))[0:HEAD_DIM]
        yield

        eligible = blk * SEL_BLOCK <= tpos
        if SEL_CHUNK * (c + 1) <= SEL_TOPN * SEL_BLOCK:
            chosen = eligible
        else:
            p_sum = p_c[:, 0:tq] + p_c[:, tq:2 * tq] + p_c[:, 2 * tq:3 * tq] + p_c[:, 3 * tq:4 * tq]
            p_hi = p_sum.astype(_bf16)
            p_lo = (p_sum - p_hi.astype(_f32)).astype(_bf16)
            imp = _dot(ovt_ref[...], p_hi) + _dot(ovt_ref[...], p_lo)
            cur = tpos // SEL_BLOCK
            forced = (blk == 0) | (blk == cur) | (blk == cur - 1)
            score = jnp.where(forced, FORCE_SCORE, jnp.where(eligible, imp, -jnp.inf))
            rank = jnp.zeros((n_sel, tq), _f32)
            for jp in range(n_sel):
                other = score[jp:jp + 1, :]
                tie = jnp.where(blk > jp, 1.0, 0.0)
                rank = rank + jnp.where(other > score, 1.0, jnp.where(other == score, tie, 0.0))
            chosen = (rank < float(SEL_TOPN)) & eligible
        sel_bias = jnp.where(chosen, 0.0, NEG_BIG).astype(_bf16)
        yield

        q_t = jnp.concatenate([qr, jnp.concatenate([sel_bias] * GQA_GROUP, axis=1),
                               jnp.zeros((LANES - HEAD_DIM - n_sel, rows), _bf16)], axis=0)
        n_tiles = (SEL_CHUNK // tq) * (c + 1)
        first_edge = n_tiles - SEL_CHUNK // tq
        masks = [None if t < first_edge else functools.partial(tile_mask, t) for t in range(n_tiles)]
        yield from attend(o_sels, h, ksl_ref, vslt_ref, q_t, 0, masks)

    o_wins, o_cmps, o_sels = {}, {}, {}
    streams = [
        window_stream(o_wins, 0, c),
        compressed_and_selected_stream(o_cmps, o_sels, 0, c),
        _delayed(window_stream(o_wins, 1, c), STREAM_SKEW),
        _delayed(compressed_and_selected_stream(o_cmps, o_sels, 1, c), STREAM_SKEW),
    ]

    def finish():
        slabs = []
        for h in range(N_KV_HEADS):
            for g in range(GQA_GROUP):
                hq = GQA_GROUP * h + g
                cols = slice(g * tq, (g + 1) * tq)
                slabs.append(gates_t[hq:hq + 1] * o_cmps[h][:, cols]
                             + gates_t[N_ATTN_HEADS + hq:N_ATTN_HEADS + hq + 1] * o_sels[h][:, cols]
                             + gates_t[2 * N_ATTN_HEADS + hq:2 * N_ATTN_HEADS + hq + 1] * o_wins[h][:, cols])
        ot = jnp.concatenate(slabs, axis=0)
        ms = jnp.mean(ot * ot, axis=0, keepdims=True)
        o_ref[0, sub_cols, :] = (ot * lax.rsqrt(ms + EPS) * gout_ref[...]).T.astype(_bf16)

    return streams, finish


def _attn_call(qn_t, qr_t, kcmp, vcmp_t, ksl, vsl_t, kwn, vwn_t, gates, ovt, masks, gout_attn):
    B, _, _, S = qn_t.shape
    tq = ATTN_Q_TILE
    assert tq == LANES and S % SEL_CHUNK == 0 and SEL_CHUNK % tq == 0 and WINDOW % tq == 0
    assert ATTN_UNIT % tq == 0 and kcmp.shape[2] == LANES
    step = tq * ATTN_SUBTILES
    assert SEL_CHUNK % step == 0
    n_cmp_pad = kcmp.shape[2]
    n_sel = S // SEL_BLOCK
    rows = GQA_GROUP * tq
    per_b4 = lambda b, i: (b, 0, 0, 0)
    const2 = lambda b, i: (0, 0)
    vt_spec = pl.BlockSpec((1, N_KV_HEADS, S // LANES, LANES, LANES), lambda b, i: (b, 0, 0, 0, 0))
    return pl.pallas_call(
        _attn_kernel,
        grid=(B, S // step),
        in_specs=[
            pl.BlockSpec((1, N_ATTN_HEADS, HEAD_DIM, step), lambda b, i: (b, 0, 0, i)),
            pl.BlockSpec((1, N_ATTN_HEADS, HEAD_DIM, step), lambda b, i: (b, 0, 0, i)),
            pl.BlockSpec((1, N_KV_HEADS, n_cmp_pad, LANES), per_b4),
            pl.BlockSpec((1, N_KV_HEADS, n_cmp_pad, LANES), per_b4),
            pl.BlockSpec((1, N_KV_HEADS, S, LANES), per_b4),
            vt_spec,
            pl.BlockSpec((1, N_KV_HEADS, S, LANES), per_b4),
            vt_spec,
            pl.BlockSpec((1, step, LANES), lambda b, i: (b, i, 0)),
            pl.BlockSpec((n_sel, n_cmp_pad), const2),
            pl.BlockSpec((4, tq, rows), lambda b, i: (0, 0, 0)),
            pl.BlockSpec((ATTN_WIDTH, tq), const2),
        ],
        out_specs=pl.BlockSpec((1, step, ATTN_WIDTH), lambda b, i: (b, i, 0)),
        out_shape=jax.ShapeDtypeStruct((B, S, ATTN_WIDTH), _bf16),
        compiler_params=pltpu.CompilerParams(
            dimension_semantics=("parallel", "arbitrary"), vmem_limit_bytes=VMEM_LIMIT),
        name="attn",
    )(qn_t, qr_t, kcmp, vcmp_t, ksl, vsl_t, kwn, vwn_t, gates, ovt, masks, gout_attn)


def _ffn_kernel(x_ref, oa_ref, om_ref, wo_ref, gffn_ref, w1_ref, w2_ref, out_ref, act_scr):
    x2 = (x_ref[...] + _dot(oa_ref[...], wo_ref[0:ATTN_WIDTH, :])
          + _dot(om_ref[...], wo_ref[ATTN_WIDTH:ATTN_WIDTH + MIX_WIDTH, :]))
    ms = jnp.mean(x2 * x2, axis=-1, keepdims=True)
    h = (x2 * lax.rsqrt(ms + EPS) * gffn_ref[...]).astype(_bf16)
    for c in range(D_FF // FF_CHUNK):
        cols = slice(c * FF_CHUNK, (c + 1) * FF_CHUNK)
        a = jnp.maximum(_dot(h, w1_ref[:, cols]), 0.0)
        act_scr[:, cols] = (a * a).astype(_bf16)
    out_ref[...] = x2 + _dot(act_scr[...], w2_ref[...])


def _ffn_call(x2d, oa, om, wo, gffn, w1, w2):
    T = x2d.shape[0]
    tm = FFN_TILE
    tok = lambda i: (i, 0)
    const = lambda i: (0, 0)
    once = pl.Buffered(1)
    return pl.pallas_call(
        _ffn_kernel,
        grid=(T // tm,),
        in_specs=[
            pl.BlockSpec((tm, D_MODEL), tok),
            pl.BlockSpec((tm, ATTN_WIDTH), tok),
            pl.BlockSpec((tm, MIX_WIDTH), tok),
            pl.BlockSpec((D_MODEL, D_MODEL), const, pipeline_mode=once),
            pl.BlockSpec((1, D_MODEL), const),
            pl.BlockSpec((D_MODEL, D_FF), const, pipeline_mode=once),
            pl.BlockSpec((D_FF, D_MODEL), const, pipeline_mode=once),
        ],
        out_specs=pl.BlockSpec((tm, D_MODEL), tok),
        out_shape=jax.ShapeDtypeStruct((T, D_MODEL), _f32),
        scratch_shapes=[pltpu.VMEM((tm, D_FF), _bf16)],
        compiler_params=pltpu.CompilerParams(
            dimension_semantics=("parallel",), vmem_limit_bytes=VMEM_LIMIT),
        name="ffn",
    )(x2d, oa, om, wo, gffn, w1, w2)


def _constants(S):
    half = HEAD_DIM // 2
    inv = ROPE_THETA ** (-jnp.arange(half, dtype=_f32) / half)
    ang = jnp.arange(S, dtype=_f32)[:, None] * inv[None, :]
    cos = jnp.cos(ang)
    sin = jnp.sin(ang)
    cos2 = jnp.concatenate([cos, cos, cos, cos], axis=1)
    sin2 = jnp.concatenate([-sin, sin, -sin, sin], axis=1)
    mlane = np.arange(MXU_DIM)
    bd = (mlane[:, None] // HEAD_DIM == mlane[None, :] // HEAD_DIM).astype(np.float32) / HEAD_DIM
    n_cmp = (S - CMP_BLOCK) // CMP_STRIDE + 1
    n_sel = S // SEL_BLOCK
    n_cmp_pad = S // CMP_STRIDE
    cmp_start = np.arange(n_cmp)[:, None] * CMP_STRIDE
    sel_start = np.arange(n_sel)[None, :] * SEL_BLOCK
    overlap = np.clip(np.minimum(cmp_start + CMP_BLOCK, sel_start + SEL_BLOCK)
                      - np.maximum(cmp_start, sel_start), 0, None).astype(np.float32) / CMP_BLOCK
    ovt = np.zeros((n_sel, n_cmp_pad), np.float32)
    ovt[:, :n_cmp] = overlap.T
    qq = (np.arange(GQA_GROUP * ATTN_Q_TILE) % ATTN_Q_TILE)[None, :]
    kk = np.arange(ATTN_Q_TILE)[:, None]
    masks = np.zeros((4, ATTN_Q_TILE, GQA_GROUP * ATTN_Q_TILE), np.float32)
    masks[MASK_DIAG] = np.where(kk <= qq, 0.0, NEG_BIG)
    masks[MASK_LOW] = np.where(kk > qq, 0.0, NEG_BIG)
    masks[MASK_NONE] = NEG_BIG
    return cos2, sin2, jnp.asarray(bd, _bf16), jnp.asarray(ovt, _bf16), jnp.asarray(masks, _f32)


def _pack_w_in(w):
    w = w.astype(_bf16)
    gate0 = ATTN_WIDTH + 6 * KV_WIDTH
    n_g = N_GATES * N_ATTN_HEADS
    gates = w[:, gate0:gate0 + n_g].reshape(D_MODEL, N_ATTN_HEADS, N_GATES)
    gates = jnp.transpose(gates, (0, 2, 1)).reshape(D_MODEL, n_g)
    pad = jnp.zeros((D_MODEL, _C_KV + _KV_GROUP - gate0 - n_g), w.dtype)
    return jnp.concatenate([w[:, :gate0], gates, pad, w[:, gate0 + n_g:]], axis=1)


def _expand_cmp_w1(w1):
    w = w1.reshape(2, CMP_STRIDE, HEAD_DIM, CMP_HIDDEN)
    z = jnp.zeros_like(w)
    h0 = jnp.concatenate([w, z], axis=-1)
    h1 = jnp.concatenate([z, w], axis=-1)
    both = jnp.stack([h0, h1], axis=2)
    return both.reshape(2, CMP_STRIDE * KV_WIDTH, N_KV_HEADS * CMP_HIDDEN).astype(_bf16)


def _expand_pe(pe):
    p = pe.reshape(2, CMP_STRIDE, 1, HEAD_DIM)
    return jnp.broadcast_to(p, (2, CMP_STRIDE, N_KV_HEADS, HEAD_DIM)).reshape(2, CMP_STRIDE * KV_WIDTH)


def kernel(x, g_mix_norm, w_in, g_q, g_k, cmp_pe, cmp_w1, cmp_w2, g_sgu, sp_w, sp_b, g_out, w_out,
           g_ffn_norm, w_ff1, w_ff2):
    B, S, _ = x.shape
    l = 0
    scale = float(HEAD_DIM ** -0.5 * np.log2(np.e))
    cos2, sin2, bd, ovt, masks = _constants(S)

    w_in_p = _pack_w_in(w_in[l])
    gq = (jnp.tile(g_q[l], N_ATTN_HEADS) * scale)[None, :]
    gksl = jnp.tile(g_k[l, 1], N_KV_HEADS)[None, :]
    gkwn = jnp.tile(g_k[l, 2], N_KV_HEADS)[None, :]
    gsgu = g_sgu[l].reshape(1, MIX_WIDTH)
    spw = sp_w[l].reshape(N_MIX_GROUPS // 2, 2, CHUNK, CHUNK)
    spw = jnp.transpose(spw, (0, 2, 1, 3)).reshape(N_MIX_GROUPS // 2, CHUNK, 2 * CHUNK).astype(_bf16)
    spb = jnp.repeat(sp_b[l].T, HEAD_DIM, axis=1)
    gout = g_out[l]

    qn, qr, kc, vc, ksl, vsl, kwn, vwn, gates, omix = _proj_call(
        x, g_mix_norm[l][None, :], w_in_p, gq, gksl, gkwn, cos2, sin2, bd, gsgu, spw, spb,
        gout[None, ATTN_WIDTH:])

    pe_rows = jnp.stack([_expand_pe(cmp_pe[l, 0]), _expand_pe(cmp_pe[l, 1])])
    w1x = jnp.stack([_expand_cmp_w1(cmp_w1[l, 0]), _expand_cmp_w1(cmp_w1[l, 1])])
    kcmp, vcmp = _compress_call(kc, vc, pe_rows, w1x, cmp_w2[l].astype(_bf16), g_k[l, 0][None, :])

    oattn = _attn_call(qn, qr, kcmp, vcmp, ksl, vsl, kwn, vwn, gates, ovt, masks,
                       jnp.broadcast_to(gout[:ATTN_WIDTH, None], (ATTN_WIDTH, ATTN_Q_TILE)))

    out = _ffn_call(x.reshape(B * S, D_MODEL), oattn.reshape(B * S, ATTN_WIDTH),
                    omix.reshape(B * S, MIX_WIDTH), w_out[l].astype(_bf16), g_ffn_norm[l][None, :],
                    w_ff1[l].astype(_bf16), w_ff2[l].astype(_bf16))
    return out.reshape(B, S, D_MODEL)
```

```python
import functools

import numpy as np
import jax
import jax.numpy as jnp
from jax import lax
from jax.experimental import pallas as pl
from jax.experimental.pallas import tpu as pltpu

D_MODEL = 1024
HEAD_DIM = 64
N_ATTN_HEADS = 8
N_MIX_GROUPS = 8
GQA_GROUP = 4
N_KV_HEADS = 2
ATTN_WIDTH = 512
MIX_WIDTH = 512
KV_WIDTH = 128
N_GATES = 3
CMP_BLOCK = 32
CMP_STRIDE = 16
CMP_HIDDEN = 256
SEL_BLOCK = 64
SEL_TOPN = 16
WINDOW = 512
CHUNK = 128
D_FF = 4 * D_MODEL
ROPE_THETA = 10000.0
EPS = 1e-6
FORCE_SCORE = 1e9

LANES = 128
NEG_BIG = -1e30
PROJ_TILE = 256
X_SLOTS = 3
PROJ_UNIT = 512
ATTN_Q_TILE = 128
ATTN_SUBTILES = 2
TILE_SKEW = 2
SEL_CHUNK = 512
ATTN_UNIT = 512
STREAM_SKEW = 1
MASK_FULL, MASK_DIAG, MASK_LOW, MASK_NONE = 0, 1, 2, 3
FFN_TILE = 1024
FF_CHUNK = 1024
VMEM_LIMIT = 56 * 1024 * 1024

_C_Q = 0
_C_KV = 512
_KV_KC, _KV_VC, _KV_KSL, _KV_VSL, _KV_KWN, _KV_VWN, _KV_GATE = (128 * n for n in range(7))
_KV_GROUP = 1024
_C_ZU = 1536
_C_ZV = 2048
_IN_COLS_PACKED = 2560
MXU_DIM = 256

_bf16 = jnp.bfloat16
_f32 = jnp.float32


def _dot(a, b):
    return jnp.dot(a, b, preferred_element_type=_f32)


def _group_mean(sq, bd_ref):
    cw = min(sq.shape[1], MXU_DIM)
    bd = bd_ref[0:cw, 0:cw]
    parts = [_dot(sq[:, c * cw:(c + 1) * cw].astype(_bf16), bd) for c in range(sq.shape[1] // cw)]
    return parts[0] if len(parts) == 1 else jnp.concatenate(parts, axis=1)


def _swap_halves(v):
    n = v.shape[1]
    lane = lax.broadcasted_iota(jnp.int32, v.shape, 1)
    return jnp.where((lane % HEAD_DIM) < HEAD_DIM // 2, pltpu.roll(v, n - HEAD_DIM // 2, 1),
                     pltpu.roll(v, HEAD_DIM // 2, 1))


def _proj_kernel(tiles_per_seq, n_tiles, x_hbm, gmix_ref, w_ref, gq_ref, gksl_ref, gkwn_ref, cos_ref, sin_ref,
                 bd_ref, gsgu_ref, spw_ref, spb_ref, gomix_ref,
                 qn_ref, qr_ref, kc_ref, vc_ref, ksl_ref, vsl_ref, kwn_ref, vwn_ref, gate_ref, omix_ref,
                 h_even, h_odd, z_even, z_odd, regroup_scr, x_buf, x_sem):
    t = pl.program_id(0)
    tm = x_buf.shape[1]
    n_steps = n_tiles + 2

    def x_copy(step):
        tile = jnp.minimum(step, n_tiles - 1)
        slot = step % X_SLOTS
        return pltpu.make_async_copy(x_hbm.at[pl.ds(pl.multiple_of(tile * tm, tm), tm), :],
                                     x_buf.at[slot], x_sem.at[slot])

    @pl.when(t == 0)
    def _():
        x_copy(t).start()
        x_copy(t + 1).start()
        h_odd[...] = jnp.zeros(h_odd.shape, _bf16)
        z_odd[...] = jnp.zeros(z_odd.shape, _f32)

    @pl.when(t + 2 < n_steps)
    def _():
        x_copy(t + 2).start()

    x_copy(t).wait()
    x_ref = x_buf.at[t % X_SLOTS]

    pl.when(t % 2 == 0)(functools.partial(
        _proj_step, tiles_per_seq, x_ref, gmix_ref, w_ref, gq_ref, gksl_ref, gkwn_ref, cos_ref, sin_ref, bd_ref,
        gsgu_ref, spw_ref, spb_ref, gomix_ref, qn_ref, qr_ref, kc_ref, vc_ref, ksl_ref, vsl_ref, kwn_ref,
        vwn_ref, gate_ref, omix_ref, h_even, h_odd, z_even, z_odd, regroup_scr))
    pl.when(t % 2 == 1)(functools.partial(
        _proj_step, tiles_per_seq, x_ref, gmix_ref, w_ref, gq_ref, gksl_ref, gkwn_ref, cos_ref, sin_ref, bd_ref,
        gsgu_ref, spw_ref, spb_ref, gomix_ref, qn_ref, qr_ref, kc_ref, vc_ref, ksl_ref, vsl_ref, kwn_ref,
        vwn_ref, gate_ref, omix_ref, h_odd, h_even, z_odd, z_even, regroup_scr))


def _proj_step(tiles_per_seq, x_ref, gmix_ref, w_ref, gq_ref, gksl_ref, gkwn_ref, cos_ref, sin_ref, bd_ref,
               gsgu_ref, spw_ref, spb_ref, gomix_ref,
               qn_ref, qr_ref, kc_ref, vc_ref, ksl_ref, vsl_ref, kwn_ref, vwn_ref, gate_ref, omix_ref,
               h_next, h_prev, z_next, z_prev, regroup_scr):
    tm = x_ref.shape[0]
    i = jnp.maximum(pl.program_id(0) - 2, 0) % tiles_per_seq
    cos = cos_ref[...]
    sin = sin_ref[...]

    def project_stream():
        h = h_prev[...]
        for lo in range(0, _IN_COLS_PACKED, PROJ_UNIT):
            z_next[:, lo:lo + PROJ_UNIT] = _dot(h, w_ref[:, lo:lo + PROJ_UNIT])
            yield

    def prenorm_stream():
        x = x_ref[...]
        ms = jnp.mean(x * x, axis=-1, keepdims=True)
        h_next[...] = (x * lax.rsqrt(ms + EPS) * gmix_ref[...]).astype(_bf16)
        yield

    def query_stream():
        zq = z_prev[:, _C_Q:_C_Q + ATTN_WIDTH]
        qn = zq * lax.rsqrt(_group_mean(zq * zq, bd_ref) + EPS) * gq_ref[...]
        yield
        cos4 = jnp.concatenate([cos] * 4, axis=1)
        sin4 = jnp.concatenate([sin] * 4, axis=1)
        qr = qn * cos4 + _swap_halves(qn) * sin4
        yield
        qn_t, qr_t = qn.T, qr.T
        for hq in range(N_ATTN_HEADS):
            sl = slice(hq * HEAD_DIM, (hq + 1) * HEAD_DIM)
            qn_ref[0, hq] = qn_t[sl].astype(_bf16)
            qr_ref[0, hq] = qr_t[sl].astype(_bf16)

    def kv_stream():
        def segment(off):
            return z_prev[:, _C_KV + off:_C_KV + off + KV_WIDTH]

        for n, (off, out_ref) in enumerate(((_KV_KC, kc_ref), (_KV_VC, vc_ref))):
            regroup_scr[n] = segment(off)
            for tok in range(CMP_STRIDE):
                every_16th = regroup_scr[n, pl.ds(tok, tm // CMP_STRIDE, stride=CMP_STRIDE), :]
                out_ref[0, :, tok * KV_WIDTH:(tok + 1) * KV_WIDTH] = every_16th.astype(_bf16)
        gate_ref[0] = jax.nn.sigmoid(segment(_KV_GATE))

        ones_block = jnp.where(lax.broadcasted_iota(jnp.int32, (HEAD_DIM, LANES), 0) == 0, 1.0, 0.0)
        for off, out_ref in ((_KV_VSL, vsl_ref), (_KV_VWN, vwn_ref)):
            zv2 = segment(off)
            for tt in range(tm // LANES):
                zt = zv2[tt * LANES:(tt + 1) * LANES, :].T
                for hh in range(N_KV_HEADS):
                    out_ref[0, hh, tt] = jnp.concatenate(
                        [zt[hh * HEAD_DIM:(hh + 1) * HEAD_DIM], ones_block], axis=0).astype(_bf16)
            yield

        lane = lax.broadcasted_iota(jnp.int32, (tm, LANES), 1)
        pos = i * tm + lax.broadcasted_iota(jnp.int32, (tm, LANES), 0)
        blk_onehot = jnp.where(lane - HEAD_DIM == pos // SEL_BLOCK, 1.0, 0.0)
        for off, g_ref, out_ref, extra in ((_KV_KSL, gksl_ref, ksl_ref, blk_onehot),
                                           (_KV_KWN, gkwn_ref, kwn_ref, jnp.zeros((tm, LANES), _f32))):
            zk = segment(off)
            kn = zk * lax.rsqrt(_group_mean(zk * zk, bd_ref) + EPS) * g_ref[...]
            kr = kn * cos + _swap_halves(kn) * sin
            out_ref[0, 0] = jnp.where(lane < HEAD_DIM, kr, extra).astype(_bf16)
            out_ref[0, 1] = jnp.where(lane < HEAD_DIM, pltpu.roll(kr, HEAD_DIM, 1), extra).astype(_bf16)
            yield

    row = lax.broadcasted_iota(jnp.int32, (CHUNK, 2 * CHUNK), 0)
    colw = lax.broadcasted_iota(jnp.int32, (CHUNK, 2 * CHUNK), 1) % CHUNK
    causal_w = colw <= row
    lane_c = lax.broadcasted_iota(jnp.int32, (CHUNK, LANES), 1)

    def gmlp_stream(c):
        rows = slice(c * CHUNK, (c + 1) * CHUNK)
        zu = jax.nn.gelu(z_prev[rows, _C_ZU:_C_ZU + MIX_WIDTH])
        yield
        zv = jax.nn.gelu(z_prev[rows, _C_ZV:_C_ZV + MIX_WIDTH])
        yield
        vn = zv * lax.rsqrt(_group_mean(zv * zv, bd_ref) + EPS) * gsgu_ref[...]
        yield
        sv_parts = []
        for p in range(N_MIX_GROUPS // 2):
            vp = vn[:, p * LANES:(p + 1) * LANES]
            rhs = jnp.concatenate([jnp.where(lane_c < HEAD_DIM, vp, 0.0),
                                   jnp.where(lane_c < HEAD_DIM, 0.0, vp)], axis=0).astype(_bf16)
            w_pair = jnp.where(causal_w, spw_ref[p], jnp.zeros((), _bf16))
            sv_parts.append(_dot(w_pair, rhs))
        yield
        sv = jnp.concatenate(sv_parts, axis=1) + spb_ref[...]
        om = zu * sv
        oms = jnp.mean(om * om, axis=-1, keepdims=True)
        omix_ref[0, rows] = (om * lax.rsqrt(oms + EPS) * gomix_ref[...]).astype(_bf16)

    _interleave([project_stream(), query_stream(), kv_stream()] + [gmlp_stream(c) for c in range(tm // CHUNK)]
                + [prenorm_stream()])


def _proj_call(x, gmix, w_in_p, gq, gksl, gkwn, cos, sin, bd, gsgu, spw, spb, gomix):
    B, S, _ = x.shape
    tm = PROJ_TILE
    nt = S // tm
    n_tiles = B * nt
    const2 = lambda t: (0, 0)
    const3 = lambda t: (0, 0, 0)

    def done(t):
        return jnp.maximum(t - 2, 0)

    tok3 = lambda t: (done(t) // nt, done(t) % nt, 0)
    head4 = lambda t: (done(t) // nt, 0, done(t) % nt, 0)
    out_shape = (
        jax.ShapeDtypeStruct((B, N_ATTN_HEADS, HEAD_DIM, S), _bf16),
        jax.ShapeDtypeStruct((B, N_ATTN_HEADS, HEAD_DIM, S), _bf16),
        jax.ShapeDtypeStruct((B, S // CMP_STRIDE, CMP_STRIDE * KV_WIDTH), _bf16),
        jax.ShapeDtypeStruct((B, S // CMP_STRIDE, CMP_STRIDE * KV_WIDTH), _bf16),
        jax.ShapeDtypeStruct((B, N_KV_HEADS, S, LANES), _bf16),
        jax.ShapeDtypeStruct((B, N_KV_HEADS, S // LANES, LANES, LANES), _bf16),
        jax.ShapeDtypeStruct((B, N_KV_HEADS, S, LANES), _bf16),
        jax.ShapeDtypeStruct((B, N_KV_HEADS, S // LANES, LANES, LANES), _bf16),
        jax.ShapeDtypeStruct((B, S, LANES), _f32),
        jax.ShapeDtypeStruct((B, S, MIX_WIDTH), _bf16),
    )
    q_spec = pl.BlockSpec((1, N_ATTN_HEADS, HEAD_DIM, tm), lambda t: (done(t) // nt, 0, 0, done(t) % nt))
    k_spec = pl.BlockSpec((1, N_KV_HEADS, tm, LANES), head4)
    t_spec = pl.BlockSpec((1, tm, LANES), tok3)
    g_spec = pl.BlockSpec((1, tm // CMP_STRIDE, CMP_STRIDE * KV_WIDTH), tok3)
    vt_spec = pl.BlockSpec((1, N_KV_HEADS, tm // LANES, LANES, LANES),
                           lambda t: (done(t) // nt, 0, done(t) % nt, 0, 0))
    return pl.pallas_call(
        functools.partial(_proj_kernel, nt, n_tiles),
        grid=(n_tiles + 2,),
        in_specs=[
            pl.BlockSpec(memory_space=pl.ANY),
            pl.BlockSpec((1, D_MODEL), const2),
            pl.BlockSpec((D_MODEL, _IN_COLS_PACKED), const2),
            pl.BlockSpec((1, ATTN_WIDTH), const2),
            pl.BlockSpec((1, KV_WIDTH), const2),
            pl.BlockSpec((1, KV_WIDTH), const2),
            pl.BlockSpec((tm, LANES), lambda t: (done(t) % nt, 0)),
            pl.BlockSpec((tm, LANES), lambda t: (done(t) % nt, 0)),
            pl.BlockSpec((MXU_DIM, MXU_DIM), const2),
            pl.BlockSpec((1, MIX_WIDTH), const2),
            pl.BlockSpec((N_MIX_GROUPS // 2, CHUNK, 2 * CHUNK), const3),
            pl.BlockSpec((CHUNK, MIX_WIDTH), const2),
            pl.BlockSpec((1, MIX_WIDTH), const2),
        ],
        out_specs=(q_spec, q_spec, g_spec, g_spec, k_spec, vt_spec, k_spec, vt_spec, t_spec,
                   pl.BlockSpec((1, tm, MIX_WIDTH), tok3)),
        out_shape=out_shape,
        scratch_shapes=[pltpu.VMEM((tm, D_MODEL), _bf16), pltpu.VMEM((tm, D_MODEL), _bf16),
                        pltpu.VMEM((tm, _IN_COLS_PACKED), _f32), pltpu.VMEM((tm, _IN_COLS_PACKED), _f32),
                        pltpu.VMEM((2, tm, KV_WIDTH), _f32),
                        pltpu.VMEM((X_SLOTS, tm, D_MODEL), _f32), pltpu.SemaphoreType.DMA((X_SLOTS,))],
        compiler_params=pltpu.CompilerParams(
            dimension_semantics=("arbitrary",), vmem_limit_bytes=VMEM_LIMIT),
        name="proj",
    )(x.reshape(B * S, D_MODEL), gmix, w_in_p, gq, gksl, gkwn, cos, sin, bd, gsgu, spw, spb, gomix)


def _compress_kernel(kc_ref, vc_ref, pe_ref, w1_ref, w2_ref, gk_ref, kcmp_ref, vcmp_ref):
    n_rows = kc_ref.shape[1]
    zero_half = jnp.zeros((n_rows, HEAD_DIM), _f32)
    for t, (src_ref, out_ref) in enumerate(((kc_ref, kcmp_ref), (vc_ref, vcmp_ref))):
        g = src_ref[0].astype(_f32)
        first = _dot((g + pe_ref[t, 0:1]).astype(_bf16), w1_ref[t, 0])
        second = _dot((g + pe_ref[t, 1:2]).astype(_bf16), w1_ref[t, 1])
        hid = jax.nn.gelu(first + pltpu.roll(second, n_rows - 1, 0)).astype(_bf16)
        heads = [_dot(hid[:, hh * CMP_HIDDEN:(hh + 1) * CMP_HIDDEN], w2_ref[t]) for hh in range(N_KV_HEADS)]
        if t == 0:
            heads = [c * lax.rsqrt(jnp.mean(c * c, axis=-1, keepdims=True) + EPS) * gk_ref[...] for c in heads]
            for hh in range(N_KV_HEADS):
                out_ref[0, hh] = jnp.concatenate([heads[hh], zero_half], axis=1).astype(_bf16)
        else:
            for hh in range(N_KV_HEADS):
                vt = jnp.concatenate([heads[hh], zero_half], axis=1).T
                out_ref[0, hh] = vt.astype(_bf16)


def _compress_call(kc_g, vc_g, pe_rows, w1x, w2, gk0):
    B, n_rows, width = kc_g.shape
    return pl.pallas_call(
        _compress_kernel,
        grid=(B,),
        in_specs=[
            pl.BlockSpec((1, n_rows, width), lambda b: (b, 0, 0)),
            pl.BlockSpec((1, n_rows, width), lambda b: (b, 0, 0)),
            pl.BlockSpec((2, 2, width), lambda b: (0, 0, 0)),
            pl.BlockSpec((2, 2, width, N_KV_HEADS * CMP_HIDDEN), lambda b: (0, 0, 0, 0)),
            pl.BlockSpec((2, CMP_HIDDEN, HEAD_DIM), lambda b: (0, 0, 0)),
            pl.BlockSpec((1, HEAD_DIM), lambda b: (0, 0)),
        ],
        out_specs=(pl.BlockSpec((1, N_KV_HEADS, n_rows, LANES), lambda b: (b, 0, 0, 0)),
                   pl.BlockSpec((1, N_KV_HEADS, n_rows, LANES), lambda b: (b, 0, 0, 0))),
        out_shape=(jax.ShapeDtypeStruct((B, N_KV_HEADS, n_rows, LANES), _bf16),
                   jax.ShapeDtypeStruct((B, N_KV_HEADS, n_rows, LANES), _bf16)),
        compiler_params=pltpu.CompilerParams(
            dimension_semantics=("parallel",), vmem_limit_bytes=VMEM_LIMIT),
        name="compress",
    )(kc_g, vc_g, pe_rows, w1x, w2, gk0)


def _interleave(streams):
    live = list(streams)
    while live:
        for g in list(live):
            try:
                next(g)
            except StopIteration:
                live.remove(g)


def _delayed(stream, units):
    for _ in range(units):
        yield
    yield from stream


def _attn_kernel(qn_ref, qr_ref, kcmp_ref, vcmpt_ref, ksl_ref, vslt_ref, kwn_ref, vwnt_ref, gate_ref,
                 ovt_ref, mask_ref, gout_ref, o_ref):
    refs = (qn_ref, qr_ref, kcmp_ref, vcmpt_ref, ksl_ref, vslt_ref, kwn_ref, vwnt_ref, gate_ref,
            ovt_ref, mask_ref, gout_ref, o_ref)
    S = ksl_ref.shape[2]

    def step_body(c):
        tiles = [_attn_tile(sub, c, *refs) for sub in range(ATTN_SUBTILES)]
        _interleave([_delayed(stream, sub * TILE_SKEW) for sub, (streams, _) in enumerate(tiles)
                     for stream in streams])
        for _, finish in tiles:
            finish()

    cls = (pl.program_id(1) * ATTN_SUBTILES * ATTN_Q_TILE) // SEL_CHUNK
    for c in range(S // SEL_CHUNK):
        pl.when(cls == c)(functools.partial(step_body, c))


def _attn_tile(sub, c, qn_ref, qr_ref, kcmp_ref, vcmpt_ref, ksl_ref, vslt_ref, kwn_ref, vwnt_ref, gate_ref,
               ovt_ref, mask_ref, gout_ref, o_ref):
    tq = ATTN_Q_TILE
    sub_cols = slice(sub * tq, (sub + 1) * tq)
    rows = GQA_GROUP * tq
    S = ksl_ref.shape[2]
    n_cmp_pad = kcmp_ref.shape[2]
    n_sel = S // SEL_BLOCK
    win_tiles = WINDOW // tq
    unit_tiles = ATTN_UNIT // tq
    i = pl.program_id(1) * ATTN_SUBTILES + sub
    q0 = i * tq

    qpos = q0 + lax.broadcasted_iota(jnp.int32, (1, rows), 1) % tq
    blk = lax.broadcasted_iota(jnp.int32, (n_sel, tq), 0)
    tpos = q0 + lax.broadcasted_iota(jnp.int32, (n_sel, tq), 1)
    gates_t = gate_ref[0, sub_cols, :].T
    zero_rows = jnp.zeros((HEAD_DIM, rows), _bf16)

    def stacked_heads_t(ref, h):
        return jnp.concatenate([ref[0, GQA_GROUP * h + g, :, sub_cols] for g in range(GQA_GROUP)], axis=1)

    def tile_mask(key_tile, low_edge=False):
        kind = jnp.where(key_tile < i, MASK_FULL, jnp.where(key_tile == i, MASK_DIAG, MASK_NONE))
        if low_edge:
            kind = jnp.where(i >= win_tiles, MASK_LOW, kind)
        return mask_ref[kind]

    def attend(out, h, k_ref, vt_ref, q_t, first_tile, tile_masks):
        n_tiles = len(tile_masks)
        scores, mx = [], None
        for u0 in range(0, n_tiles, unit_tiles):
            nt = min(unit_tiles, n_tiles - u0)
            if isinstance(first_tile, int):
                keys = slice((first_tile + u0) * tq, (first_tile + u0 + nt) * tq)
            else:
                keys = pl.ds(pl.multiple_of((first_tile + u0) * tq, tq), nt * tq)
            sc = _dot(k_ref[0, h, keys, :], q_t)
            if any(tile_masks[u0 + t] is not None for t in range(nt)):
                sc = jnp.concatenate(
                    [sc[t * tq:(t + 1) * tq] if tile_masks[u0 + t] is None
                     else sc[t * tq:(t + 1) * tq] + tile_masks[u0 + t]() for t in range(nt)], axis=0)
            scores.append(sc)
            cm = jnp.max(sc, axis=0, keepdims=True)
            mx = cm if mx is None else jnp.maximum(mx, cm)
            yield
        probs = []
        for sc in scores:
            probs.append(jnp.exp2(sc - mx).astype(_bf16))
            yield
        vt = jnp.concatenate([vt_ref[0, h, first_tile + t] for t in range(n_tiles)], axis=1)
        acc = _dot(vt, jnp.concatenate(probs, axis=0))
        out[h] = acc[0:HEAD_DIM] / acc[HEAD_DIM:HEAD_DIM + 1]
        yield

    def window_stream(out, h, c):
        first_tile = jnp.maximum(i - win_tiles, 0)
        q_t = jnp.concatenate([stacked_heads_t(qr_ref, h), zero_rows], axis=0)
        if c * SEL_CHUNK >= WINDOW:
            masks = ([lambda: mask_ref[MASK_LOW]] + [None] * (win_tiles - 1) + [lambda: mask_ref[MASK_DIAG]])
        else:
            masks = [functools.partial(tile_mask, first_tile + t, low_edge=(t == 0))
                     for t in range(win_tiles + 1)]
        yield from attend(out, h, kwn_ref, vwnt_ref, q_t, first_tile, masks)

    def compressed_and_selected_stream(o_cmps, o_sels, h, c):
        qn = stacked_heads_t(qn_ref, h)
        qr = stacked_heads_t(qr_ref, h)

        s = _dot(kcmp_ref[0, h], jnp.concatenate([qn, zero_rows], axis=0))
        n_idx = lax.broadcasted_iota(jnp.int32, (n_cmp_pad, 1), 0)
        valid_c = n_idx * CMP_STRIDE + (CMP_BLOCK - 1) <= qpos
        s = jnp.where(valid_c, s, NEG_BIG)
        m = jnp.max(s, axis=0, keepdims=True)
        e = jnp.where(valid_c, jnp.exp2(s - m), 0.0)
        p_c = e / jnp.maximum(jnp.sum(e, axis=0, keepdims=True), 1e-20)
        o_cmps[h] = _dot(vcmpt_ref[0, h], p_c.astype(_bf16))[0:HEAD_DIM]
        yield

        eligible = blk * SEL_BLOCK <= tpos
        if SEL_CHUNK * (c + 1) <= SEL_TOPN * SEL_BLOCK:
            chosen = eligible
        else:
            p_sum = p_c[:, 0:tq] + p_c[:, tq:2 * tq] + p_c[:, 2 * tq:3 * tq] + p_c[:, 3 * tq:4 * tq]
            p_hi = p_sum.astype(_bf16)
            p_lo = (p_sum - p_hi.astype(_f32)).astype(_bf16)
            imp = _dot(ovt_ref[...], p_hi) + _dot(ovt_ref[...], p_lo)
            cur = tpos // SEL_BLOCK
            forced = (blk == 0) | (blk == cur) | (blk == cur - 1)
            score = jnp.where(forced, FORCE_SCORE, jnp.where(eligible, imp, -jnp.inf))
            rank = jnp.zeros((n_sel, tq), _f32)
            for jp in range(n_sel):
                other = score[jp:jp + 1, :]
                tie = jnp.where(blk > jp, 1.0, 0.0)
                rank = rank + jnp.where(other > score, 1.0, jnp.where(other == score, tie, 0.0))
            chosen = (rank < float(SEL_TOPN)) & eligible
        sel_bias = jnp.where(chosen, 0.0, NEG_BIG).astype(_bf16)
        yield

        q_t = jnp.concatenate([qr, jnp.concatenate([sel_bias] * GQA_GROUP, axis=1),
                               jnp.zeros((LANES - HEAD_DIM - n_sel, rows), _bf16)], axis=0)
        n_tiles = (SEL_CHUNK // tq) * (c + 1)
        first_edge = n_tiles - SEL_CHUNK // tq
        masks = [None if t < first_edge else functools.partial(tile_mask, t) for t in range(n_tiles)]
        yield from attend(o_sels, h, ksl_ref, vslt_ref, q_t, 0, masks)

    o_wins, o_cmps, o_sels = {}, {}, {}
    streams = [
        window_stream(o_wins, 0, c),
        compressed_and_selected_stream(o_cmps, o_sels, 0, c),
        _delayed(window_stream(o_wins, 1, c), STREAM_SKEW),
        _delayed(compressed_and_selected_stream(o_cmps, o_sels, 1, c), STREAM_SKEW),
    ]

    def finish():
        slabs = []
        for h in range(N_KV_HEADS):
            for g in range(GQA_GROUP):
                hq = GQA_GROUP * h + g
                cols = slice(g * tq, (g + 1) * tq)
                slabs.append(gates_t[hq:hq + 1] * o_cmps[h][:, cols]
                             + gates_t[N_ATTN_HEADS + hq:N_ATTN_HEADS + hq + 1] * o_sels[h][:, cols]
                             + gates_t[2 * N_ATTN_HEADS + hq:2 * N_ATTN_HEADS + hq + 1] * o_wins[h][:, cols])
        ot = jnp.concatenate(slabs, axis=0)
        ms = jnp.mean(ot * ot, axis=0, keepdims=True)
        o_ref[0, sub_cols, :] = (ot * lax.rsqrt(ms + EPS) * gout_ref[...]).T.astype(_bf16)

    return streams, finish


def _attn_call(qn_t, qr_t, kcmp, vcmp_t, ksl, vsl_t, kwn, vwn_t, gates, ovt, masks, gout_attn):
    B, _, _, S = qn_t.shape
    tq = ATTN_Q_TILE
    assert tq == LANES and S % SEL_CHUNK == 0 and SEL_CHUNK % tq == 0 and WINDOW % tq == 0
    assert ATTN_UNIT % tq == 0 and kcmp.shape[2] == LANES
    step = tq * ATTN_SUBTILES
    assert SEL_CHUNK % step == 0
    n_cmp_pad = kcmp.shape[2]
    n_sel = S // SEL_BLOCK
    rows = GQA_GROUP * tq
    per_b4 = lambda b, i: (b, 0, 0, 0)
    const2 = lambda b, i: (0, 0)
    vt_spec = pl.BlockSpec((1, N_KV_HEADS, S // LANES, LANES, LANES), lambda b, i: (b, 0, 0, 0, 0))
    return pl.pallas_call(
        _attn_kernel,
        grid=(B, S // step),
        in_specs=[
            pl.BlockSpec((1, N_ATTN_HEADS, HEAD_DIM, step), lambda b, i: (b, 0, 0, i)),
            pl.BlockSpec((1, N_ATTN_HEADS, HEAD_DIM, step), lambda b, i: (b, 0, 0, i)),
            pl.BlockSpec((1, N_KV_HEADS, n_cmp_pad, LANES), per_b4),
            pl.BlockSpec((1, N_KV_HEADS, n_cmp_pad, LANES), per_b4),
            pl.BlockSpec((1, N_KV_HEADS, S, LANES), per_b4),
            vt_spec,
            pl.BlockSpec((1, N_KV_HEADS, S, LANES), per_b4),
            vt_spec,
            pl.BlockSpec((1, step, LANES), lambda b, i: (b, i, 0)),
            pl.BlockSpec((n_sel, n_cmp_pad), const2),
            pl.BlockSpec((4, tq, rows), lambda b, i: (0, 0, 0)),
            pl.BlockSpec((ATTN_WIDTH, tq), const2),
        ],
        out_specs=pl.BlockSpec((1, step, ATTN_WIDTH), lambda b, i: (b, i, 0)),
        out_shape=jax.ShapeDtypeStruct((B, S, ATTN_WIDTH), _bf16),
        compiler_params=pltpu.CompilerParams(
            dimension_semantics=("parallel", "arbitrary"), vmem_limit_bytes=VMEM_LIMIT),
        name="attn",
    )(qn_t, qr_t, kcmp, vcmp_t, ksl, vsl_t, kwn, vwn_t, gates, ovt, masks, gout_attn)


def _ffn_kernel(x_ref, oa_ref, om_ref, wo_ref, gffn_ref, w1_ref, w2_ref, out_ref, act_scr):
    x2 = (x_ref[...] + _dot(oa_ref[...], wo_ref[0:ATTN_WIDTH, :])
          + _dot(om_ref[...], wo_ref[ATTN_WIDTH:ATTN_WIDTH + MIX_WIDTH, :]))
    ms = jnp.mean(x2 * x2, axis=-1, keepdims=True)
    h = (x2 * lax.rsqrt(ms + EPS) * gffn_ref[...]).astype(_bf16)
    for c in range(D_FF // FF_CHUNK):
        cols = slice(c * FF_CHUNK, (c + 1) * FF_CHUNK)
        a = jnp.maximum(_dot(h, w1_ref[:, cols]), 0.0)
        act_scr[:, cols] = (a * a).astype(_bf16)
    out_ref[...] = x2 + _dot(act_scr[...], w2_ref[...])


def _ffn_call(x2d, oa, om, wo, gffn, w1, w2):
    T = x2d.shape[0]
    tm = FFN_TILE
    tok = lambda i: (i, 0)
    const = lambda i: (0, 0)
    once = pl.Buffered(1)
    return pl.pallas_call(
        _ffn_kernel,
        grid=(T // tm,),
        in_specs=[
            pl.BlockSpec((tm, D_MODEL), tok),
            pl.BlockSpec((tm, ATTN_WIDTH), tok),
            pl.BlockSpec((tm, MIX_WIDTH), tok),
            pl.BlockSpec((D_MODEL, D_MODEL), const, pipeline_mode=once),
            pl.BlockSpec((1, D_MODEL), const),
            pl.BlockSpec((D_MODEL, D_FF), const, pipeline_mode=once),
            pl.BlockSpec((D_FF, D_MODEL), const, pipeline_mode=once),
        ],
        out_specs=pl.BlockSpec((tm, D_MODEL), tok),
        out_shape=jax.ShapeDtypeStruct((T, D_MODEL), _f32),
        scratch_shapes=[pltpu.VMEM((tm, D_FF), _bf16)],
        compiler_params=pltpu.CompilerParams(
            dimension_semantics=("parallel",), vmem_limit_bytes=VMEM_LIMIT),
        name="ffn",
    )(x2d, oa, om, wo, gffn, w1, w2)


def _constants(S):
    half = HEAD_DIM // 2
    inv = ROPE_THETA ** (-jnp.arange(half, dtype=_f32) / half)
    ang = jnp.arange(S, dtype=_f32)[:, None] * inv[None, :]
    cos = jnp.cos(ang)
    sin = jnp.sin(ang)
    cos2 = jnp.concatenate([cos, cos, cos, cos], axis=1)
    sin2 = jnp.concatenate([-sin, sin, -sin, sin], axis=1)
    mlane = np.arange(MXU_DIM)
    bd = (mlane[:, None] // HEAD_DIM == mlane[None, :] // HEAD_DIM).astype(np.float32) / HEAD_DIM
    n_cmp = (S - CMP_BLOCK) // CMP_STRIDE + 1
    n_sel = S // SEL_BLOCK
    n_cmp_pad = S // CMP_STRIDE
    cmp_start = np.arange(n_cmp)[:, None] * CMP_STRIDE
    sel_start = np.arange(n_sel)[None, :] * SEL_BLOCK
    overlap = np.clip(np.minimum(cmp_start + CMP_BLOCK, sel_start + SEL_BLOCK)
                      - np.maximum(cmp_start, sel_start), 0, None).astype(np.float32) / CMP_BLOCK
    ovt = np.zeros((n_sel, n_cmp_pad), np.float32)
    ovt[:, :n_cmp] = overlap.T
    qq = (np.arange(GQA_GROUP * ATTN_Q_TILE) % ATTN_Q_TILE)[None, :]
    kk = np.arange(ATTN_Q_TILE)[:, None]
    masks = np.zeros((4, ATTN_Q_TILE, GQA_GROUP * ATTN_Q_TILE), np.float32)
    masks[MASK_DIAG] = np.where(kk <= qq, 0.0, NEG_BIG)
    masks[MASK_LOW] = np.where(kk > qq, 0.0, NEG_BIG)
    masks[MASK_NONE] = NEG_BIG
    return cos2, sin2, jnp.asarray(bd, _bf16), jnp.asarray(ovt, _bf16), jnp.asarray(masks, _f32)


def _pack_w_in(w):
    w = w.astype(_bf16)
    gate0 = ATTN_WIDTH + 6 * KV_WIDTH
    n_g = N_GATES * N_ATTN_HEADS
    gates = w[:, gate0:gate0 + n_g].reshape(D_MODEL, N_ATTN_HEADS, N_GATES)
    gates = jnp.transpose(gates, (0, 2, 1)).reshape(D_MODEL, n_g)
    pad = jnp.zeros((D_MODEL, _C_KV + _KV_GROUP - gate0 - n_g), w.dtype)
    return jnp.concatenate([w[:, :gate0], gates, pad, w[:, gate0 + n_g:]], axis=1)


def _expand_cmp_w1(w1):
    w = w1.reshape(2, CMP_STRIDE, HEAD_DIM, CMP_HIDDEN)
    z = jnp.zeros_like(w)
    h0 = jnp.concatenate([w, z], axis=-1)
    h1 = jnp.concatenate([z, w], axis=-1)
    both = jnp.stack([h0, h1], axis=2)
    return both.reshape(2, CMP_STRIDE * KV_WIDTH, N_KV_HEADS * CMP_HIDDEN).astype(_bf16)


def _expand_pe(pe):
    p = pe.reshape(2, CMP_STRIDE, 1, HEAD_DIM)
    return jnp.broadcast_to(p, (2, CMP_STRIDE, N_KV_HEADS, HEAD_DIM)).reshape(2, CMP_STRIDE * KV_WIDTH)


def kernel(x, g_mix_norm, w_in, g_q, g_k, cmp_pe, cmp_w1, cmp_w2, g_sgu, sp_w, sp_b, g_out, w_out,
           g_ffn_norm, w_ff1, w_ff2):
    B, S, _ = x.shape
    l = 0
    scale = float(HEAD_DIM ** -0.5 * np.log2(np.e))
    cos2, sin2, bd, ovt, masks = _constants(S)

    w_in_p = _pack_w_in(w_in[l])
    gq = (jnp.tile(g_q[l], N_ATTN_HEADS) * scale)[None, :]
    gksl = jnp.tile(g_k[l, 1], N_KV_HEADS)[None, :]
    gkwn = jnp.tile(g_k[l, 2], N_KV_HEADS)[None, :]
    gsgu = g_sgu[l].reshape(1, MIX_WIDTH)
    spw = sp_w[l].reshape(N_MIX_GROUPS // 2, 2, CHUNK, CHUNK)
    spw = jnp.transpose(spw, (0, 2, 1, 3)).reshape(N_MIX_GROUPS // 2, CHUNK, 2 * CHUNK).astype(_bf16)
    spb = jnp.repeat(sp_b[l].T, HEAD_DIM, axis=1)
    gout = g_out[l]

    qn, qr, kc, vc, ksl, vsl, kwn, vwn, gates, omix = _proj_call(
        x, g_mix_norm[l][None, :], w_in_p, gq, gksl, gkwn, cos2, sin2, bd, gsgu, spw, spb,
        gout[None, ATTN_WIDTH:])

    pe_rows = jnp.stack([_expand_pe(cmp_pe[l, 0]), _expand_pe(cmp_pe[l, 1])])
    w1x = jnp.stack([_expand_cmp_w1(cmp_w1[l, 0]), _expand_cmp_w1(cmp_w1[l, 1])])
    kcmp, vcmp = _compress_call(kc, vc, pe_rows, w1x, cmp_w2[l].astype(_bf16), g_k[l, 0][None, :])

    oattn = _attn_call(qn, qr, kcmp, vcmp, ksl, vsl, kwn, vwn, gates, ovt, masks,
                       jnp.broadcast_to(gout[:ATTN_WIDTH, None], (ATTN_WIDTH, ATTN_Q_TILE)))

    out = _ffn_call(x.reshape(B * S, D_MODEL), oattn.reshape(B * S, ATTN_WIDTH),
                    omix.reshape(B * S, MIX_WIDTH), w_out[l].astype(_bf16), g_ffn_norm[l][None, :],
                    w_ff1[l].astype(_bf16), w_ff2[l].astype(_bf16))
    return out.reshape(B, S, D_MODEL)
```
